```python
import math
import jax
import jax.numpy as jnp
from jax import lax
import numpy as np

D_MODEL = 2048
BATCH = 4
SEQ = 4096
DEPTH = 2

GRID_W = 64
CTX_LEN = 256
EPS = 1e-6
N_MOD = 6
HY_WIDTH = D_MODEL // 4
HY_ORDER = 2
HY_BANDS = 16
HY_EMB = 1 + 2 * HY_BANDS
HY_HIDDEN = 64
HY_FAST_DECAY = 0.3
HY_SLOW_DECAY = 1.5
HY_TARGET = 1e-2
RET_HEADS = 8
RET_V_W = D_MODEL // 2
RET_DV = RET_V_W // RET_HEADS
RET_DK = RET_DV // 2
RET_QK_W = RET_HEADS * RET_DK
RET_CHUNK = 128
S5_WIDTH = D_MODEL // 4
S5_GROUP = 16
S5_GROUPS = S5_WIDTH // S5_GROUP
S5_STATE = 64
D_FF = 256 * ((8 * D_MODEL // 3 + 255) // 256)
IN_WIDTH = 3 * HY_WIDTH + 2 * RET_QK_W + 2 * RET_V_W + S5_WIDTH
MIX_WIDTH = HY_WIDTH + RET_V_W + S5_WIDTH
F32 = jnp.float32

kernel_name = 'hybrid_hyena_retention_s5_dit'


def _rmsnorm(x, g):
    xf = x.astype(F32)
    y = xf * lax.rsqrt(jnp.mean(xf * xf, axis=-1, keepdims=True) + EPS)
    return (y * g.astype(F32)).astype(x.dtype)


def _modulate(x, g, shift, scale):
    return _rmsnorm(x, g) * (1.0 + scale) + shift


def _split_proj(a):
    h3 = 3 * HY_WIDTH
    cuts = [h3, h3 + RET_QK_W, h3 + 2 * RET_QK_W, h3 + 2 * RET_QK_W + RET_V_W,
            h3 + 2 * RET_QK_W + 2 * RET_V_W]
    return jnp.split(a, cuts, axis=-1)


def _short_conv(x, w, b):
    xp = jnp.pad(x, ((0, 0), (1, 1), (0, 0)))
    return xp[:, :-2] * w[0] + xp[:, 1:-1] * w[1] + xp[:, 2:] * w[2] + b


def _hyena_filters(length, w1, b1, w2, b2, w3, freq):
    idx = jnp.arange(length, dtype=F32)
    t = (idx / max(length - 1, 1))[:, None]
    bands = jnp.linspace(1e-4, HY_BANDS - 1, HY_BANDS, dtype=F32)
    ang = (2.0 * math.pi * idx / length)[:, None] * bands[None]
    z = jnp.concatenate([t, jnp.cos(ang), -jnp.sin(ang)], axis=-1)
    freq = freq.astype(F32)
    h = jnp.sin(freq[0] * (z @ w1.astype(F32) + b1.astype(F32)))
    h = jnp.sin(freq[1] * (h @ w2.astype(F32) + b2.astype(F32)))
    h = (h @ w3.astype(F32)).reshape(length, 2, HY_ORDER, HY_WIDTH)
    max_decay = math.log(HY_TARGET) / HY_FAST_DECAY
    min_decay = math.log(HY_TARGET) / HY_SLOW_DECAY
    deltas = jnp.linspace(min_decay, max_decay, HY_WIDTH, dtype=F32)
    h = h * jnp.exp(-t * jnp.abs(deltas))[:, None, None, :]
    k_full = jnp.concatenate([h[:, 0], jnp.zeros_like(h[:1, 0]), jnp.flip(h[1:, 1], axis=0)], axis=0)
    k_full = k_full / jnp.sum(jnp.abs(k_full), axis=0, keepdims=True)
    return jnp.fft.rfft(k_full, axis=0)


def _fft_conv(u, k_f):
    length = u.shape[1]
    u_f = jnp.fft.rfft(u, n=2 * length, axis=1)
    return jnp.fft.irfft(u_f * k_f[None], n=2 * length, axis=1)[:, :length]


def _hyena(p, conv_w, conv_b, w1, b1, w2, b2, w3, freq, bias):
    dtype = p.dtype
    length = p.shape[1]
    p = _short_conv(p, conv_w, conv_b).astype(F32)
    v, x1, x2 = jnp.split(p, 3, axis=-1)
    k_f = _hyena_filters(length, w1, b1, w2, b2, w3, freq)
    bias = bias.astype(F32)
    z = x1 * (_fft_conv(v, k_f[:, 0]) + bias[0] * v)
    y = x2 * (_fft_conv(z, k_f[:, 1]) + bias[1] * z)
    return y.astype(dtype)


def _heads(t, d):
    b, l, _ = t.shape
    return t.astype(F32).reshape(b, l, RET_HEADS, d).transpose(0, 2, 1, 3)


def _retention_dir(q, k, v, log_gamma, init, strict):
    bsz, nh, length, dk = q.shape
    dv = v.shape[-1]
    nc = length // RET_CHUNK
    q = q.reshape(bsz, nh, nc, RET_CHUNK, dk)
    k = k.reshape(bsz, nh, nc, RET_CHUNK, dk)
    v = v.reshape(bsz, nh, nc, RET_CHUNK, dv)
    pos = jnp.arange(RET_CHUNK, dtype=F32)
    diff = pos[:, None] - pos[None, :]
    keep = (diff > 0) if strict else (diff >= 0)
    decay_in = jnp.where(keep, jnp.exp(jnp.maximum(diff, 0.0) * log_gamma[:, None, None]), 0.0)
    scores = jnp.einsum('bhncd,bhnmd->bhncm', q, k) * decay_in[:, None]
    inner = jnp.einsum('bhncm,bhnme->bhnce', scores, v)
    k_dec = k * jnp.exp((RET_CHUNK - 1.0 - pos) * log_gamma[:, None])[:, None, :, None]
    kv = jnp.einsum('bhnmd,bhnme->bhnde', k_dec, v)
    chunk_decay = jnp.exp(RET_CHUNK * log_gamma)[:, None, None]

    def step(state, kv_n):
        return chunk_decay * state + kv_n, state

    final, s_in = lax.scan(step, init, jnp.moveaxis(kv, 2, 0))
    s_in = jnp.moveaxis(s_in, 0, 2)
    q_dec = q * jnp.exp((pos + 1.0) * log_gamma[:, None])[:, None, :, None]
    cross = jnp.einsum('bhncd,bhnde->bhnce', q_dec, s_in)
    return (inner + cross).reshape(bsz, nh, length, dv), final


def _retention_scan(q, k, v, log_gamma, init_f, init_b):
    y_f, s_f = _retention_dir(q, k, v, log_gamma[0], init_f, False)
    fl = lambda t: jnp.flip(t, axis=2)
    y_b, s_b = _retention_dir(fl(q), fl(k), fl(v), log_gamma[1], init_b, True)
    return y_f + fl(y_b), s_f, s_b


def _retention_out(y, g):
    y = y * lax.rsqrt(jnp.mean(y * y, axis=-1, keepdims=True) + EPS)
    b, h, l, dv = y.shape
    y = y.transpose(0, 2, 1, 3).reshape(b, l, h * dv).astype(g.dtype)
    return jax.nn.silu(g) * y


def _s5_params(lam_re, lam_im, log_step, b_re, b_im, c_re, c_im):
    lam = lax.complex(jnp.minimum(lam_re.astype(F32), -1e-4), lam_im.astype(F32))
    step = jnp.exp(log_step.astype(F32))
    b_mat = lax.complex(b_re.astype(F32), b_im.astype(F32))
    c_mat = lax.complex(c_re.astype(F32), c_im.astype(F32))
    return lam, step, b_mat, c_mat


def _s5_combine(e1, e2):
    a1, b1 = e1
    a2, b2 = e2
    return a1 * a2, a2 * b1 + b2


def _s5_dir(u, lam, step, b_mat, c_mat, init):
    lam_dt = lam * step[:, None]
    lam_bar = jnp.exp(lam_dt)
    b_bar = ((lam_bar - 1.0) / lam)[:, :, None] * b_mat
    bu = jnp.einsum('blgi,gpi->blgp', u, b_bar)
    a = jnp.broadcast_to(lam_bar, bu.shape)
    _, xs = lax.associative_scan(_s5_combine, (a, bu), axis=1)
    if init is not None:
        n = jnp.arange(1, u.shape[1] + 1, dtype=F32)
        xs = xs + jnp.exp(n[:, None, None] * lam_dt)[None] * init[:, None]
    y = jnp.einsum('blgp,gip->blgi', xs, c_mat).real
    return y, xs[:, -1]


def _s5_scan(u, lam, step, b_mat, c_mat, init_f, init_b):
    bsz, l, _ = u.shape
    ug = u.astype(F32).reshape(bsz, l, S5_GROUPS, S5_GROUP).astype(jnp.complex64)
    y_f, s_f = _s5_dir(ug, lam[0], step[0], b_mat[0], c_mat[0], init_f)
    y_b, s_b = _s5_dir(jnp.flip(ug, axis=1), lam[1], step[1], b_mat[1], c_mat[1], init_b)
    y = (y_f + jnp.flip(y_b, axis=1)).reshape(bsz, l, S5_WIDTH)
    return y, s_f, s_b


def _s5_out(y, u, d, glu_w, glu_b):
    z = jax.nn.gelu((y + d.astype(F32) * u.astype(F32)).astype(u.dtype))
    return z * jax.nn.sigmoid(z @ glu_w + glu_b)


def _conv_ffn(h, w_up, conv_w, conv_b, w_down, rows, width):
    bsz, l, _ = h.shape
    gate, val = jnp.split(h @ w_up, 2, axis=-1)
    gate = gate.reshape(bsz, rows, width, D_FF)
    gate = lax.conv_general_dilated(gate, conv_w[:, :, None, :], (1, 1), 'SAME',
                                    dimension_numbers=('NHWC', 'HWIO', 'NHWC'),
                                    feature_group_count=D_FF) + conv_b
    gate = gate.reshape(bsz, l, D_FF)
    return (jax.nn.gelu(gate) * val) @ w_down


def setup_inputs(seed: int = 0) -> dict:
    key = jax.random.key(seed)
    ks = iter(jax.random.split(key, 48))
    nrm = lambda shape, std: std * jax.random.normal(next(ks), shape, F32)
    D = D_MODEL
    x = nrm((BATCH, SEQ, D), 1.0)
    c = nrm((BATCH, D), 1.0)
    ctx = nrm((BATCH, CTX_LEN, D), 1.0)
    c_ctx = nrm((D,), 1.0)
    ada_w = nrm((DEPTH, D, N_MOD * D), 0.5 * D ** -0.5)
    ada_b = nrm((DEPTH, N_MOD * D), 0.01)
    norm1_g = 1.0 + nrm((DEPTH, D), 0.02)
    w_in = nrm((DEPTH, D, IN_WIDTH), D ** -0.5)
    hy_conv_w = nrm((DEPTH, 3, 3 * HY_WIDTH), 3 ** -0.5)
    hy_conv_b = nrm((DEPTH, 3 * HY_WIDTH), 0.01)
    hy_w1 = nrm((DEPTH, HY_EMB, HY_HIDDEN), HY_EMB ** -0.5)
    hy_b1 = nrm((DEPTH, HY_HIDDEN), 0.1)
    hy_w2 = nrm((DEPTH, HY_HIDDEN, HY_HIDDEN), HY_HIDDEN ** -0.5)
    hy_b2 = nrm((DEPTH, HY_HIDDEN), 0.1)
    hy_w3 = nrm((DEPTH, HY_HIDDEN, 2 * HY_ORDER * HY_WIDTH), HY_HIDDEN ** -0.5)
    hy_freq = 1.0 + nrm((DEPTH, 2, HY_HIDDEN), 0.02)
    hy_bias = nrm((DEPTH, HY_ORDER, HY_WIDTH), 0.5)
    gamma = 1.0 - 2.0 ** (-5.0 - jnp.arange(RET_HEADS, dtype=F32))
    ret_decay = jnp.log(-jnp.log(gamma))[None, None] + nrm((DEPTH, 2, RET_HEADS), 0.01)
    s5_shape = (DEPTH, 2, S5_GROUPS, S5_STATE)
    s5_lam_re = -0.5 + nrm(s5_shape, 0.01)
    s5_lam_im = jnp.broadcast_to(math.pi * jnp.arange(S5_STATE, dtype=F32), s5_shape) + nrm(s5_shape, 0.01)
    s5_log_step = jax.random.uniform(next(ks), (DEPTH, 2, S5_GROUPS), F32, math.log(1e-3), math.log(1e-1))
    s5_b_re = nrm((DEPTH, 2, S5_GROUPS, S5_STATE, S5_GROUP), (2 * S5_GROUP) ** -0.5)
    s5_b_im = nrm((DEPTH, 2, S5_GROUPS, S5_STATE, S5_GROUP), (2 * S5_GROUP) ** -0.5)
    s5_c_re = nrm((DEPTH, 2, S5_GROUPS, S5_GROUP, S5_STATE), 0.35)
    s5_c_im = nrm((DEPTH, 2, S5_GROUPS, S5_GROUP, S5_STATE), 0.35)
    s5_d = nrm((DEPTH, S5_WIDTH), 1.0)
    s5_glu_w = nrm((DEPTH, S5_WIDTH, S5_WIDTH), S5_WIDTH ** -0.5)
    s5_glu_b = nrm((DEPTH, S5_WIDTH), 0.01)
    w_out = nrm((DEPTH, MIX_WIDTH, D), MIX_WIDTH ** -0.5)
    norm2_g = 1.0 + nrm((DEPTH, D), 0.02)
    ffn_w_up = nrm((DEPTH, D, 2 * D_FF), D ** -0.5)
    ffn_conv_w = nrm((DEPTH, 3, 3, D_FF), 1.0 / 3.0)
    ffn_conv_b = nrm((DEPTH, D_FF), 0.01)
    ffn_w_down = nrm((DEPTH, D_FF, D), D_FF ** -0.5)
    norm_f = 1.0 + nrm((D,), 0.02)
    return {'x': x, 'c': c, 'ctx': ctx, 'c_ctx': c_ctx, 'ada_w': ada_w, 'ada_b': ada_b,
            'norm1_g': norm1_g, 'w_in': w_in, 'hy_conv_w': hy_conv_w, 'hy_conv_b': hy_conv_b,
            'hy_w1': hy_w1, 'hy_b1': hy_b1, 'hy_w2': hy_w2, 'hy_b2': hy_b2, 'hy_w3': hy_w3,
            'hy_freq': hy_freq, 'hy_bias': hy_bias, 'ret_decay': ret_decay,
            's5_lam_re': s5_lam_re, 's5_lam_im': s5_lam_im, 's5_log_step': s5_log_step,
            's5_b_re': s5_b_re, 's5_b_im': s5_b_im, 's5_c_re': s5_c_re, 's5_c_im': s5_c_im,
            's5_d': s5_d, 's5_glu_w': s5_glu_w, 's5_glu_b': s5_glu_b, 'w_out': w_out,
            'norm2_g': norm2_g, 'ffn_w_up': ffn_w_up, 'ffn_conv_w': ffn_conv_w,
            'ffn_conv_b': ffn_conv_b, 'ffn_w_down': ffn_w_down, 'norm_f': norm_f}


def reference(x, c, ctx, c_ctx, ada_w, ada_b, norm1_g, w_in, hy_conv_w, hy_conv_b,
              hy_w1, hy_b1, hy_w2, hy_b2, hy_w3, hy_freq, hy_bias, ret_decay,
              s5_lam_re, s5_lam_im, s5_log_step, s5_b_re, s5_b_im, s5_c_re, s5_c_im,
              s5_d, s5_glu_w, s5_glu_b, w_out, norm2_g, ffn_w_up, ffn_conv_w,
              ffn_conv_b, ffn_w_down, norm_f):
    bsz, length, _ = x.shape
    rows = length // GRID_W
    ctx_len = ctx.shape[1]
    h_lat, h_ctx = x, ctx
    for i in range(DEPTH):
        last = i == DEPTH - 1
        m_lat = jnp.split((jax.nn.silu(c) @ ada_w[i] + ada_b[i])[:, None, :], N_MOD, axis=-1)
        m_ctx = jnp.split(jax.nn.silu(c_ctx) @ ada_w[i] + ada_b[i], N_MOD, axis=-1)
        a_lat = _modulate(h_lat, norm1_g[i], m_lat[0], m_lat[1]) @ w_in[i]
        a_ctx = _modulate(h_ctx, norm1_g[i], m_ctx[0], m_ctx[1]) @ w_in[i]
        hy_l, q_l, k_l, v_l, g_l, u_l = _split_proj(a_lat)
        hy_c, q_c, k_c, v_c, g_c, u_c = _split_proj(a_ctx)
        log_gamma = -jnp.exp(ret_decay[i].astype(F32))
        lam, step, b_mat, c_mat = _s5_params(s5_lam_re[i], s5_lam_im[i], s5_log_step[i],
                                             s5_b_re[i], s5_b_im[i], s5_c_re[i], s5_c_im[i])
        zero = jnp.zeros((bsz, RET_HEADS, RET_DK, RET_DV), F32)
        r_c, rs_f, rs_b = _retention_scan(_heads(q_c, RET_DK), _heads(k_c, RET_DK) * RET_DK ** -0.5,
                                          _heads(v_c, RET_DV), log_gamma, zero, zero)
        y5_c, ss_f, ss_b = _s5_scan(u_c, lam, step, b_mat, c_mat, None, None)
        r_l, _, _ = _retention_scan(_heads(q_l, RET_DK), _heads(k_l, RET_DK) * RET_DK ** -0.5,
                                    _heads(v_l, RET_DV), log_gamma, rs_f, rs_b)
        y5_l, _, _ = _s5_scan(u_l, lam, step, b_mat, c_mat, ss_f, ss_b)
        hyena_l = _hyena(hy_l, hy_conv_w[i], hy_conv_b[i], hy_w1[i], hy_b1[i], hy_w2[i], hy_b2[i],
                         hy_w3[i], hy_freq[i], hy_bias[i])
        mix_l = jnp.concatenate([hyena_l, _retention_out(r_l, g_l),
                                 _s5_out(y5_l, u_l, s5_d[i], s5_glu_w[i], s5_glu_b[i])], axis=-1) @ w_out[i]
        if not last:
            hyena_c = _hyena(hy_c, hy_conv_w[i], hy_conv_b[i], hy_w1[i], hy_b1[i], hy_w2[i], hy_b2[i],
                             hy_w3[i], hy_freq[i], hy_bias[i])
            mix_c = jnp.concatenate([hyena_c, _retention_out(r_c, g_c),
                                     _s5_out(y5_c, u_c, s5_d[i], s5_glu_w[i], s5_glu_b[i])], axis=-1) @ w_out[i]
            h_ctx = h_ctx + m_ctx[2] * mix_c
            h_ctx = h_ctx + m_ctx[5] * _conv_ffn(_modulate(h_ctx, norm2_g[i], m_ctx[3], m_ctx[4]),
                                                 ffn_w_up[i], ffn_conv_w[i], ffn_conv_b[i], ffn_w_down[i],
                                                 1, ctx_len)
        h_lat = h_lat + m_lat[2] * mix_l
        h_lat = h_lat + m_lat[5] * _conv_ffn(_modulate(h_lat, norm2_g[i], m_lat[3], m_lat[4]),
                                             ffn_w_up[i], ffn_conv_w[i], ffn_conv_b[i], ffn_w_down[i],
                                             rows, GRID_W)
    return _rmsnorm(h_lat, norm_f)
```

```python
import functools
import math

import numpy as np
import jax
import jax.numpy as jnp
from jax import lax
from jax.experimental import pallas as pl
from jax.experimental.pallas import tpu as pltpu

F32 = jnp.float32
BF16 = jnp.bfloat16
HIGHEST = lax.Precision.HIGHEST

EPS = 1e-6
N_MOD = 6
GRID_W = 64
LANES = 128
SUBLANES = 8
VMEM_LIMIT_MB = 56

HY_ORDER = 2
HY_BANDS = 16
HY_FAST_DECAY = 0.3
HY_SLOW_DECAY = 1.5
HY_TARGET = 1e-2
RET_HEADS = 8
RET_CHUNK = 256
S5_GROUP = 16
S5_STATE = 64
S5_CHUNK = 16


def _params(sem, vmem_mb=VMEM_LIMIT_MB):
    return pltpu.CompilerParams(dimension_semantics=sem, vmem_limit_bytes=vmem_mb << 20)


def _const_spec(shape):
    nd = len(shape)
    return pl.BlockSpec(shape, lambda *_: (0,) * nd)


def _mod_kernel(c_ref, w_ref, b_ref, o_ref):
    s = jax.nn.silu(c_ref[...])
    o_ref[0] = jnp.dot(s, w_ref[0], preferred_element_type=F32, precision=HIGHEST) + b_ref[0]


def _modulation(cc, ada_w, ada_b):
    depth, d, n = ada_w.shape
    bn = 1536
    return pl.pallas_call(
        _mod_kernel,
        grid=(depth, n // bn),
        in_specs=[pl.BlockSpec((8, d), lambda i, j: (0, 0)),
                  pl.BlockSpec((1, d, bn), lambda i, j: (i, 0, j)),
                  pl.BlockSpec((1, 1, bn), lambda i, j: (i, 0, j))],
        out_specs=pl.BlockSpec((1, 8, bn), lambda i, j: (i, 0, j)),
        out_shape=jax.ShapeDtypeStruct((depth, 8, n), F32),
        compiler_params=_params(("parallel", "parallel")),
        name="adaln_mod",
    )(cc, ada_w, ada_b.reshape(depth, 1, n))


def _modmm_kernel(h_ref, g_ref, sh_ref, sc_ref, w_ref, o_ref, xm_ref):
    @pl.when(pl.program_id(1) == 0)
    def _():
        x = h_ref[...]
        y = x * lax.rsqrt(jnp.mean(x * x, axis=-1, keepdims=True) + EPS)
        y = y * g_ref[...]
        xm_ref[...] = (y * (1.0 + sc_ref[0]) + sh_ref[0]).astype(BF16)

    o_ref[...] = jnp.dot(xm_ref[...], w_ref[...], preferred_element_type=F32).astype(o_ref.dtype)


def _modmm(h, g, shift, scale, w, rows_per_mod, bm, bn):
    r, d = h.shape
    n = w.shape[1]
    bm = min(bm, r)
    mod_idx = lambda i, j: ((i * bm) // rows_per_mod, 0, 0)
    return pl.pallas_call(
        _modmm_kernel,
        grid=(r // bm, n // bn),
        in_specs=[pl.BlockSpec((bm, d), lambda i, j: (i, 0)),
                  pl.BlockSpec((1, d), lambda i, j: (0, 0)),
                  pl.BlockSpec((1, 1, d), mod_idx),
                  pl.BlockSpec((1, 1, d), mod_idx),
                  pl.BlockSpec((d, bn), lambda i, j: (0, j))],
        out_specs=pl.BlockSpec((bm, bn), lambda i, j: (i, j)),
        out_shape=jax.ShapeDtypeStruct((r, n), BF16),
        scratch_shapes=[pltpu.VMEM((bm, d), BF16)],
        compiler_params=_params(("parallel", "arbitrary")),
        name="modmm",
    )(h, g.reshape(1, d), shift, scale, w)


def _resmm_kernel(*refs, n_in):
    x_refs = refs[:n_in]
    w_ref, h_ref, gate_ref, o_ref = refs[n_in:]
    off = 0
    acc = None
    for x_ref in x_refs:
        k = x_ref.shape[1]
        part = jnp.dot(x_ref[...].astype(BF16), w_ref[off:off + k, :], preferred_element_type=F32)
        acc = part if acc is None else acc + part
        off += k
    o_ref[...] = h_ref[...] + gate_ref[0] * acc


def _resmm(xs, w, h, gate, rows_per_mod, bm, bn):
    r, n = h.shape
    bm = min(bm, r)
    k = w.shape[0]
    in_specs = [pl.BlockSpec((bm, x.shape[1]), lambda i, j: (i, 0)) for x in xs]
    in_specs += [pl.BlockSpec((k, bn), lambda i, j: (0, j)),
                 pl.BlockSpec((bm, bn), lambda i, j: (i, j)),
                 pl.BlockSpec((1, 1, bn), lambda i, j: ((i * bm) // rows_per_mod, 0, j))]
    return pl.pallas_call(
        functools.partial(_resmm_kernel, n_in=len(xs)),
        grid=(r // bm, n // bn),
        in_specs=in_specs,
        out_specs=pl.BlockSpec((bm, bn), lambda i, j: (i, j)),
        out_shape=jax.ShapeDtypeStruct((r, n), F32),
        compiler_params=_params(("parallel", "arbitrary")),
        name="resmm",
    )(*xs, w, h, gate)


def _shortconv_kernel(x_ref, w_ref, b_ref, o_ref, pad_ref, *, length, chunk):
    cb = x_ref.shape[-1]
    zeros = jnp.zeros((8, cb), F32)
    pad_ref[0:8, :] = zeros
    pad_ref[length + 8:length + 16, :] = zeros
    pad_ref[8:length + 8, :] = x_ref[0].astype(F32)
    w0, w1, w2 = w_ref[0:1, :], w_ref[1:2, :], w_ref[2:3, :]
    b = b_ref[...]
    for c in range(length // chunk):
        r = c * chunk
        o_ref[0, r:r + chunk, :] = (pad_ref[r + 7:r + 7 + chunk, :] * w0 + pad_ref[r + 8:r + 8 + chunk, :] * w1
                                    + pad_ref[r + 9:r + 9 + chunk, :] * w2 + b)


def _shortconv(a, w, b, width):
    bsz, length, _ = a.shape
    cb = 256
    chunk = min(512, length)
    return pl.pallas_call(
        functools.partial(_shortconv_kernel, length=length, chunk=chunk),
        grid=(bsz, width // cb),
        in_specs=[pl.BlockSpec((1, length, cb), lambda i, j: (i, 0, j)),
                  pl.BlockSpec((3, cb), lambda i, j: (0, j)),
                  pl.BlockSpec((1, cb), lambda i, j: (0, j))],
        out_specs=pl.BlockSpec((1, length, cb), lambda i, j: (i, 0, j)),
        out_shape=jax.ShapeDtypeStruct((bsz, length, width), F32),
        scratch_shapes=[pltpu.VMEM((length + 16, cb), F32)],
        compiler_params=_params(("parallel", "parallel")),
        name="hy_shortconv",
    )(a, w, b.reshape(1, width))


def _taps_kernel(z_ref, w1_ref, b1_ref, w2_ref, b2_ref, w3_ref, f_ref, dl_ref, o_ref, *, length):
    z = z_ref[...]
    h = jnp.sin(f_ref[0:1, :] * (jnp.dot(z, w1_ref[...], preferred_element_type=F32, precision=HIGHEST)
                                  + b1_ref[...]))
    h = jnp.sin(f_ref[1:2, :] * (jnp.dot(h, w2_ref[...], preferred_element_type=F32, precision=HIGHEST)
                                  + b2_ref[...]))
    h = jnp.dot(h, w3_ref[...], preferred_element_type=F32, precision=HIGHEST)
    t = z[:, 0:1]
    h = h * jnp.exp(-t * jnp.abs(dl_ref[...]))
    rb = z.shape[0]
    row = lax.broadcasted_iota(jnp.int32, h.shape, 0) + pl.program_id(0) * rb
    o_ref[...] = jnp.where(row == length, 0.0, h)


def _hyena_taps(length, w1, b1, w2, b2, w3, freq, width):
    n = 2 * length
    hid = w1.shape[1]
    pos = jnp.arange(n)
    idx = jnp.where(pos < length, pos, n - pos).astype(F32)
    t = (idx / max(length - 1, 1))[:, None]
    bands = jnp.linspace(1e-4, HY_BANDS - 1, HY_BANDS, dtype=F32)
    ang = (2.0 * math.pi * idx / length)[:, None] * bands[None]
    emb = 1 + 2 * HY_BANDS
    z = jnp.concatenate([t, jnp.cos(ang), -jnp.sin(ang), jnp.zeros((n, 64 - emb), F32)], axis=-1)
    w1p = jnp.concatenate([w1.astype(F32), jnp.zeros((64 - emb, hid), F32)], axis=0)
    max_decay = math.log(HY_TARGET) / HY_FAST_DECAY
    min_decay = math.log(HY_TARGET) / HY_SLOW_DECAY
    deltas = jnp.tile(jnp.linspace(min_decay, max_decay, width, dtype=F32), HY_ORDER)[None]
    oc = HY_ORDER * width
    rb = min(1024, length)
    half = length // rb
    return pl.pallas_call(
        functools.partial(_taps_kernel, length=length),
        grid=(n // rb,),
        in_specs=[pl.BlockSpec((rb, 64), lambda i: (i, 0)),
                  _const_spec((64, hid)), _const_spec((1, hid)),
                  _const_spec((hid, hid)), _const_spec((1, hid)),
                  pl.BlockSpec((hid, oc), lambda i: (0, i // half)),
                  _const_spec((2, hid)), _const_spec((1, oc))],
        out_specs=pl.BlockSpec((rb, oc), lambda i: (i, 0)),
        out_shape=jax.ShapeDtypeStruct((n, oc), F32),
        compiler_params=_params(("parallel",)),
        name="hy_taps",
    )(z, w1p, b1.reshape(1, hid).astype(F32), w2.astype(F32), b2.reshape(1, hid).astype(F32),
      w3.astype(F32), freq.astype(F32), deltas)


def _bf16(x):
    return jnp.asarray(x).astype(BF16)


def _split_bf16(x):
    x = jnp.asarray(x)
    hi = x.astype(BF16)
    return hi, (x - hi.astype(F32)).astype(BF16)


@functools.lru_cache(maxsize=None)
def _dft2_consts(length):
    n = 2 * length
    n2 = LANES
    n1 = n // n2
    hf = n1 // 2
    j = np.arange(n2)[:, None, None]
    k1 = np.arange(n1)[None, :, None]
    m1 = np.arange(n1)[None, None, :]
    ph = -2.0 * np.pi * (j * k1 / n + (m1 * k1 % n1) / n1)
    mr, mi = np.cos(ph), np.sin(ph)
    g1 = np.concatenate([np.concatenate([mr[:, :, :hf], -mi[:, :, :hf]], 2),
                         np.concatenate([mi[:, :, :hf], mr[:, :, :hf]], 2)], 1)
    g1f = np.concatenate([mr, mi], 1)
    mrt = np.transpose(mr, (0, 2, 1))[:, :hf] / n
    mit = -np.transpose(mi, (0, 2, 1))[:, :hf] / n
    g1i = np.concatenate([np.concatenate([mrt, -mit], 2),
                          np.concatenate([mit, mrt], 2)], 1)
    a = np.arange(n2)
    ph2 = -2.0 * np.pi * ((a[:, None] * a[None, :]) % n2) / n2
    fr, fi = np.cos(ph2), np.sin(ph2)
    g2 = np.block([[fr, -fi], [fi, fr]])
    g2i = np.block([[fr, fi], [-fi, fr]])
    f32 = lambda m: np.asarray(m, np.float32)
    return dict(g1=f32(g1), g1i=f32(g1i), g2=f32(g2), g2i=f32(g2i), g1f=f32(g1f))


@functools.lru_cache(maxsize=None)
def _dft1_consts(length):
    n = 2 * length
    a = np.arange(n)
    ph = -2.0 * np.pi * ((a[:, None] * a[None, :]) % n) / n
    fr, fi = np.cos(ph), np.sin(ph)
    gf = np.block([[fr[:, :length], -fi[:, :length]], [fi[:, :length], fr[:, :length]]])
    gi = np.block([[fr[:length], fi[:length]], [-fi[:length], fr[:length]]]) / n
    gff = np.concatenate([fr, fi], 0)
    f32 = lambda m: np.asarray(m, np.float32)
    return dict(gf=f32(gf), gi=f32(gi), gff=f32(gff))


def _dot3(g_hi, g_lo, x):
    x_hi = x.astype(BF16)
    x_lo = (x - x_hi.astype(F32)).astype(BF16)
    return (jnp.dot(g_hi, x_hi, preferred_element_type=F32) + jnp.dot(g_hi, x_lo, preferred_element_type=F32)
            + jnp.dot(g_lo, x_hi, preferred_element_type=F32))


def _fspec2_kernel(k_ref, g1h_ref, g1l_ref, g2h_ref, g2l_ref, kr_ref, ki_ref, *, n1):
    inv = 1.0 / jnp.sum(jnp.abs(k_ref[...]), axis=0, keepdims=True)

    def s1(j, c):
        x = k_ref[pl.ds(j, n1, stride=LANES), :]
        a = _dot3(g1h_ref[j], g1l_ref[j], x)
        kr_ref[pl.ds(j, n1, stride=LANES), :] = a[:n1]
        ki_ref[pl.ds(j, n1, stride=LANES), :] = a[n1:]
        return c

    lax.fori_loop(0, LANES, s1, 0)

    def s2(k1, c):
        r0 = pl.multiple_of(k1 * LANES, LANES)
        a = jnp.concatenate([kr_ref[pl.ds(r0, LANES), :], ki_ref[pl.ds(r0, LANES), :]], axis=0)
        x = _dot3(g2h_ref[...], g2l_ref[...], a) * inv
        kr_ref[pl.ds(r0, LANES), :] = x[:LANES]
        ki_ref[pl.ds(r0, LANES), :] = x[LANES:]
        return c

    lax.fori_loop(0, n1, s2, 0)


def _filter_spectrum2(taps):
    n, oc = taps.shape
    cst = _dft2_consts(n // 2)
    n1 = n // LANES
    cb = LANES
    g1h, g1l = _split_bf16(cst["g1f"])
    g2h, g2l = _split_bf16(cst["g2"])
    out = jax.ShapeDtypeStruct((n, oc), F32)
    return pl.pallas_call(
        functools.partial(_fspec2_kernel, n1=n1),
        grid=(oc // cb,),
        in_specs=[pl.BlockSpec((n, cb), lambda i: (0, i)),
                  _const_spec(g1h.shape), _const_spec(g1l.shape),
                  _const_spec(g2h.shape), _const_spec(g2l.shape)],
        out_specs=[pl.BlockSpec((n, cb), lambda i: (0, i))] * 2,
        out_shape=[out, out],
        compiler_params=_params(("parallel",)),
        name="hy_fspec2",
    )(taps, g1h, g1l, g2h, g2l)


def _fspec1_kernel(k_ref, gh_ref, gl_ref, kr_ref, ki_ref):
    k = k_ref[...]
    n = k.shape[0]
    inv = 1.0 / jnp.sum(jnp.abs(k), axis=0, keepdims=True)
    x = _dot3(gh_ref[...], gl_ref[...], k) * inv
    kr_ref[...] = x[:n]
    ki_ref[...] = x[n:]


def _filter_spectrum1(taps):
    n, oc = taps.shape
    gh, gl = _split_bf16(_dft1_consts(n // 2)["gff"])
    cb = 256
    out = jax.ShapeDtypeStruct((n, oc), F32)
    return pl.pallas_call(
        _fspec1_kernel,
        grid=(oc // cb,),
        in_specs=[pl.BlockSpec((n, cb), lambda i: (0, i)), _const_spec(gh.shape), _const_spec(gl.shape)],
        out_specs=[pl.BlockSpec((n, cb), lambda i: (0, i))] * 2,
        out_shape=[out, out],
        compiler_params=_params(("parallel",)),
        name="hy_fspec1",
    )(taps, gh, gl)


def _hyconv2_kernel(v_ref, x_ref, kr_ref, ki_ref, bias_ref, g1_ref, g1i_ref, g2_ref, g2i_ref, o_ref,
                    ar_ref, ai_ref, *, n1):
    hf = n1 // 2

    def s1(j, c):
        xa = v_ref[0, pl.ds(j, hf, stride=LANES), :]
        xb = v_ref[1, pl.ds(j, hf, stride=LANES), :]
        x = jnp.concatenate([xa, xb], axis=0).astype(BF16)
        a = jnp.dot(g1_ref[j], x, preferred_element_type=F32)
        ar_ref[pl.ds(j, n1, stride=LANES), :] = a[:n1]
        ai_ref[pl.ds(j, n1, stride=LANES), :] = a[n1:]
        return c

    lax.fori_loop(0, LANES, s1, 0)

    def s2(k1, c):
        r0 = pl.multiple_of(k1 * LANES, LANES)
        a = jnp.concatenate([ar_ref[pl.ds(r0, LANES), :], ai_ref[pl.ds(r0, LANES), :]], axis=0).astype(BF16)
        x = jnp.dot(g2_ref[...], a, preferred_element_type=F32)
        xr, xi = x[:LANES], x[LANES:]
        kr = kr_ref[pl.ds(r0, LANES), :]
        ki = ki_ref[pl.ds(r0, LANES), :]
        y = jnp.concatenate([xr * kr - xi * ki, xr * ki + xi * kr], axis=0).astype(BF16)
        b = jnp.dot(g2i_ref[...], y, preferred_element_type=F32)
        ar_ref[pl.ds(r0, LANES), :] = b[:LANES]
        ai_ref[pl.ds(r0, LANES), :] = b[LANES:]
        return c

    lax.fori_loop(0, n1, s2, 0)

    def s3(j, c):
        b = jnp.concatenate([ar_ref[pl.ds(j, n1, stride=LANES), :], ai_ref[pl.ds(j, n1, stride=LANES), :]],
                            axis=0).astype(BF16)
        y = jnp.dot(g1i_ref[j], b, preferred_element_type=F32)
        o_ref[0, pl.ds(j, hf, stride=LANES), :] = y[:hf]
        o_ref[1, pl.ds(j, hf, stride=LANES), :] = y[hf:]
        return c

    lax.fori_loop(0, LANES, s3, 0)
    bias = bias_ref[...]
    for b in range(2):
        o_ref[b] = x_ref[b] * (o_ref[b] + bias * v_ref[b])


def _hyconv2(va, v_col, xa, x_col, kr, ki, k_col, bias, length):
    bsz = va.shape[0]
    width = bias.shape[-1]
    cst = _dft2_consts(length)
    n = 2 * length
    n1 = n // LANES
    cb = LANES
    ncb = width // cb
    return pl.pallas_call(
        functools.partial(_hyconv2_kernel, n1=n1),
        grid=(ncb, bsz // 2),
        in_specs=[pl.BlockSpec((2, length, cb), lambda c, q: (q, 0, v_col + c)),
                  pl.BlockSpec((2, length, cb), lambda c, q: (q, 0, x_col + c)),
                  pl.BlockSpec((n, cb), lambda c, q: (0, k_col + c), pipeline_mode=pl.Buffered(1)),
                  pl.BlockSpec((n, cb), lambda c, q: (0, k_col + c), pipeline_mode=pl.Buffered(1)),
                  pl.BlockSpec((1, cb), lambda c, q: (0, c)),
                  _const_spec(cst["g1"].shape), _const_spec(cst["g1i"].shape),
                  _const_spec(cst["g2"].shape), _const_spec(cst["g2i"].shape)],
        out_specs=pl.BlockSpec((2, length, cb), lambda c, q: (q, 0, c)),
        out_shape=jax.ShapeDtypeStruct((bsz, length, width), F32),
        scratch_shapes=[pltpu.VMEM((n, cb), F32), pltpu.VMEM((n, cb), F32)],
        compiler_params=_params(("parallel", "arbitrary")),
        name="hy_conv2",
    )(va, xa, kr, ki, bias.reshape(1, width), _bf16(cst["g1"]), _bf16(cst["g1i"]), _bf16(cst["g2"]),
      _bf16(cst["g2i"]))


def _hyconv1_kernel(v_ref, x_ref, kr_ref, ki_ref, bias_ref, gf_ref, gi_ref, o_ref):
    length = v_ref.shape[1]
    n = 2 * length
    x = jnp.concatenate([v_ref[0], v_ref[1]], axis=0).astype(BF16)
    s = jnp.dot(gf_ref[...], x, preferred_element_type=F32)
    sr, si = s[:n], s[n:]
    kr, ki = kr_ref[...], ki_ref[...]
    y = jnp.concatenate([sr * kr - si * ki, sr * ki + si * kr], axis=0).astype(BF16)
    out = jnp.dot(gi_ref[...], y, preferred_element_type=F32)
    bias = bias_ref[...]
    for b in range(2):
        o_ref[b] = x_ref[b] * (out[b * length:(b + 1) * length] + bias * v_ref[b])


def _hyconv1(va, v_col, xa, x_col, kr, ki, k_col, bias, length):
    bsz = va.shape[0]
    width = bias.shape[-1]
    cst = _dft1_consts(length)
    n = 2 * length
    cb = LANES
    return pl.pallas_call(
        _hyconv1_kernel,
        grid=(width // cb, bsz // 2),
        in_specs=[pl.BlockSpec((2, length, cb), lambda c, q: (q, 0, v_col + c)),
                  pl.BlockSpec((2, length, cb), lambda c, q: (q, 0, x_col + c)),
                  pl.BlockSpec((n, cb), lambda c, q: (0, k_col + c)),
                  pl.BlockSpec((n, cb), lambda c, q: (0, k_col + c)),
                  pl.BlockSpec((1, cb), lambda c, q: (0, c)),
                  _const_spec(cst["gf"].shape), _const_spec(cst["gi"].shape)],
        out_specs=pl.BlockSpec((2, length, cb), lambda c, q: (q, 0, c)),
        out_shape=jax.ShapeDtypeStruct((bsz, length, width), F32),
        compiler_params=_params(("parallel", "arbitrary")),
        name="hy_conv1",
    )(va, xa, kr, ki, bias.reshape(1, width), _bf16(cst["gf"]), _bf16(cst["gi"]))


def _hyena(a, conv_w, conv_b, w1, b1, w2, b2, w3, freq, bias, width):
    bsz, length, _ = a.shape
    p = _shortconv(a, conv_w, conv_b, 3 * width)
    taps = _hyena_taps(length, w1, b1, w2, b2, w3, freq, width)
    two_stage = (2 * length) % (LANES * 16) == 0
    kr, ki = (_filter_spectrum2 if two_stage else _filter_spectrum1)(taps)
    conv = _hyconv2 if two_stage else _hyconv1
    ncb = width // LANES
    z = conv(p, 0, p, ncb, kr, ki, 0, bias[0], length)
    return conv(z, 0, p, 2 * ncb, kr, ki, ncb, bias[1], length)


def _ret_kernel(lg_ref, q_ref, k_ref, v_ref, g_ref, s0_ref, o_ref, sfin_ref, sb_ref, *, length, chunk, dk):
    hp = pl.program_id(1)
    nc = length // chunk
    dv = LANES
    row = lax.broadcasted_iota(jnp.int32, (chunk, chunk), 0)
    col = lax.broadcasted_iota(jnp.int32, (chunk, chunk), 1)
    diff = (row - col).astype(F32)
    lane = lax.broadcasted_iota(jnp.int32, (chunk, 2 * dk), 1)
    pos = lax.broadcasted_iota(jnp.int32, (chunk, 2 * dk), 0).astype(F32)
    ones_s = jnp.ones((2 * dk, dv), F32)
    kscale = dk ** -0.5
    for hh in range(2):
        h = hp * 2 + hh
        lgf = lg_ref[0, h]
        lgb = lg_ref[1, h]
        decay = jnp.where(diff >= 0.0, jnp.exp(jnp.maximum(diff, 0.0) * lgf),
                          jnp.exp(jnp.maximum(-diff, 0.0) * lgb))
        qmask = (lane >= dk * hh) & (lane < dk * (hh + 1))
        qf_dec = jnp.exp((pos + 1.0) * lgf)
        qb_dec = jnp.exp((chunk - pos) * lgb)
        kf_dec = jnp.exp((chunk - 1.0 - pos) * lgf) * kscale
        kb_dec = jnp.exp(pos * lgb) * kscale
        cdf = jnp.exp(ones_s * (chunk * lgf))
        cdb = jnp.exp(ones_s * (chunk * lgb))
        vs = slice(dv * hh, dv * (hh + 1))

        def bstep(i, state):
            n = nc - 1 - i
            r0 = pl.multiple_of(n * chunk, chunk)
            sb_ref[hh, n] = state
            k = k_ref[0, pl.ds(r0, chunk), :].astype(F32)
            v = v_ref[0, pl.ds(r0, chunk), vs]
            inc = lax.dot_general((k * kb_dec).astype(BF16), v, (((0,), (0,)), ((), ())),
                                  preferred_element_type=F32)
            return cdb * state + inc

        sfin_ref[0, 1, hh] = lax.fori_loop(0, nc, bstep, s0_ref[0, 1, hh])

        def fstep(n, state):
            r0 = pl.multiple_of(n * chunk, chunk)
            q = jnp.where(qmask, q_ref[0, pl.ds(r0, chunk), :].astype(F32), 0.0)
            k = k_ref[0, pl.ds(r0, chunk), :].astype(F32)
            v = v_ref[0, pl.ds(r0, chunk), vs]
            s = lax.dot_general(q.astype(BF16), (k * kscale).astype(BF16), (((1,), (1,)), ((), ())),
                                preferred_element_type=F32)
            y = jnp.dot((s * decay).astype(BF16), v, preferred_element_type=F32)
            y = y + jnp.dot((q * qf_dec).astype(BF16), state.astype(BF16), preferred_element_type=F32)
            y = y + jnp.dot((q * qb_dec).astype(BF16), sb_ref[hh, n].astype(BF16), preferred_element_type=F32)
            y = y * lax.rsqrt(jnp.mean(y * y, axis=-1, keepdims=True) + EPS)
            g = g_ref[0, pl.ds(r0, chunk), vs].astype(F32)
            o_ref[0, pl.ds(r0, chunk), vs] = (jax.nn.silu(g) * y).astype(o_ref.dtype)
            inc = lax.dot_general((k * kf_dec).astype(BF16), v, (((0,), (0,)), ((), ())),
                                  preferred_element_type=F32)
            return cdf * state + inc

        sfin_ref[0, 0, hh] = lax.fori_loop(0, nc, fstep, s0_ref[0, 0, hh])


def _retention(a, log_gamma, s0, q_off, dk):
    bsz, length, _ = a.shape
    heads = RET_HEADS
    dv = 2 * dk
    assert dv == LANES
    chunk = min(RET_CHUNK, length)
    qb = q_off // (2 * dk)
    kb = qb + heads // 2
    vb = (q_off + 2 * heads * dk) // (2 * dv)
    gb = vb + heads // 2
    seq = lambda blk, off: pl.BlockSpec((1, length, blk), lambda b, h, lg: (b, 0, off + h))
    st = pl.BlockSpec((1, 2, 2, 2 * dk, dv), lambda b, h, lg: (b, 0, h, 0, 0))
    grid_spec = pltpu.PrefetchScalarGridSpec(
        num_scalar_prefetch=1,
        grid=(bsz, heads // 2),
        in_specs=[seq(2 * dk, qb), seq(2 * dk, kb), seq(2 * dv, vb), seq(2 * dv, gb), st],
        out_specs=[pl.BlockSpec((1, length, 2 * dv), lambda b, h, lg: (b, 0, h)), st],
        scratch_shapes=[pltpu.VMEM((2, length // chunk, 2 * dk, dv), F32)],
    )
    return pl.pallas_call(
        functools.partial(_ret_kernel, length=length, chunk=chunk, dk=dk),
        grid_spec=grid_spec,
        out_shape=[jax.ShapeDtypeStruct((bsz, length, heads * dv), BF16),
                   jax.ShapeDtypeStruct(s0.shape, F32)],
        compiler_params=_params(("parallel", "parallel")),
        name="retention",
    )(log_gamma, a, a, a, a, s0)


def _s5_mats(lam_re, lam_im, log_step, b_re, b_im, c_re, c_im):
    t_len = S5_CHUNK
    lr = jnp.minimum(lam_re.astype(F32), -1e-4)
    li = lam_im.astype(F32)
    step = jnp.exp(log_step.astype(F32))[..., None]
    dr, di = lr * step, li * step
    d = jnp.arange(t_len + 1, dtype=F32)[:, None, None, None]
    mag = jnp.exp(d * dr)
    pr, pi = mag * jnp.cos(d * di), mag * jnp.sin(d * di)
    nr, ni = pr[1] - 1.0, pi[1]
    den = lr * lr + li * li
    cr, ci = (nr * lr + ni * li) / den, (ni * lr - nr * li) / den
    bbr = cr[..., None] * b_re - ci[..., None] * b_im
    bbi = cr[..., None] * b_im + ci[..., None] * b_re
    clr = c_re[None] * pr[:, :, :, None, :] - c_im[None] * pi[:, :, :, None, :]
    cli = c_re[None] * pi[:, :, :, None, :] + c_im[None] * pr[:, :, :, None, :]
    kern = (jnp.einsum('dxgop,xgpi->dxgoi', clr, bbr, precision=HIGHEST)
            - jnp.einsum('dxgop,xgpi->dxgoi', cli, bbi, precision=HIGHEST))
    s_i = jnp.arange(t_len)[:, None]
    t_i = jnp.arange(t_len)[None, :]
    lag = t_i - s_i
    tf = jnp.where((lag >= 0)[:, :, None, None, None], kern[jnp.clip(lag, 0, t_len), 0], 0.0)
    tb = jnp.where((lag <= 0)[:, :, None, None, None], kern[jnp.clip(-lag, 0, t_len), 1], 0.0)
    groups = lr.shape[1]
    toe = jnp.transpose(tf + tb, (2, 0, 4, 1, 3)).reshape(groups, t_len * S5_GROUP, t_len * S5_GROUP)
    sf = jnp.arange(t_len - 1, -1, -1)
    sb = jnp.arange(t_len)
    ef_r = pr[sf, 0][..., None] * bbr[0][None] - pi[sf, 0][..., None] * bbi[0][None]
    ef_i = pr[sf, 0][..., None] * bbi[0][None] + pi[sf, 0][..., None] * bbr[0][None]
    eb_r = pr[sb, 1][..., None] * bbr[1][None] - pi[sb, 1][..., None] * bbi[1][None]
    eb_i = pr[sb, 1][..., None] * bbi[1][None] + pi[sb, 1][..., None] * bbr[1][None]
    pk = lambda e: jnp.transpose(e, (1, 0, 3, 2)).reshape(groups, t_len * S5_GROUP, S5_STATE)
    tfw = jnp.arange(1, t_len + 1)
    tbw = jnp.arange(t_len, 0, -1)
    qk = lambda e: jnp.transpose(e, (1, 3, 0, 2)).reshape(groups, S5_STATE, t_len * S5_GROUP)
    qf_r, qf_i = qk(clr[tfw, 0]), qk(-cli[tfw, 0])
    qb_r, qb_i = qk(clr[tbw, 1]), qk(-cli[tbw, 1])

    def pair_cols(m):
        g, r, c = m.shape
        m = m.reshape(g // 2, 2, r, c)
        z = jnp.zeros_like(m[:, 0])
        return jnp.concatenate([jnp.concatenate([m[:, 0], z], 2), jnp.concatenate([z, m[:, 1]], 2)], 1)

    w1 = jnp.concatenate([pair_cols(toe), pair_cols(pk(ef_r)), pair_cols(pk(ef_i)),
                          pair_cols(pk(eb_r)), pair_cols(pk(eb_i))], axis=2)
    w2 = jnp.concatenate([pair_cols(qf_r), pair_cols(qf_i), pair_cols(qb_r), pair_cols(qb_i)], axis=1)
    pl2 = lambda e: e.reshape(groups // 2, 2 * S5_STATE)
    lam_t = jnp.stack([pl2(pr[t_len, 0]), pl2(pi[t_len, 0]), pl2(pr[t_len, 1]), pl2(pi[t_len, 1])], axis=1)
    return w1.astype(BF16), w2.astype(BF16), lam_t


def _s5_kernel(u_ref, w1_ref, w2_ref, lam_ref, s0_ref, y_ref, sfin_ref, r_ref, *, nc):
    sw = 2 * S5_STATE
    yw = 2 * S5_CHUNK * S5_GROUP
    r_ref[...] = jnp.dot(u_ref[0], w1_ref[0], preferred_element_type=F32)
    lfr, lfi, lbr, lbi = (jnp.broadcast_to(lam_ref[0, i:i + 1, :], (SUBLANES, sw)) for i in range(4))
    cols = [slice(yw + i * sw, yw + (i + 1) * sw) for i in range(4)]

    def step(n, carry):
        fr, fi, br, bi = carry
        rf = pl.ds(pl.multiple_of(n * SUBLANES, SUBLANES), SUBLANES)
        rb = pl.ds(pl.multiple_of((nc - 1 - n) * SUBLANES, SUBLANES), SUBLANES)
        efr, efi = r_ref[rf, cols[0]], r_ref[rf, cols[1]]
        ebr, ebi = r_ref[rb, cols[2]], r_ref[rb, cols[3]]
        r_ref[rf, cols[0]] = fr
        r_ref[rf, cols[1]] = fi
        r_ref[rb, cols[2]] = br
        r_ref[rb, cols[3]] = bi
        return (lfr * fr - lfi * fi + efr, lfr * fi + lfi * fr + efi,
                lbr * br - lbi * bi + ebr, lbr * bi + lbi * br + ebi)

    fin = lax.fori_loop(0, nc, step, tuple(s0_ref[0, i] for i in range(4)))
    for i in range(4):
        sfin_ref[0, i] = fin[i]
    xin = r_ref[:, yw:yw + 4 * sw].astype(BF16)
    y_ref[0] = r_ref[:, 0:yw] + jnp.dot(xin, w2_ref[0], preferred_element_type=F32)


def _s5(u, w1, w2, lam_t, s0):
    bsz, length, width = u.shape
    assert bsz <= SUBLANES
    t_len = S5_CHUNK
    nc = length // t_len
    pairs = width // (2 * S5_GROUP)
    pw = 2 * t_len * S5_GROUP
    up = u.reshape(bsz, nc, t_len, pairs, 2, S5_GROUP).transpose(3, 1, 0, 4, 2, 5)
    up = jnp.pad(up, ((0, 0), (0, 0), (0, SUBLANES - bsz), (0, 0), (0, 0), (0, 0)))
    up = up.reshape(pairs, nc * SUBLANES, pw)
    st = pl.BlockSpec((1, 4, SUBLANES, 2 * S5_STATE), lambda p: (p, 0, 0, 0))
    y, sfin = pl.pallas_call(
        functools.partial(_s5_kernel, nc=nc),
        grid=(pairs,),
        in_specs=[pl.BlockSpec((1, nc * SUBLANES, pw), lambda p: (p, 0, 0)),
                  pl.BlockSpec((1,) + w1.shape[1:], lambda p: (p, 0, 0)),
                  pl.BlockSpec((1,) + w2.shape[1:], lambda p: (p, 0, 0)),
                  pl.BlockSpec((1, 4, 2 * S5_STATE), lambda p: (p, 0, 0)),
                  st],
        out_specs=[pl.BlockSpec((1, nc * SUBLANES, pw), lambda p: (p, 0, 0)), st],
        out_shape=[jax.ShapeDtypeStruct((pairs, nc * SUBLANES, pw), F32), jax.ShapeDtypeStruct(s0.shape, F32)],
        scratch_shapes=[pltpu.VMEM((nc * SUBLANES, pw + 8 * S5_STATE), F32)],
        compiler_params=_params(("parallel",)),
        name="s5_scan",
    )(up, w1, w2, lam_t, s0)
    y = y.reshape(pairs, nc, SUBLANES, 2, t_len, S5_GROUP)[:, :, :bsz]
    y = y.transpose(2, 1, 4, 0, 3, 5).reshape(bsz, length, width)
    return y, sfin


def _s5out_kernel(y_ref, u_ref, d_ref, w_ref, b_ref, o_ref):
    z = jax.nn.gelu(y_ref[...] + d_ref[...] * u_ref[...].astype(F32))
    gate = jnp.dot(z.astype(BF16), w_ref[...], preferred_element_type=F32) + b_ref[...]
    o_ref[...] = (z * jax.nn.sigmoid(gate)).astype(o_ref.dtype)


def _s5_out(y, a, u_col, d, glu_w, glu_b):
    r, width = y.shape
    bm = min(1024, r)
    return pl.pallas_call(
        _s5out_kernel,
        grid=(r // bm,),
        in_specs=[pl.BlockSpec((bm, width), lambda i: (i, 0)),
                  pl.BlockSpec((bm, width), lambda i: (i, u_col)),
                  _const_spec((1, width)), _const_spec((width, width)), _const_spec((1, width))],
        out_specs=pl.BlockSpec((bm, width), lambda i: (i, 0)),
        out_shape=jax.ShapeDtypeStruct((r, width), BF16),
        compiler_params=_params(("parallel",)),
        name="s5_out",
    )(y, a, d.reshape(1, width).astype(F32), glu_w.astype(BF16), glu_b.reshape(1, width).astype(F32))


def _ffnact_kernel(g_ref, v_ref, w_ref, b_ref, o_ref, pad_ref, *, rows, width, chunk):
    length = rows * width
    cb = g_ref.shape[-1]
    off = width + 8
    zeros = jnp.zeros((off, cb), F32)
    pad_ref[0:off, :] = zeros
    pad_ref[off + length:2 * off + length, :] = zeros
    pad_ref[off:off + length, :] = g_ref[0].astype(F32)
    col = lax.broadcasted_iota(jnp.int32, (chunk, cb), 0) % width
    not_first = col != 0
    not_last = col != width - 1
    bias = b_ref[...]
    for c in range(length // chunk):
        r = c * chunk
        acc = None
        for dw in range(3):
            part = None
            for dr in range(3):
                if rows == 1 and dr != 1:
                    continue
                s = off + r + (dr - 1) * width + (dw - 1)
                term = pad_ref[s:s + chunk, :] * w_ref[3 * dr + dw:3 * dr + dw + 1, :]
                part = term if part is None else part + term
            if dw == 0:
                part = jnp.where(not_first, part, 0.0)
            elif dw == 2:
                part = jnp.where(not_last, part, 0.0)
            acc = part if acc is None else acc + part
        gate = jax.nn.gelu(acc + bias)
        o_ref[0, r:r + chunk, :] = (gate * v_ref[0, r:r + chunk, :].astype(F32)).astype(o_ref.dtype)


def _ffn_act(gv, conv_w, conv_b, rows, width):
    nimg, length, f2 = gv.shape
    dff = f2 // 2
    cb = 256
    chunk = min(512, length)
    return pl.pallas_call(
        functools.partial(_ffnact_kernel, rows=rows, width=width, chunk=chunk),
        grid=(nimg, dff // cb),
        in_specs=[pl.BlockSpec((1, length, cb), lambda i, j: (i, 0, j)),
                  pl.BlockSpec((1, length, cb), lambda i, j: (i, 0, dff // cb + j)),
                  pl.BlockSpec((9, cb), lambda i, j: (0, j)),
                  pl.BlockSpec((1, cb), lambda i, j: (0, j))],
        out_specs=pl.BlockSpec((1, length, cb), lambda i, j: (i, 0, j)),
        out_shape=jax.ShapeDtypeStruct((nimg, length, dff), BF16),
        scratch_shapes=[pltpu.VMEM((length + 2 * (width + 8), cb), F32)],
        compiler_params=_params(("parallel", "parallel")),
        name="ffn_act",
    )(gv, gv, conv_w.reshape(9, dff).astype(F32), conv_b.reshape(1, dff).astype(F32))


def _rms_kernel(x_ref, g_ref, o_ref):
    x = x_ref[...]
    o_ref[...] = x * lax.rsqrt(jnp.mean(x * x, axis=-1, keepdims=True) + EPS) * g_ref[...]


def _rmsnorm(x, g):
    r, d = x.shape
    bm = min(1024, r)
    return pl.pallas_call(
        _rms_kernel,
        grid=(r // bm,),
        in_specs=[pl.BlockSpec((bm, d), lambda i: (i, 0)), _const_spec((1, d))],
        out_specs=pl.BlockSpec((bm, d), lambda i: (i, 0)),
        out_shape=jax.ShapeDtypeStruct((r, d), F32),
        compiler_params=_params(("parallel",)),
        name="final_norm",
    )(x, g.reshape(1, d).astype(F32))


def _mixer(h, mods, i, p, states, img_rows, img_width, full):
    bsz, length, d = h.shape
    r = bsz * length
    rpm = r // mods.shape[0]
    width = d // 4
    dk = d // 2 // RET_HEADS // 2
    h2 = h.reshape(r, d)
    a = _modmm(h2, p['norm1_g'][i], mods[:, 0], mods[:, 1], p['w_in'][i], rpm, 1024, 1024)
    a3 = a.reshape(bsz, length, -1)
    q_off = 3 * width
    u_off = q_off + 2 * RET_HEADS * dk + 2 * (d // 2)
    ret, ret_fin = _retention(a3, p['log_gamma'][i], states[0], q_off, dk)
    y5, s5_fin = _s5(a3[:, :, u_off:], *p['s5_mats'][i], states[1])
    if not full:
        return None, (ret_fin, s5_fin)
    hy = _hyena(a3, p['hy_conv_w'][i], p['hy_conv_b'][i], p['hy_w1'][i], p['hy_b1'][i], p['hy_w2'][i],
                p['hy_b2'][i], p['hy_w3'][i], p['hy_freq'][i], p['hy_bias'][i], width)
    s5o = _s5_out(y5.reshape(r, width), a, u_off // width, p['s5_d'][i], p['s5_glu_w'][i], p['s5_glu_b'][i])
    h2 = _resmm([hy.reshape(r, width), ret.reshape(r, d // 2), s5o], p['w_out'][i], h2, mods[:, 2], rpm, 512, 2048)
    gv = _modmm(h2, p['norm2_g'][i], mods[:, 3], mods[:, 4], p['ffn_w_up'][i], rpm, 1024, 1024)
    act = _ffn_act(gv.reshape(bsz, length, -1), p['ffn_conv_w'][i], p['ffn_conv_b'][i], img_rows, img_width)
    h2 = _resmm([act.reshape(r, -1)], p['ffn_w_down'][i], h2, mods[:, 5], rpm, 1024, 512)
    return h2.reshape(bsz, length, d), (ret_fin, s5_fin)


def kernel(x, c, ctx, c_ctx, ada_w, ada_b, norm1_g, w_in, hy_conv_w, hy_conv_b, hy_w1, hy_b1, hy_w2, hy_b2,
           hy_w3, hy_freq, hy_bias, ret_decay, s5_lam_re, s5_lam_im, s5_log_step, s5_b_re, s5_b_im, s5_c_re,
           s5_c_im, s5_d, s5_glu_w, s5_glu_b, w_out, norm2_g, ffn_w_up, ffn_conv_w, ffn_conv_b, ffn_w_down,
           norm_f):
    bsz, length, d = x.shape
    depth = ada_w.shape[0]
    ctx_len = ctx.shape[1]
    dk = d // 2 // RET_HEADS // 2
    pairs = d // 4 // (2 * S5_GROUP)

    cc = jnp.concatenate([c_ctx[None], c, jnp.zeros((8 - 1 - bsz, d), F32)], axis=0)
    mods = _modulation(cc, ada_w, ada_b).reshape(depth, 8, N_MOD, 1, d)
    p = dict(norm1_g=norm1_g, norm2_g=norm2_g, hy_conv_w=hy_conv_w, hy_conv_b=hy_conv_b, hy_w1=hy_w1,
             hy_b1=hy_b1, hy_w2=hy_w2, hy_b2=hy_b2, hy_w3=hy_w3, hy_freq=hy_freq, hy_bias=hy_bias,
             s5_d=s5_d, s5_glu_w=s5_glu_w, s5_glu_b=s5_glu_b, ffn_conv_w=ffn_conv_w, ffn_conv_b=ffn_conv_b,
             w_in=w_in.astype(BF16), w_out=w_out.astype(BF16), ffn_w_up=ffn_w_up.astype(BF16),
             ffn_w_down=ffn_w_down.astype(BF16),
             log_gamma=-jnp.exp(ret_decay.astype(F32)),
             s5_mats=[_s5_mats(s5_lam_re[i], s5_lam_im[i], s5_log_step[i], s5_b_re[i], s5_b_im[i],
                               s5_c_re[i], s5_c_im[i]) for i in range(depth)])
    zero_states = (jnp.zeros((bsz, 2, RET_HEADS, 2 * dk, 4 * dk), F32),
                   jnp.zeros((pairs, 4, SUBLANES, 2 * S5_STATE), F32))
    h_lat, h_ctx = x, ctx
    for i in range(depth):
        last = i == depth - 1
        h_ctx, ctx_states = _mixer(h_ctx, mods[i, 0:1], i, p, zero_states, 1, ctx_len, not last)
        h_lat, _ = _mixer(h_lat, mods[i, 1:1 + bsz], i, p, ctx_states, length // GRID_W, GRID_W, True)
    return _rmsnorm(h_lat.reshape(bsz * length, d), norm_f).reshape(bsz, length, d)
```

```python
import functools
import math

import numpy as np
import jax
import jax.numpy as jnp
from jax import lax
from jax.experimental import pallas as pl
from jax.experimental.pallas import tpu as pltpu

F32 = jnp.float32
BF16 = jnp.bfloat16
HIGHEST = lax.Precision.HIGHEST

EPS = 1e-6
N_MOD = 6
GRID_W = 64
LANES = 128
SUBLANES = 8
VMEM_LIMIT_MB = 56

HY_ORDER = 2
HY_BANDS = 16
HY_FAST_DECAY = 0.3
HY_SLOW_DECAY = 1.5
HY_TARGET = 1e-2
RET_HEADS = 8
RET_CHUNK = 256
S5_GROUP = 16
S5_STATE = 64
S5_CHUNK = 16


def _params(sem, vmem_mb=VMEM_LIMIT_MB):
    return pltpu.CompilerParams(dimension_semantics=sem, vmem_limit_bytes=vmem_mb << 20)


def _const_spec(shape):
    nd = len(shape)
    return pl.BlockSpec(shape, lambda *_: (0,) * nd)


def _mod_kernel(c_ref, w_ref, b_ref, o_ref):
    s = jax.nn.silu(c_ref[...])
    o_ref[0] = jnp.dot(s, w_ref[0], preferred_element_type=F32, precision=HIGHEST) + b_ref[0]


def _modulation(cc, ada_w, ada_b):
    depth, d, n = ada_w.shape
    bn = 1536
    return pl.pallas_call(
        _mod_kernel,
        grid=(depth, n // bn),
        in_specs=[pl.BlockSpec((8, d), lambda i, j: (0, 0)),
                  pl.BlockSpec((1, d, bn), lambda i, j: (i, 0, j)),
                  pl.BlockSpec((1, 1, bn), lambda i, j: (i, 0, j))],
        out_specs=pl.BlockSpec((1, 8, bn), lambda i, j: (i, 0, j)),
        out_shape=jax.ShapeDtypeStruct((depth, 8, n), F32),
        compiler_params=_params(("parallel", "parallel")),
        name="adaln_mod",
    )(cc, ada_w, ada_b.reshape(depth, 1, n))


def _modmm_kernel(h_ref, g_ref, sh_ref, sc_ref, w_ref, o_ref, *rest, chunk_cols):
    xm_ref = rest[-1]

    @pl.when(pl.program_id(1) == 0)
    def _():
        x = h_ref[...]
        y = x * lax.rsqrt(jnp.mean(x * x, axis=-1, keepdims=True) + EPS)
        y = y * g_ref[...]
        xm_ref[...] = (y * (1.0 + sc_ref[0]) + sh_ref[0]).astype(BF16)

    res = jnp.dot(xm_ref[...], w_ref[...], preferred_element_type=F32)
    o_ref[...] = res.astype(o_ref.dtype)
    if chunk_cols is not None:
        oc_ref, tok_ref = rest[0], rest[1]
        jblk, lo, width = chunk_cols

        @pl.when(pl.program_id(1) == jblk)
        def _():
            nchunk = oc_ref.shape[0]
            for q in range(width // LANES):
                tok_ref[q] = res[:, lo + q * LANES:lo + (q + 1) * LANES]
                for t in range(S5_CHUNK):
                    c0 = (q * S5_CHUNK + t) * LANES
                    oc_ref[:, c0:c0 + LANES] = tok_ref[q, pl.ds(t, nchunk, stride=S5_CHUNK), :].astype(oc_ref.dtype)


def _modmm(h, g, shift, scale, w, rows_per_mod, bm, bn, chunk_cols=None):
    r, d = h.shape
    n = w.shape[1]
    bm = min(bm, r)
    mod_idx = lambda i, j: ((i * bm) // rows_per_mod, 0, 0)
    out_specs = pl.BlockSpec((bm, bn), lambda i, j: (i, j))
    out_shape = jax.ShapeDtypeStruct((r, n), BF16)
    scratch = [pltpu.VMEM((bm, d), BF16)]
    cc = None
    if chunk_cols is not None:
        start, width = chunk_cols
        assert start // bn == (start + width - 1) // bn
        cc = (start // bn, start % bn, width)
        out_specs = [out_specs, pl.BlockSpec((bm // S5_CHUNK, S5_CHUNK * width), lambda i, j: (i, 0))]
        out_shape = [out_shape, jax.ShapeDtypeStruct((r // S5_CHUNK, S5_CHUNK * width), BF16)]
        scratch = [pltpu.VMEM((width // LANES, bm, LANES), F32)] + scratch
    return pl.pallas_call(
        functools.partial(_modmm_kernel, chunk_cols=cc),
        grid=(r // bm, n // bn),
        in_specs=[pl.BlockSpec((bm, d), lambda i, j: (i, 0)),
                  pl.BlockSpec((1, d), lambda i, j: (0, 0)),
                  pl.BlockSpec((1, 1, d), mod_idx),
                  pl.BlockSpec((1, 1, d), mod_idx),
                  pl.BlockSpec((d, bn), lambda i, j: (0, j))],
        out_specs=out_specs,
        out_shape=out_shape,
        scratch_shapes=scratch,
        compiler_params=_params(("parallel", "arbitrary")),
        name="modmm",
    )(h, g.reshape(1, d), shift, scale, w)


def _resmm_kernel(*refs, n_in):
    x_refs = refs[:n_in]
    w_ref, h_ref, gate_ref, o_ref = refs[n_in:]
    off = 0
    acc = None
    for x_ref in x_refs:
        k = x_ref.shape[1]
        part = jnp.dot(x_ref[...].astype(BF16), w_ref[off:off + k, :], preferred_element_type=F32)
        acc = part if acc is None else acc + part
        off += k
    o_ref[...] = h_ref[...] + gate_ref[0] * acc


def _resmm(xs, w, h, gate, rows_per_mod, bm, bn):
    r, n = h.shape
    bm = min(bm, r)
    k = w.shape[0]
    in_specs = [pl.BlockSpec((bm, x.shape[1]), lambda i, j: (i, 0)) for x in xs]
    in_specs += [pl.BlockSpec((k, bn), lambda i, j: (0, j)),
                 pl.BlockSpec((bm, bn), lambda i, j: (i, j)),
                 pl.BlockSpec((1, 1, bn), lambda i, j: ((i * bm) // rows_per_mod, 0, j))]
    return pl.pallas_call(
        functools.partial(_resmm_kernel, n_in=len(xs)),
        grid=(r // bm, n // bn),
        in_specs=in_specs,
        out_specs=pl.BlockSpec((bm, bn), lambda i, j: (i, j)),
        out_shape=jax.ShapeDtypeStruct((r, n), F32),
        compiler_params=_params(("parallel", "arbitrary")),
        name="resmm",
    )(*xs, w, h, gate)


def _shortconv_kernel(x_ref, w_ref, b_ref, o_ref, pad_ref, *, length, chunk):
    cb = x_ref.shape[-1]
    zeros = jnp.zeros((8, cb), F32)
    pad_ref[0:8, :] = zeros
    pad_ref[length + 8:length + 16, :] = zeros
    pad_ref[8:length + 8, :] = x_ref[0].astype(F32)
    w0, w1, w2 = w_ref[0:1, :], w_ref[1:2, :], w_ref[2:3, :]
    b = b_ref[...]
    for c in range(length // chunk):
        r = c * chunk
        o_ref[0, r:r + chunk, :] = (pad_ref[r + 7:r + 7 + chunk, :] * w0 + pad_ref[r + 8:r + 8 + chunk, :] * w1
                                    + pad_ref[r + 9:r + 9 + chunk, :] * w2 + b)


def _shortconv(a, w, b, width):
    bsz, length, _ = a.shape
    cb = 256
    chunk = min(512, length)
    return pl.pallas_call(
        functools.partial(_shortconv_kernel, length=length, chunk=chunk),
        grid=(bsz, width // cb),
        in_specs=[pl.BlockSpec((1, length, cb), lambda i, j: (i, 0, j)),
                  pl.BlockSpec((3, cb), lambda i, j: (0, j)),
                  pl.BlockSpec((1, cb), lambda i, j: (0, j))],
        out_specs=pl.BlockSpec((1, length, cb), lambda i, j: (i, 0, j)),
        out_shape=jax.ShapeDtypeStruct((bsz, length, width), F32),
        scratch_shapes=[pltpu.VMEM((length + 16, cb), F32)],
        compiler_params=_params(("parallel", "parallel")),
        name="hy_shortconv",
    )(a, w, b.reshape(1, width))


def _taps_kernel(z_ref, w1_ref, b1_ref, w2_ref, b2_ref, w3_ref, f_ref, dl_ref, o_ref, *, length):
    z = z_ref[...]
    h = jnp.sin(f_ref[0:1, :] * (jnp.dot(z, w1_ref[...], preferred_element_type=F32, precision=HIGHEST)
                                  + b1_ref[...]))
    h = jnp.sin(f_ref[1:2, :] * (jnp.dot(h, w2_ref[...], preferred_element_type=F32, precision=HIGHEST)
                                  + b2_ref[...]))
    h = jnp.dot(h, w3_ref[...], preferred_element_type=F32, precision=HIGHEST)
    t = z[:, 0:1]
    h = h * jnp.exp(-t * jnp.abs(dl_ref[...]))
    rb = z.shape[0]
    row = lax.broadcasted_iota(jnp.int32, h.shape, 0) + pl.program_id(0) * rb
    o_ref[...] = jnp.where(row == length, 0.0, h)


def _hyena_taps(length, w1, b1, w2, b2, w3, freq, width):
    n = 2 * length
    hid = w1.shape[1]
    pos = jnp.arange(n)
    idx = jnp.where(pos < length, pos, n - pos).astype(F32)
    t = (idx / max(length - 1, 1))[:, None]
    bands = jnp.linspace(1e-4, HY_BANDS - 1, HY_BANDS, dtype=F32)
    ang = (2.0 * math.pi * idx / length)[:, None] * bands[None]
    emb = 1 + 2 * HY_BANDS
    z = jnp.concatenate([t, jnp.cos(ang), -jnp.sin(ang), jnp.zeros((n, 64 - emb), F32)], axis=-1)
    w1p = jnp.concatenate([w1.astype(F32), jnp.zeros((64 - emb, hid), F32)], axis=0)
    max_decay = math.log(HY_TARGET) / HY_FAST_DECAY
    min_decay = math.log(HY_TARGET) / HY_SLOW_DECAY
    deltas = jnp.tile(jnp.linspace(min_decay, max_decay, width, dtype=F32), HY_ORDER)[None]
    oc = HY_ORDER * width
    rb = min(1024, length)
    half = length // rb
    return pl.pallas_call(
        functools.partial(_taps_kernel, length=length),
        grid=(n // rb,),
        in_specs=[pl.BlockSpec((rb, 64), lambda i: (i, 0)),
                  _const_spec((64, hid)), _const_spec((1, hid)),
                  _const_spec((hid, hid)), _const_spec((1, hid)),
                  pl.BlockSpec((hid, oc), lambda i: (0, i // half)),
                  _const_spec((2, hid)), _const_spec((1, oc))],
        out_specs=pl.BlockSpec((rb, oc), lambda i: (i, 0)),
        out_shape=jax.ShapeDtypeStruct((n, oc), F32),
        compiler_params=_params(("parallel",)),
        name="hy_taps",
    )(z, w1p, b1.reshape(1, hid).astype(F32), w2.astype(F32), b2.reshape(1, hid).astype(F32),
      w3.astype(F32), freq.astype(F32), deltas)


def _bf16(x):
    return jnp.asarray(x).astype(BF16)


def _split_bf16(x):
    x = jnp.asarray(x)
    hi = x.astype(BF16)
    return hi, (x - hi.astype(F32)).astype(BF16)


@functools.lru_cache(maxsize=None)
def _dft2_consts(length):
    n = 2 * length
    n2 = LANES
    n1 = n // n2
    hf = n1 // 2
    j = np.arange(n2)[:, None, None]
    k1 = np.arange(n1)[None, :, None]
    m1 = np.arange(n1)[None, None, :]
    ph = -2.0 * np.pi * (j * k1 / n + (m1 * k1 % n1) / n1)
    mr, mi = np.cos(ph), np.sin(ph)
    g1 = np.concatenate([np.concatenate([mr[:, :, :hf], -mi[:, :, :hf]], 2),
                         np.concatenate([mi[:, :, :hf], mr[:, :, :hf]], 2)], 1)
    g1f = np.concatenate([mr, mi], 1)
    mrt = np.transpose(mr, (0, 2, 1))[:, :hf] / n
    mit = -np.transpose(mi, (0, 2, 1))[:, :hf] / n
    g1i = np.concatenate([np.concatenate([mrt, -mit], 2),
                          np.concatenate([mit, mrt], 2)], 1)
    a = np.arange(n2)
    ph2 = -2.0 * np.pi * ((a[:, None] * a[None, :]) % n2) / n2
    fr, fi = np.cos(ph2), np.sin(ph2)
    g2 = np.block([[fr, -fi], [fi, fr]])
    g2i = np.block([[fr, fi], [-fi, fr]])
    f32 = lambda m: np.asarray(m, np.float32)
    return dict(g1=f32(g1), g1i=f32(g1i), g2=f32(g2), g2i=f32(g2i), g1f=f32(g1f))


@functools.lru_cache(maxsize=None)
def _dft1_consts(length):
    n = 2 * length
    a = np.arange(n)
    ph = -2.0 * np.pi * ((a[:, None] * a[None, :]) % n) / n
    fr, fi = np.cos(ph), np.sin(ph)
    gf = np.block([[fr[:, :length], -fi[:, :length]], [fi[:, :length], fr[:, :length]]])
    gi = np.block([[fr[:length], fi[:length]], [-fi[:length], fr[:length]]]) / n
    gff = np.concatenate([fr, fi], 0)
    f32 = lambda m: np.asarray(m, np.float32)
    return dict(gf=f32(gf), gi=f32(gi), gff=f32(gff))


def _dot3(g_hi, g_lo, x):
    x_hi = x.astype(BF16)
    x_lo = (x - x_hi.astype(F32)).astype(BF16)
    return (jnp.dot(g_hi, x_hi, preferred_element_type=F32) + jnp.dot(g_hi, x_lo, preferred_element_type=F32)
            + jnp.dot(g_lo, x_hi, preferred_element_type=F32))


def _fspec2_kernel(k_ref, g1h_ref, g1l_ref, g2h_ref, g2l_ref, kr_ref, ki_ref, *, n1):
    inv = 1.0 / jnp.sum(jnp.abs(k_ref[...]), axis=0, keepdims=True)

    def s1(j, c):
        x = k_ref[pl.ds(j, n1, stride=LANES), :]
        a = _dot3(g1h_ref[j], g1l_ref[j], x)
        kr_ref[pl.ds(j, n1, stride=LANES), :] = a[:n1]
        ki_ref[pl.ds(j, n1, stride=LANES), :] = a[n1:]
        return c

    lax.fori_loop(0, LANES, s1, 0, unroll=8)

    def s2(k1, c):
        r0 = pl.multiple_of(k1 * LANES, LANES)
        a = jnp.concatenate([kr_ref[pl.ds(r0, LANES), :], ki_ref[pl.ds(r0, LANES), :]], axis=0)
        x = _dot3(g2h_ref[...], g2l_ref[...], a) * inv
        kr_ref[pl.ds(r0, LANES), :] = x[:LANES]
        ki_ref[pl.ds(r0, LANES), :] = x[LANES:]
        return c

    lax.fori_loop(0, n1, s2, 0, unroll=4)


def _filter_spectrum2(taps):
    n, oc = taps.shape
    cst = _dft2_consts(n // 2)
    n1 = n // LANES
    cb = LANES
    g1h, g1l = _split_bf16(cst["g1f"])
    g2h, g2l = _split_bf16(cst["g2"])
    out = jax.ShapeDtypeStruct((n, oc), F32)
    return pl.pallas_call(
        functools.partial(_fspec2_kernel, n1=n1),
        grid=(oc // cb,),
        in_specs=[pl.BlockSpec((n, cb), lambda i: (0, i)),
                  _const_spec(g1h.shape), _const_spec(g1l.shape),
                  _const_spec(g2h.shape), _const_spec(g2l.shape)],
        out_specs=[pl.BlockSpec((n, cb), lambda i: (0, i))] * 2,
        out_shape=[out, out],
        compiler_params=_params(("parallel",)),
        name="hy_fspec2",
    )(taps, g1h, g1l, g2h, g2l)


def _fspec1_kernel(k_ref, gh_ref, gl_ref, kr_ref, ki_ref):
    k = k_ref[...]
    n = k.shape[0]
    inv = 1.0 / jnp.sum(jnp.abs(k), axis=0, keepdims=True)
    x = _dot3(gh_ref[...], gl_ref[...], k) * inv
    kr_ref[...] = x[:n]
    ki_ref[...] = x[n:]


def _filter_spectrum1(taps):
    n, oc = taps.shape
    gh, gl = _split_bf16(_dft1_consts(n // 2)["gff"])
    cb = 256
    out = jax.ShapeDtypeStruct((n, oc), F32)
    return pl.pallas_call(
        _fspec1_kernel,
        grid=(oc // cb,),
        in_specs=[pl.BlockSpec((n, cb), lambda i: (0, i)), _const_spec(gh.shape), _const_spec(gl.shape)],
        out_specs=[pl.BlockSpec((n, cb), lambda i: (0, i))] * 2,
        out_shape=[out, out],
        compiler_params=_params(("parallel",)),
        name="hy_fspec1",
    )(taps, gh, gl)


def _hyconv2_kernel(v_ref, x_ref, kr_ref, ki_ref, bias_ref, g1_ref, g1i_ref, g2_ref, g2i_ref, o_ref,
                    ar_ref, ai_ref, *, n1):
    hf = n1 // 2

    def s1(j, c):
        xa = v_ref[0, pl.ds(j, hf, stride=LANES), :]
        xb = v_ref[1, pl.ds(j, hf, stride=LANES), :]
        x = jnp.concatenate([xa, xb], axis=0).astype(BF16)
        a = jnp.dot(g1_ref[j], x, preferred_element_type=F32)
        ar_ref[pl.ds(j, n1, stride=LANES), :] = a[:n1]
        ai_ref[pl.ds(j, n1, stride=LANES), :] = a[n1:]
        return c

    lax.fori_loop(0, LANES, s1, 0, unroll=8)

    def s2(k1, c):
        r0 = pl.multiple_of(k1 * LANES, LANES)
        a = jnp.concatenate([ar_ref[pl.ds(r0, LANES), :], ai_ref[pl.ds(r0, LANES), :]], axis=0).astype(BF16)
        x = jnp.dot(g2_ref[...], a, preferred_element_type=F32)
        xr, xi = x[:LANES], x[LANES:]
        kr = kr_ref[pl.ds(r0, LANES), :]
        ki = ki_ref[pl.ds(r0, LANES), :]
        y = jnp.concatenate([xr * kr - xi * ki, xr * ki + xi * kr], axis=0).astype(BF16)
        b = jnp.dot(g2i_ref[...], y, preferred_element_type=F32)
        ar_ref[pl.ds(r0, LANES), :] = b[:LANES]
        ai_ref[pl.ds(r0, LANES), :] = b[LANES:]
        return c

    lax.fori_loop(0, n1, s2, 0, unroll=4)

    def s3(j, c):
        b = jnp.concatenate([ar_ref[pl.ds(j, n1, stride=LANES), :], ai_ref[pl.ds(j, n1, stride=LANES), :]],
                            axis=0).astype(BF16)
        y = jnp.dot(g1i_ref[j], b, preferred_element_type=F32)
        o_ref[0, pl.ds(j, hf, stride=LANES), :] = y[:hf]
        o_ref[1, pl.ds(j, hf, stride=LANES), :] = y[hf:]
        return c

    lax.fori_loop(0, LANES, s3, 0, unroll=8)
    bias = bias_ref[...]
    for b in range(2):
        o_ref[b] = x_ref[b] * (o_ref[b] + bias * v_ref[b])


def _hyconv2(va, v_col, xa, x_col, kr, ki, k_col, bias, length):
    bsz = va.shape[0]
    width = bias.shape[-1]
    cst = _dft2_consts(length)
    n = 2 * length
    n1 = n // LANES
    cb = LANES
    ncb = width // cb
    return pl.pallas_call(
        functools.partial(_hyconv2_kernel, n1=n1),
        grid=(ncb, bsz // 2),
        in_specs=[pl.BlockSpec((2, length, cb), lambda c, q: (q, 0, v_col + c)),
                  pl.BlockSpec((2, length, cb), lambda c, q: (q, 0, x_col + c)),
                  pl.BlockSpec((n, cb), lambda c, q: (0, k_col + c), pipeline_mode=pl.Buffered(1)),
                  pl.BlockSpec((n, cb), lambda c, q: (0, k_col + c), pipeline_mode=pl.Buffered(1)),
                  pl.BlockSpec((1, cb), lambda c, q: (0, c)),
                  _const_spec(cst["g1"].shape), _const_spec(cst["g1i"].shape),
                  _const_spec(cst["g2"].shape), _const_spec(cst["g2i"].shape)],
        out_specs=pl.BlockSpec((2, length, cb), lambda c, q: (q, 0, c)),
        out_shape=jax.ShapeDtypeStruct((bsz, length, width), F32),
        scratch_shapes=[pltpu.VMEM((n, cb), F32), pltpu.VMEM((n, cb), F32)],
        compiler_params=_params(("parallel", "arbitrary")),
        name="hy_conv2",
    )(va, xa, kr, ki, bias.reshape(1, width), _bf16(cst["g1"]), _bf16(cst["g1i"]), _bf16(cst["g2"]),
      _bf16(cst["g2i"]))


def _hyconv1_kernel(v_ref, x_ref, kr_ref, ki_ref, bias_ref, gf_ref, gi_ref, o_ref):
    length = v_ref.shape[1]
    n = 2 * length
    x = jnp.concatenate([v_ref[0], v_ref[1]], axis=0).astype(BF16)
    s = jnp.dot(gf_ref[...], x, preferred_element_type=F32)
    sr, si = s[:n], s[n:]
    kr, ki = kr_ref[...], ki_ref[...]
    y = jnp.concatenate([sr * kr - si * ki, sr * ki + si * kr], axis=0).astype(BF16)
    out = jnp.dot(gi_ref[...], y, preferred_element_type=F32)
    bias = bias_ref[...]
    for b in range(2):
        o_ref[b] = x_ref[b] * (out[b * length:(b + 1) * length] + bias * v_ref[b])


def _hyconv1(va, v_col, xa, x_col, kr, ki, k_col, bias, length):
    bsz = va.shape[0]
    width = bias.shape[-1]
    cst = _dft1_consts(length)
    n = 2 * length
    cb = LANES
    return pl.pallas_call(
        _hyconv1_kernel,
        grid=(width // cb, bsz // 2),
        in_specs=[pl.BlockSpec((2, length, cb), lambda c, q: (q, 0, v_col + c)),
                  pl.BlockSpec((2, length, cb), lambda c, q: (q, 0, x_col + c)),
                  pl.BlockSpec((n, cb), lambda c, q: (0, k_col + c)),
                  pl.BlockSpec((n, cb), lambda c, q: (0, k_col + c)),
                  pl.BlockSpec((1, cb), lambda c, q: (0, c)),
                  _const_spec(cst["gf"].shape), _const_spec(cst["gi"].shape)],
        out_specs=pl.BlockSpec((2, length, cb), lambda c, q: (q, 0, c)),
        out_shape=jax.ShapeDtypeStruct((bsz, length, width), F32),
        compiler_params=_params(("parallel", "arbitrary")),
        name="hy_conv1",
    )(va, xa, kr, ki, bias.reshape(1, width), _bf16(cst["gf"]), _bf16(cst["gi"]))


def _hyena(a, conv_w, conv_b, w1, b1, w2, b2, w3, freq, bias, width):
    bsz, length, _ = a.shape
    p = _shortconv(a, conv_w, conv_b, 3 * width)
    taps = _hyena_taps(length, w1, b1, w2, b2, w3, freq, width)
    two_stage = (2 * length) % (LANES * 16) == 0
    kr, ki = (_filter_spectrum2 if two_stage else _filter_spectrum1)(taps)
    conv = _hyconv2 if two_stage else _hyconv1
    ncb = width // LANES
    z = conv(p, 0, p, ncb, kr, ki, 0, bias[0], length)
    return conv(z, 0, p, 2 * ncb, kr, ki, ncb, bias[1], length)


def _ret_kernel(lg_ref, q_ref, k_ref, v_ref, g_ref, s0_ref, o_ref, sfin_ref, sb_ref, *, length, chunk, dk):
    hp = pl.program_id(1)
    nc = length // chunk
    dv = LANES
    row = lax.broadcasted_iota(jnp.int32, (chunk, chunk), 0)
    col = lax.broadcasted_iota(jnp.int32, (chunk, chunk), 1)
    diff = (row - col).astype(F32)
    lane = lax.broadcasted_iota(jnp.int32, (chunk, 2 * dk), 1)
    pos = lax.broadcasted_iota(jnp.int32, (chunk, 2 * dk), 0).astype(F32)
    ones_s = jnp.ones((2 * dk, dv), F32)
    kscale = dk ** -0.5
    for hh in range(2):
        h = hp * 2 + hh
        lgf = lg_ref[0, h]
        lgb = lg_ref[1, h]
        decay = jnp.where(diff >= 0.0, jnp.exp(jnp.maximum(diff, 0.0) * lgf),
                          jnp.exp(jnp.maximum(-diff, 0.0) * lgb))
        qmask = (lane >= dk * hh) & (lane < dk * (hh + 1))
        qf_dec = jnp.exp((pos + 1.0) * lgf)
        qb_dec = jnp.exp((chunk - pos) * lgb)
        kf_dec = jnp.exp((chunk - 1.0 - pos) * lgf) * kscale
        kb_dec = jnp.exp(pos * lgb) * kscale
        cdf = jnp.exp(ones_s * (chunk * lgf))
        cdb = jnp.exp(ones_s * (chunk * lgb))
        vs = slice(dv * hh, dv * (hh + 1))

        def bstep(i, state):
            n = nc - 1 - i
            r0 = pl.multiple_of(n * chunk, chunk)
            sb_ref[hh, n] = state
            k = k_ref[0, pl.ds(r0, chunk), :].astype(F32)
            v = v_ref[0, pl.ds(r0, chunk), vs]
            inc = lax.dot_general((k * kb_dec).astype(BF16), v, (((0,), (0,)), ((), ())),
                                  preferred_element_type=F32)
            return cdb * state + inc

        sfin_ref[0, 1, hh] = lax.fori_loop(0, nc, bstep, s0_ref[0, 1, hh])

        def fstep(n, state):
            r0 = pl.multiple_of(n * chunk, chunk)
            q = jnp.where(qmask, q_ref[0, pl.ds(r0, chunk), :].astype(F32), 0.0)
            k = k_ref[0, pl.ds(r0, chunk), :].astype(F32)
            v = v_ref[0, pl.ds(r0, chunk), vs]
            s = lax.dot_general(q.astype(BF16), (k * kscale).astype(BF16), (((1,), (1,)), ((), ())),
                                preferred_element_type=F32)
            y = jnp.dot((s * decay).astype(BF16), v, preferred_element_type=F32)
            y = y + jnp.dot((q * qf_dec).astype(BF16), state.astype(BF16), preferred_element_type=F32)
            y = y + jnp.dot((q * qb_dec).astype(BF16), sb_ref[hh, n].astype(BF16), preferred_element_type=F32)
            y = y * lax.rsqrt(jnp.mean(y * y, axis=-1, keepdims=True) + EPS)
            g = g_ref[0, pl.ds(r0, chunk), vs].astype(F32)
            o_ref[0, pl.ds(r0, chunk), vs] = (jax.nn.silu(g) * y).astype(o_ref.dtype)
            inc = lax.dot_general((k * kf_dec).astype(BF16), v, (((0,), (0,)), ((), ())),
                                  preferred_element_type=F32)
            return cdf * state + inc

        sfin_ref[0, 0, hh] = lax.fori_loop(0, nc, fstep, s0_ref[0, 0, hh])


def _retention(a, log_gamma, s0, q_off, dk):
    bsz, length, _ = a.shape
    heads = RET_HEADS
    dv = 2 * dk
    assert dv == LANES
    chunk = min(RET_CHUNK, length)
    qb = q_off // (2 * dk)
    kb = qb + heads // 2
    vb = (q_off + 2 * heads * dk) // (2 * dv)
    gb = vb + heads // 2
    seq = lambda blk, off: pl.BlockSpec((1, length, blk), lambda b, h, lg: (b, 0, off + h))
    st = pl.BlockSpec((1, 2, 2, 2 * dk, dv), lambda b, h, lg: (b, 0, h, 0, 0))
    grid_spec = pltpu.PrefetchScalarGridSpec(
        num_scalar_prefetch=1,
        grid=(bsz, heads // 2),
        in_specs=[seq(2 * dk, qb), seq(2 * dk, kb), seq(2 * dv, vb), seq(2 * dv, gb), st],
        out_specs=[pl.BlockSpec((1, length, 2 * dv), lambda b, h, lg: (b, 0, h)), st],
        scratch_shapes=[pltpu.VMEM((2, length // chunk, 2 * dk, dv), F32)],
    )
    return pl.pallas_call(
        functools.partial(_ret_kernel, length=length, chunk=chunk, dk=dk),
        grid_spec=grid_spec,
        out_shape=[jax.ShapeDtypeStruct((bsz, length, heads * dv), BF16),
                   jax.ShapeDtypeStruct(s0.shape, F32)],
        compiler_params=_params(("parallel", "parallel")),
        name="retention",
    )(log_gamma, a, a, a, a, s0)


def _s5_mats(lam_re, lam_im, log_step, b_re, b_im, c_re, c_im):
    t_len = S5_CHUNK
    lr = jnp.minimum(lam_re.astype(F32), -1e-4)
    li = lam_im.astype(F32)
    step = jnp.exp(log_step.astype(F32))[..., None]
    dr, di = lr * step, li * step
    d = jnp.arange(t_len + 1, dtype=F32)[:, None, None, None]
    mag = jnp.exp(d * dr)
    pr, pi = mag * jnp.cos(d * di), mag * jnp.sin(d * di)
    nr, ni = pr[1] - 1.0, pi[1]
    den = lr * lr + li * li
    cr, ci = (nr * lr + ni * li) / den, (ni * lr - nr * li) / den
    bbr = cr[..., None] * b_re - ci[..., None] * b_im
    bbi = cr[..., None] * b_im + ci[..., None] * b_re
    clr = c_re[None] * pr[:, :, :, None, :] - c_im[None] * pi[:, :, :, None, :]
    cli = c_re[None] * pi[:, :, :, None, :] + c_im[None] * pr[:, :, :, None, :]
    kern = (jnp.einsum('dxgop,xgpi->dxgoi', clr, bbr, precision=HIGHEST)
            - jnp.einsum('dxgop,xgpi->dxgoi', cli, bbi, precision=HIGHEST))
    s_i = jnp.arange(t_len)[:, None]
    t_i = jnp.arange(t_len)[None, :]
    lag = t_i - s_i
    tf = jnp.where((lag >= 0)[:, :, None, None, None], kern[jnp.clip(lag, 0, t_len), 0], 0.0)
    tb = jnp.where((lag <= 0)[:, :, None, None, None], kern[jnp.clip(-lag, 0, t_len), 1], 0.0)
    groups = lr.shape[1]
    gl = LANES // S5_GROUP
    nq = groups // gl
    eye = jnp.eye(gl, dtype=F32)
    tq = lambda e: e.reshape((nq, gl) + e.shape[1:])
    toe = tq(jnp.transpose(tf + tb, (2, 0, 4, 1, 3)))
    w1t = jnp.einsum('qgsito,gh->qsgitho', toe, eye).reshape(nq, t_len * LANES, t_len * LANES)
    sf = jnp.arange(t_len - 1, -1, -1)
    sb = jnp.arange(t_len)
    ef_r = pr[sf, 0][..., None] * bbr[0][None] - pi[sf, 0][..., None] * bbi[0][None]
    ef_i = pr[sf, 0][..., None] * bbi[0][None] + pi[sf, 0][..., None] * bbr[0][None]
    eb_r = pr[sb, 1][..., None] * bbr[1][None] - pi[sb, 1][..., None] * bbi[1][None]
    eb_i = pr[sb, 1][..., None] * bbi[1][None] + pi[sb, 1][..., None] * bbr[1][None]
    pk = lambda e: jnp.einsum('qgsip,gh->qsgihp', tq(jnp.transpose(e, (1, 0, 3, 2))),
                              eye).reshape(nq, t_len * LANES, gl * S5_STATE)
    tfw = jnp.arange(1, t_len + 1)
    tbw = jnp.arange(t_len, 0, -1)
    qk = lambda e: jnp.einsum('qgpto,gh->qgptho', tq(jnp.transpose(e, (1, 3, 0, 2))),
                              eye).reshape(nq, gl * S5_STATE, t_len * LANES)
    w1 = jnp.concatenate([w1t, pk(ef_r), pk(ef_i), pk(eb_r), pk(eb_i)], axis=2)
    w2 = jnp.concatenate([qk(clr[tfw, 0]), qk(-cli[tfw, 0]), qk(clr[tbw, 1]), qk(-cli[tbw, 1])], axis=1)
    pl2 = lambda e: e.reshape(nq, gl * S5_STATE)
    lam_t = jnp.stack([pl2(pr[t_len, 0]), pl2(pi[t_len, 0]), pl2(pr[t_len, 1]), pl2(pi[t_len, 1])], axis=1)
    return w1.astype(BF16), w2.astype(BF16), lam_t


def _s5_kernel(u_ref, w1_ref, w2_ref, lam_ref, s0_ref, y_ref, sfin_ref, r_ref, *, nc):
    sw = (LANES // S5_GROUP) * S5_STATE
    yw = S5_CHUNK * LANES
    r_ref[...] = jnp.dot(u_ref[0], w1_ref[0], preferred_element_type=F32)
    lfr, lfi, lbr, lbi = (lam_ref[0, i:i + 1, :] for i in range(4))
    cols = [slice(yw + i * sw, yw + (i + 1) * sw) for i in range(4)]
    rid = lax.broadcasted_iota(jnp.int32, (SUBLANES, sw), 0)
    nblk = nc // SUBLANES

    def block(kb, carry):
        fr, fi, br, bi = carry
        rf = pl.ds(pl.multiple_of(kb * SUBLANES, SUBLANES), SUBLANES)
        rb = pl.ds(pl.multiple_of((nblk - 1 - kb) * SUBLANES, SUBLANES), SUBLANES)
        efr, efi = r_ref[rf, cols[0]], r_ref[rf, cols[1]]
        ebr, ebi = r_ref[rb, cols[2]], r_ref[rb, cols[3]]
        xfr, xfi, xbr, xbi = efr, efi, ebr, ebi
        for s in range(SUBLANES):
            xfr = jnp.where(rid == s, fr, xfr)
            xfi = jnp.where(rid == s, fi, xfi)
            fr, fi = (lfr * fr - lfi * fi + efr[s:s + 1], lfr * fi + lfi * fr + efi[s:s + 1])
            z = SUBLANES - 1 - s
            xbr = jnp.where(rid == z, br, xbr)
            xbi = jnp.where(rid == z, bi, xbi)
            br, bi = (lbr * br - lbi * bi + ebr[z:z + 1], lbr * bi + lbi * br + ebi[z:z + 1])
        r_ref[rf, cols[0]] = xfr
        r_ref[rf, cols[1]] = xfi
        r_ref[rb, cols[2]] = xbr
        r_ref[rb, cols[3]] = xbi
        return fr, fi, br, bi

    fin = lax.fori_loop(0, nblk, block, tuple(s0_ref[0, 0, i:i + 1, :] for i in range(4)))
    for i in range(4):
        sfin_ref[0, 0, i:i + 1, :] = fin[i]
    xin = r_ref[:, yw:yw + 4 * sw].astype(BF16)
    y_ref[0] = r_ref[:, 0:yw] + jnp.dot(xin, w2_ref[0], preferred_element_type=F32)


def _s5(uc, w1, w2, lam_t, s0):
    bsz, nc, cw = uc.shape
    nq = w1.shape[0]
    tw = cw // nq
    sw = lam_t.shape[-1]
    st = pl.BlockSpec((1, 1, 4, sw), lambda q, b: (b, q, 0, 0))
    single = pl.Buffered(1)
    return pl.pallas_call(
        functools.partial(_s5_kernel, nc=nc),
        grid=(nq, bsz),
        in_specs=[pl.BlockSpec((1, nc, tw), lambda q, b: (b, 0, q)),
                  pl.BlockSpec((1,) + w1.shape[1:], lambda q, b: (q, 0, 0), pipeline_mode=single),
                  pl.BlockSpec((1,) + w2.shape[1:], lambda q, b: (q, 0, 0), pipeline_mode=single),
                  pl.BlockSpec((1, 4, sw), lambda q, b: (q, 0, 0)),
                  st],
        out_specs=[pl.BlockSpec((1, nc, tw), lambda q, b: (b, 0, q)), st],
        out_shape=[jax.ShapeDtypeStruct((bsz, nc, cw), F32), jax.ShapeDtypeStruct(s0.shape, F32)],
        scratch_shapes=[pltpu.VMEM((nc, w1.shape[2]), F32)],
        compiler_params=_params(("parallel", "arbitrary")),
        name="s5_scan",
    )(uc, w1, w2, lam_t, s0)


def _s5out_kernel(y_ref, u_ref, d_ref, w_ref, b_ref, o_ref, tok_ref):
    nchunk = y_ref.shape[0]
    nq = tok_ref.shape[0]
    for q in range(nq):
        for t in range(S5_CHUNK):
            c0 = (q * S5_CHUNK + t) * LANES
            tok_ref[q, pl.ds(t, nchunk, stride=S5_CHUNK), :] = y_ref[:, c0:c0 + LANES]
    y = jnp.concatenate([tok_ref[q] for q in range(nq)], axis=-1)
    z = jax.nn.gelu(y + d_ref[...] * u_ref[...].astype(F32))
    gate = jnp.dot(z.astype(BF16), w_ref[...], preferred_element_type=F32) + b_ref[...]
    o_ref[...] = (z * jax.nn.sigmoid(gate)).astype(o_ref.dtype)


def _s5_out(yc, a, u_col, d, glu_w, glu_b):
    width = d.shape[-1]
    r = a.shape[0]
    bm = min(1024, r)
    return pl.pallas_call(
        _s5out_kernel,
        grid=(r // bm,),
        in_specs=[pl.BlockSpec((bm // S5_CHUNK, S5_CHUNK * width), lambda i: (i, 0)),
                  pl.BlockSpec((bm, width), lambda i: (i, u_col)),
                  _const_spec((1, width)), _const_spec((width, width)), _const_spec((1, width))],
        out_specs=pl.BlockSpec((bm, width), lambda i: (i, 0)),
        out_shape=jax.ShapeDtypeStruct((r, width), BF16),
        scratch_shapes=[pltpu.VMEM((width // LANES, bm, LANES), F32)],
        compiler_params=_params(("parallel",)),
        name="s5_out",
    )(yc, a, d.reshape(1, width).astype(F32), glu_w.astype(BF16), glu_b.reshape(1, width).astype(F32))


def _ffnact_kernel(g_ref, v_ref, w_ref, b_ref, o_ref, pad_ref, *, rows, width, chunk):
    length = rows * width
    cb = g_ref.shape[-1]
    off = width + 8
    zeros = jnp.zeros((off, cb), F32)
    pad_ref[0:off, :] = zeros
    pad_ref[off + length:2 * off + length, :] = zeros
    pad_ref[off:off + length, :] = g_ref[0].astype(F32)
    col = lax.broadcasted_iota(jnp.int32, (chunk, cb), 0) % width
    not_first = col != 0
    not_last = col != width - 1
    bias = b_ref[...]
    for c in range(length // chunk):
        r = c * chunk
        acc = None
        for dw in range(3):
            part = None
            for dr in range(3):
                if rows == 1 and dr != 1:
                    continue
                s = off + r + (dr - 1) * width + (dw - 1)
                term = pad_ref[s:s + chunk, :] * w_ref[3 * dr + dw:3 * dr + dw + 1, :]
                part = term if part is None else part + term
            if dw == 0:
                part = jnp.where(not_first, part, 0.0)
            elif dw == 2:
                part = jnp.where(not_last, part, 0.0)
            acc = part if acc is None else acc + part
        gate = jax.nn.gelu(acc + bias)
        o_ref[0, r:r + chunk, :] = (gate * v_ref[0, r:r + chunk, :].astype(F32)).astype(o_ref.dtype)


def _ffn_act(gv, conv_w, conv_b, rows, width):
    nimg, length, f2 = gv.shape
    dff = f2 // 2
    cb = 256
    chunk = min(512, length)
    return pl.pallas_call(
        functools.partial(_ffnact_kernel, rows=rows, width=width, chunk=chunk),
        grid=(nimg, dff // cb),
        in_specs=[pl.BlockSpec((1, length, cb), lambda i, j: (i, 0, j)),
                  pl.BlockSpec((1, length, cb), lambda i, j: (i, 0, dff // cb + j)),
                  pl.BlockSpec((9, cb), lambda i, j: (0, j)),
                  pl.BlockSpec((1, cb), lambda i, j: (0, j))],
        out_specs=pl.BlockSpec((1, length, cb), lambda i, j: (i, 0, j)),
        out_shape=jax.ShapeDtypeStruct((nimg, length, dff), BF16),
        scratch_shapes=[pltpu.VMEM((length + 2 * (width + 8), cb), F32)],
        compiler_params=_params(("parallel", "parallel")),
        name="ffn_act",
    )(gv, gv, conv_w.reshape(9, dff).astype(F32), conv_b.reshape(1, dff).astype(F32))


def _rms_kernel(x_ref, g_ref, o_ref):
    x = x_ref[...]
    o_ref[...] = x * lax.rsqrt(jnp.mean(x * x, axis=-1, keepdims=True) + EPS) * g_ref[...]


def _rmsnorm(x, g):
    r, d = x.shape
    bm = min(1024, r)
    return pl.pallas_call(
        _rms_kernel,
        grid=(r // bm,),
        in_specs=[pl.BlockSpec((bm, d), lambda i: (i, 0)), _const_spec((1, d))],
        out_specs=pl.BlockSpec((bm, d), lambda i: (i, 0)),
        out_shape=jax.ShapeDtypeStruct((r, d), F32),
        compiler_params=_params(("parallel",)),
        name="final_norm",
    )(x, g.reshape(1, d).astype(F32))


def _mixer(h, mods, i, p, states, img_rows, img_width, full):
    bsz, length, d = h.shape
    r = bsz * length
    rpm = r // mods.shape[0]
    width = d // 4
    dk = d // 2 // RET_HEADS // 2
    h2 = h.reshape(r, d)
    q_off = 3 * width
    u_off = q_off + 2 * RET_HEADS * dk + 2 * (d // 2)
    a, uc = _modmm(h2, p['norm1_g'][i], mods[:, 0], mods[:, 1], p['w_in'][i], rpm, 1024, 1024,
                   chunk_cols=(u_off, width))
    a3 = a.reshape(bsz, length, -1)
    ret, ret_fin = _retention(a3, p['log_gamma'][i], states[0], q_off, dk)
    y5, s5_fin = _s5(uc.reshape(bsz, length // S5_CHUNK, -1), *p['s5_mats'][i], states[1])
    if not full:
        return None, (ret_fin, s5_fin)
    hy = _hyena(a3, p['hy_conv_w'][i], p['hy_conv_b'][i], p['hy_w1'][i], p['hy_b1'][i], p['hy_w2'][i],
                p['hy_b2'][i], p['hy_w3'][i], p['hy_freq'][i], p['hy_bias'][i], width)
    s5o = _s5_out(y5.reshape(r // S5_CHUNK, -1), a, u_off // width, p['s5_d'][i], p['s5_glu_w'][i],
                  p['s5_glu_b'][i])
    h2 = _resmm([hy.reshape(r, width), ret.reshape(r, d // 2), s5o], p['w_out'][i], h2, mods[:, 2], rpm, 512, 2048)
    gv = _modmm(h2, p['norm2_g'][i], mods[:, 3], mods[:, 4], p['ffn_w_up'][i], rpm, 1024, 1024)
    act = _ffn_act(gv.reshape(bsz, length, -1), p['ffn_conv_w'][i], p['ffn_conv_b'][i], img_rows, img_width)
    h2 = _resmm([act.reshape(r, -1)], p['ffn_w_down'][i], h2, mods[:, 5], rpm, 1024, 512)
    return h2.reshape(bsz, length, d), (ret_fin, s5_fin)


def kernel(x, c, ctx, c_ctx, ada_w, ada_b, norm1_g, w_in, hy_conv_w, hy_conv_b, hy_w1, hy_b1, hy_w2, hy_b2,
           hy_w3, hy_freq, hy_bias, ret_decay, s5_lam_re, s5_lam_im, s5_log_step, s5_b_re, s5_b_im, s5_c_re,
           s5_c_im, s5_d, s5_glu_w, s5_glu_b, w_out, norm2_g, ffn_w_up, ffn_conv_w, ffn_conv_b, ffn_w_down,
           norm_f):
    bsz, length, d = x.shape
    depth = ada_w.shape[0]
    ctx_len = ctx.shape[1]
    dk = d // 2 // RET_HEADS // 2
    pairs = d // 4 // (2 * S5_GROUP)

    cc = jnp.concatenate([c_ctx[None], c, jnp.zeros((8 - 1 - bsz, d), F32)], axis=0)
    mods = _modulation(cc, ada_w, ada_b).reshape(depth, 8, N_MOD, 1, d)
    p = dict(norm1_g=norm1_g, norm2_g=norm2_g, hy_conv_w=hy_conv_w, hy_conv_b=hy_conv_b, hy_w1=hy_w1,
             hy_b1=hy_b1, hy_w2=hy_w2, hy_b2=hy_b2, hy_w3=hy_w3, hy_freq=hy_freq, hy_bias=hy_bias,
             s5_d=s5_d, s5_glu_w=s5_glu_w, s5_glu_b=s5_glu_b, ffn_conv_w=ffn_conv_w, ffn_conv_b=ffn_conv_b,
             w_in=w_in.astype(BF16), w_out=w_out.astype(BF16), ffn_w_up=ffn_w_up.astype(BF16),
             ffn_w_down=ffn_w_down.astype(BF16),
             log_gamma=-jnp.exp(ret_decay.astype(F32)),
             s5_mats=[_s5_mats(s5_lam_re[i], s5_lam_im[i], s5_log_step[i], s5_b_re[i], s5_b_im[i],
                               s5_c_re[i], s5_c_im[i]) for i in range(depth)])
    zero_states = (jnp.zeros((bsz, 2, RET_HEADS, 2 * dk, 4 * dk), F32),
                   jnp.zeros((bsz, d // 4 // LANES, 4, (LANES // S5_GROUP) * S5_STATE), F32))
    h_lat, h_ctx = x, ctx
    for i in range(depth):
        last = i == depth - 1
        h_ctx, ctx_states = _mixer(h_ctx, mods[i, 0:1], i, p, zero_states, 1, ctx_len, not last)
        h_lat, _ = _mixer(h_lat, mods[i, 1:1 + bsz], i, p, ctx_states, length // GRID_W, GRID_W, True)
    return _rmsnorm(h_lat.reshape(bsz * length, d), norm_f).reshape(bsz, length, d)
```

```python
import functools
import math

import numpy as np
import jax
import jax.numpy as jnp
from jax import lax
from jax.experimental import pallas as pl
from jax.experimental.pallas import tpu as pltpu

F32 = jnp.float32
BF16 = jnp.bfloat16
HIGHEST = lax.Precision.HIGHEST

EPS = 1e-6
N_MOD = 6
GRID_W = 64
LANES = 128
SUBLANES = 8
VMEM_LIMIT_MB = 56

HY_ORDER = 2
HY_BANDS = 16
HY_FAST_DECAY = 0.3
HY_SLOW_DECAY = 1.5
HY_TARGET = 1e-2
RET_HEADS = 8
RET_CHUNK = 256
S5_GROUP = 16
S5_STATE = 64
S5_CHUNK = 16


def _params(sem, vmem_mb=VMEM_LIMIT_MB):
    return pltpu.CompilerParams(dimension_semantics=sem, vmem_limit_bytes=vmem_mb << 20)


def _const_spec(shape):
    nd = len(shape)
    return pl.BlockSpec(shape, lambda *_: (0,) * nd)


def _mod_kernel(c_ref, w_ref, b_ref, o_ref):
    s = jax.nn.silu(c_ref[...])
    o_ref[0] = jnp.dot(s, w_ref[0], preferred_element_type=F32, precision=HIGHEST) + b_ref[0]


def _modulation(cc, ada_w, ada_b):
    depth, d, n = ada_w.shape
    bn = 1536
    return pl.pallas_call(
        _mod_kernel,
        grid=(depth, n // bn),
        in_specs=[pl.BlockSpec((8, d), lambda i, j: (0, 0)),
                  pl.BlockSpec((1, d, bn), lambda i, j: (i, 0, j)),
                  pl.BlockSpec((1, 1, bn), lambda i, j: (i, 0, j))],
        out_specs=pl.BlockSpec((1, 8, bn), lambda i, j: (i, 0, j)),
        out_shape=jax.ShapeDtypeStruct((depth, 8, n), F32),
        compiler_params=_params(("parallel", "parallel")),
        name="adaln_mod",
    )(cc, ada_w, ada_b.reshape(depth, 1, n))


def _modmm_kernel(h_ref, g_ref, sh_ref, sc_ref, w_ref, o_ref, *rest, chunk_cols):
    xm_ref = rest[-1]

    @pl.when(pl.program_id(1) == 0)
    def _():
        x = h_ref[...]
        y = x * lax.rsqrt(jnp.mean(x * x, axis=-1, keepdims=True) + EPS)
        y = y * g_ref[...]
        xm_ref[...] = (y * (1.0 + sc_ref[0]) + sh_ref[0]).astype(BF16)

    res = jnp.dot(xm_ref[...], w_ref[...].astype(BF16), preferred_element_type=F32)
    o_ref[...] = res.astype(o_ref.dtype)
    if chunk_cols is not None:
        oc_ref, tok_ref = rest[0], rest[1]
        jblk, lo, width = chunk_cols

        @pl.when(pl.program_id(1) == jblk)
        def _():
            nchunk = oc_ref.shape[0]
            for q in range(width // LANES):
                tok_ref[q] = res[:, lo + q * LANES:lo + (q + 1) * LANES]
                for t in range(S5_CHUNK):
                    c0 = (q * S5_CHUNK + t) * LANES
                    oc_ref[:, c0:c0 + LANES] = tok_ref[q, pl.ds(t, nchunk, stride=S5_CHUNK), :].astype(oc_ref.dtype)


def _modmm(h, g, shift, scale, w, layer, rows_per_mod, bm, bn, chunk_cols=None):
    r, d = h.shape
    n = w.shape[2]
    bm = min(bm, r)
    mod_idx = lambda i, j: ((i * bm) // rows_per_mod, 0, 0)
    out_specs = pl.BlockSpec((bm, bn), lambda i, j: (i, j))
    out_shape = jax.ShapeDtypeStruct((r, n), BF16)
    scratch = [pltpu.VMEM((bm, d), BF16)]
    cc = None
    if chunk_cols is not None:
        start, width = chunk_cols
        assert start // bn == (start + width - 1) // bn
        cc = (start // bn, start % bn, width)
        out_specs = [out_specs, pl.BlockSpec((bm // S5_CHUNK, S5_CHUNK * width), lambda i, j: (i, 0))]
        out_shape = [out_shape, jax.ShapeDtypeStruct((r // S5_CHUNK, S5_CHUNK * width), BF16)]
        scratch = [pltpu.VMEM((width // LANES, bm, LANES), F32)] + scratch
    return pl.pallas_call(
        functools.partial(_modmm_kernel, chunk_cols=cc),
        grid=(r // bm, n // bn),
        in_specs=[pl.BlockSpec((bm, d), lambda i, j: (i, 0)),
                  pl.BlockSpec((1, d), lambda i, j: (0, 0)),
                  pl.BlockSpec((1, 1, d), mod_idx),
                  pl.BlockSpec((1, 1, d), mod_idx),
                  pl.BlockSpec((None, d, bn), lambda i, j: (layer, 0, j))],
        out_specs=out_specs,
        out_shape=out_shape,
        scratch_shapes=scratch,
        compiler_params=_params(("parallel", "arbitrary")),
        name="modmm",
    )(h, g.reshape(1, d), shift, scale, w)


def _resmm_kernel(*refs, n_in):
    x_refs = refs[:n_in]
    w_ref, h_ref, gate_ref, o_ref = refs[n_in:]
    off = 0
    acc = None
    for x_ref in x_refs:
        k = x_ref.shape[1]
        part = jnp.dot(x_ref[...].astype(BF16), w_ref[off:off + k, :].astype(BF16), preferred_element_type=F32)
        acc = part if acc is None else acc + part
        off += k
    o_ref[...] = h_ref[...] + gate_ref[0] * acc


def _resmm(xs, w, layer, h, gate, rows_per_mod, bm, bn):
    r, n = h.shape
    bm = min(bm, r)
    k = w.shape[1]
    w_mode = pl.Buffered(1) if bn == n else None
    in_specs = [pl.BlockSpec((bm, x.shape[1]), lambda i, j: (i, 0)) for x in xs]
    in_specs += [pl.BlockSpec((None, k, bn), lambda i, j: (layer, 0, j), pipeline_mode=w_mode),
                 pl.BlockSpec((bm, bn), lambda i, j: (i, j)),
                 pl.BlockSpec((1, 1, bn), lambda i, j: ((i * bm) // rows_per_mod, 0, j))]
    return pl.pallas_call(
        functools.partial(_resmm_kernel, n_in=len(xs)),
        grid=(r // bm, n // bn),
        in_specs=in_specs,
        out_specs=pl.BlockSpec((bm, bn), lambda i, j: (i, j)),
        out_shape=jax.ShapeDtypeStruct((r, n), F32),
        compiler_params=_params(("parallel", "arbitrary")),
        name="resmm",
    )(*xs, w, h, gate)


def _shortconv_kernel(x_ref, w_ref, b_ref, o_ref, pad_ref, *, length, chunk):
    cb = x_ref.shape[-1]
    zeros = jnp.zeros((8, cb), F32)
    pad_ref[0:8, :] = zeros
    pad_ref[length + 8:length + 16, :] = zeros
    pad_ref[8:length + 8, :] = x_ref[0].astype(F32)
    w0, w1, w2 = w_ref[0:1, :], w_ref[1:2, :], w_ref[2:3, :]
    b = b_ref[...]
    for c in range(length // chunk):
        r = c * chunk
        o_ref[0, r:r + chunk, :] = (pad_ref[r + 7:r + 7 + chunk, :] * w0 + pad_ref[r + 8:r + 8 + chunk, :] * w1
                                    + pad_ref[r + 9:r + 9 + chunk, :] * w2 + b)


def _shortconv(a, w, b, width):
    bsz, length, _ = a.shape
    cb = 256
    chunk = min(512, length)
    return pl.pallas_call(
        functools.partial(_shortconv_kernel, length=length, chunk=chunk),
        grid=(bsz, width // cb),
        in_specs=[pl.BlockSpec((1, length, cb), lambda i, j: (i, 0, j)),
                  pl.BlockSpec((3, cb), lambda i, j: (0, j)),
                  pl.BlockSpec((1, cb), lambda i, j: (0, j))],
        out_specs=pl.BlockSpec((1, length, cb), lambda i, j: (i, 0, j)),
        out_shape=jax.ShapeDtypeStruct((bsz, length, width), F32),
        scratch_shapes=[pltpu.VMEM((length + 16, cb), F32)],
        compiler_params=_params(("parallel", "parallel")),
        name="hy_shortconv",
    )(a, w, b.reshape(1, width))


def _taps_kernel(z_ref, w1_ref, b1_ref, w2_ref, b2_ref, w3_ref, f_ref, dl_ref, o_ref, *, length):
    z = z_ref[...]
    h = jnp.sin(f_ref[0:1, :] * (jnp.dot(z, w1_ref[...], preferred_element_type=F32, precision=HIGHEST)
                                  + b1_ref[...]))
    h = jnp.sin(f_ref[1:2, :] * (jnp.dot(h, w2_ref[...], preferred_element_type=F32, precision=HIGHEST)
                                  + b2_ref[...]))
    h = jnp.dot(h, w3_ref[...], preferred_element_type=F32, precision=HIGHEST)
    t = z[:, 0:1]
    h = h * jnp.exp(-t * jnp.abs(dl_ref[...]))
    rb = z.shape[0]
    row = lax.broadcasted_iota(jnp.int32, h.shape, 0) + pl.program_id(0) * rb
    o_ref[...] = jnp.where(row == length, 0.0, h)


def _hyena_taps(length, w1, b1, w2, b2, w3, freq, width):
    n = 2 * length
    hid = w1.shape[1]
    pos = jnp.arange(n)
    idx = jnp.where(pos < length, pos, n - pos).astype(F32)
    t = (idx / max(length - 1, 1))[:, None]
    bands = jnp.linspace(1e-4, HY_BANDS - 1, HY_BANDS, dtype=F32)
    ang = (2.0 * math.pi * idx / length)[:, None] * bands[None]
    emb = 1 + 2 * HY_BANDS
    z = jnp.concatenate([t, jnp.cos(ang), -jnp.sin(ang), jnp.zeros((n, 64 - emb), F32)], axis=-1)
    w1p = jnp.concatenate([w1.astype(F32), jnp.zeros((64 - emb, hid), F32)], axis=0)
    max_decay = math.log(HY_TARGET) / HY_FAST_DECAY
    min_decay = math.log(HY_TARGET) / HY_SLOW_DECAY
    deltas = jnp.tile(jnp.linspace(min_decay, max_decay, width, dtype=F32), HY_ORDER)[None]
    oc = HY_ORDER * width
    rb = min(1024, length)
    half = length // rb
    return pl.pallas_call(
        functools.partial(_taps_kernel, length=length),
        grid=(n // rb,),
        in_specs=[pl.BlockSpec((rb, 64), lambda i: (i, 0)),
                  _const_spec((64, hid)), _const_spec((1, hid)),
                  _const_spec((hid, hid)), _const_spec((1, hid)),
                  pl.BlockSpec((hid, oc), lambda i: (0, i // half)),
                  _const_spec((2, hid)), _const_spec((1, oc))],
        out_specs=pl.BlockSpec((rb, oc), lambda i: (i, 0)),
        out_shape=jax.ShapeDtypeStruct((n, oc), F32),
        compiler_params=_params(("parallel",)),
        name="hy_taps",
    )(z, w1p, b1.reshape(1, hid).astype(F32), w2.astype(F32), b2.reshape(1, hid).astype(F32),
      w3.astype(F32), freq.astype(F32), deltas)


def _bf16(x):
    return jnp.asarray(x).astype(BF16)


@functools.lru_cache(maxsize=None)
def _dft2_consts(length):
    n = 2 * length
    n2 = LANES
    n1 = n // n2
    hf = n1 // 2
    j = np.arange(n2)[:, None, None]
    k1 = np.arange(n1)[None, :, None]
    m1 = np.arange(n1)[None, None, :]
    ph = -2.0 * np.pi * (j * k1 / n + (m1 * k1 % n1) / n1)
    mr, mi = np.cos(ph), np.sin(ph)
    g1 = np.concatenate([np.concatenate([mr[:, :, :hf], -mi[:, :, :hf]], 2),
                         np.concatenate([mi[:, :, :hf], mr[:, :, :hf]], 2)], 1)
    g1f = np.concatenate([mr, mi], 1)
    mrt = np.transpose(mr, (0, 2, 1))[:, :hf] / n
    mit = -np.transpose(mi, (0, 2, 1))[:, :hf] / n
    g1i = np.concatenate([np.concatenate([mrt, -mit], 2),
                          np.concatenate([mit, mrt], 2)], 1)
    a = np.arange(n2)
    ph2 = -2.0 * np.pi * ((a[:, None] * a[None, :]) % n2) / n2
    fr, fi = np.cos(ph2), np.sin(ph2)
    g2 = np.block([[fr, -fi], [fi, fr]])
    g2i = np.block([[fr, fi], [-fi, fr]])
    f32 = lambda m: np.asarray(m, np.float32)
    return dict(g1=f32(g1), g1i=f32(g1i), g2=f32(g2), g2i=f32(g2i), g1f=f32(g1f))


@functools.lru_cache(maxsize=None)
def _dft1_consts(length):
    n = 2 * length
    a = np.arange(n)
    ph = -2.0 * np.pi * ((a[:, None] * a[None, :]) % n) / n
    fr, fi = np.cos(ph), np.sin(ph)
    gf = np.block([[fr[:, :length], -fi[:, :length]], [fi[:, :length], fr[:, :length]]])
    gi = np.block([[fr[:length], fi[:length]], [-fi[:length], fr[:length]]]) / n
    gff = np.concatenate([fr, fi], 0)
    f32 = lambda m: np.asarray(m, np.float32)
    return dict(gf=f32(gf), gi=f32(gi), gff=f32(gff))


def _fspec2_kernel(k_ref, g1_ref, g2_ref, kr_ref, ki_ref, *, n1):
    inv = 1.0 / jnp.sum(jnp.abs(k_ref[...]), axis=0, keepdims=True)

    def s1(j, c):
        x = k_ref[pl.ds(j, n1, stride=LANES), :].astype(BF16)
        a = jnp.dot(g1_ref[j], x, preferred_element_type=F32)
        kr_ref[pl.ds(j, n1, stride=LANES), :] = a[:n1]
        ki_ref[pl.ds(j, n1, stride=LANES), :] = a[n1:]
        return c

    lax.fori_loop(0, LANES, s1, 0, unroll=8)

    def s2(k1, c):
        r0 = pl.multiple_of(k1 * LANES, LANES)
        a = jnp.concatenate([kr_ref[pl.ds(r0, LANES), :], ki_ref[pl.ds(r0, LANES), :]], axis=0).astype(BF16)
        x = jnp.dot(g2_ref[...], a, preferred_element_type=F32) * inv
        kr_ref[pl.ds(r0, LANES), :] = x[:LANES]
        ki_ref[pl.ds(r0, LANES), :] = x[LANES:]
        return c

    lax.fori_loop(0, n1, s2, 0, unroll=4)


def _filter_spectrum2(taps):
    n, oc = taps.shape
    cst = _dft2_consts(n // 2)
    n1 = n // LANES
    cb = LANES
    g1, g2 = _bf16(cst["g1f"]), _bf16(cst["g2"])
    out = jax.ShapeDtypeStruct((n, oc), F32)
    return pl.pallas_call(
        functools.partial(_fspec2_kernel, n1=n1),
        grid=(oc // cb,),
        in_specs=[pl.BlockSpec((n, cb), lambda i: (0, i)), _const_spec(g1.shape), _const_spec(g2.shape)],
        out_specs=[pl.BlockSpec((n, cb), lambda i: (0, i))] * 2,
        out_shape=[out, out],
        compiler_params=_params(("parallel",)),
        name="hy_fspec2",
    )(taps, g1, g2)


def _fspec1_kernel(k_ref, g_ref, kr_ref, ki_ref):
    k = k_ref[...]
    n = k.shape[0]
    inv = 1.0 / jnp.sum(jnp.abs(k), axis=0, keepdims=True)
    x = jnp.dot(g_ref[...], k.astype(BF16), preferred_element_type=F32) * inv
    kr_ref[...] = x[:n]
    ki_ref[...] = x[n:]


def _filter_spectrum1(taps):
    n, oc = taps.shape
    g = _bf16(_dft1_consts(n // 2)["gff"])
    cb = 256
    out = jax.ShapeDtypeStruct((n, oc), F32)
    return pl.pallas_call(
        _fspec1_kernel,
        grid=(oc // cb,),
        in_specs=[pl.BlockSpec((n, cb), lambda i: (0, i)), _const_spec(g.shape)],
        out_specs=[pl.BlockSpec((n, cb), lambda i: (0, i))] * 2,
        out_shape=[out, out],
        compiler_params=_params(("parallel",)),
        name="hy_fspec1",
    )(taps, g)


def _hyconv2_kernel(v_ref, x_ref, kr_ref, ki_ref, bias_ref, g1_ref, g1i_ref, g2_ref, g2i_ref, o_ref,
                    ar_ref, ai_ref, *, n1):
    hf = n1 // 2

    def s1(j, c):
        xa = v_ref[0, pl.ds(j, hf, stride=LANES), :]
        xb = v_ref[1, pl.ds(j, hf, stride=LANES), :]
        x = jnp.concatenate([xa, xb], axis=0).astype(BF16)
        a = jnp.dot(g1_ref[j], x, preferred_element_type=F32)
        ar_ref[pl.ds(j, n1, stride=LANES), :] = a[:n1]
        ai_ref[pl.ds(j, n1, stride=LANES), :] = a[n1:]
        return c

    lax.fori_loop(0, LANES, s1, 0, unroll=8)

    def s2(k1, c):
        r0 = pl.multiple_of(k1 * LANES, LANES)
        a = jnp.concatenate([ar_ref[pl.ds(r0, LANES), :], ai_ref[pl.ds(r0, LANES), :]], axis=0).astype(BF16)
        x = jnp.dot(g2_ref[...], a, preferred_element_type=F32)
        xr, xi = x[:LANES], x[LANES:]
        kr = kr_ref[pl.ds(r0, LANES), :]
        ki = ki_ref[pl.ds(r0, LANES), :]
        y = jnp.concatenate([xr * kr - xi * ki, xr * ki + xi * kr], axis=0).astype(BF16)
        b = jnp.dot(g2i_ref[...], y, preferred_element_type=F32)
        ar_ref[pl.ds(r0, LANES), :] = b[:LANES]
        ai_ref[pl.ds(r0, LANES), :] = b[LANES:]
        return c

    lax.fori_loop(0, n1, s2, 0, unroll=4)

    def s3(j, c):
        b = jnp.concatenate([ar_ref[pl.ds(j, n1, stride=LANES), :], ai_ref[pl.ds(j, n1, stride=LANES), :]],
                            axis=0).astype(BF16)
        y = jnp.dot(g1i_ref[j], b, preferred_element_type=F32)
        o_ref[0, pl.ds(j, hf, stride=LANES), :] = y[:hf]
        o_ref[1, pl.ds(j, hf, stride=LANES), :] = y[hf:]
        return c

    lax.fori_loop(0, LANES, s3, 0, unroll=8)
    bias = bias_ref[...]
    for b in range(2):
        o_ref[b] = x_ref[b] * (o_ref[b] + bias * v_ref[b])


def _hyconv2(va, v_col, xa, x_col, kr, ki, k_col, bias, length):
    bsz = va.shape[0]
    width = bias.shape[-1]
    cst = _dft2_consts(length)
    n = 2 * length
    n1 = n // LANES
    cb = LANES
    ncb = width // cb
    return pl.pallas_call(
        functools.partial(_hyconv2_kernel, n1=n1),
        grid=(ncb, bsz // 2),
        in_specs=[pl.BlockSpec((2, length, cb), lambda c, q: (q, 0, v_col + c)),
                  pl.BlockSpec((2, length, cb), lambda c, q: (q, 0, x_col + c)),
                  pl.BlockSpec((n, cb), lambda c, q: (0, k_col + c), pipeline_mode=pl.Buffered(1)),
                  pl.BlockSpec((n, cb), lambda c, q: (0, k_col + c), pipeline_mode=pl.Buffered(1)),
                  pl.BlockSpec((1, cb), lambda c, q: (0, c)),
                  _const_spec(cst["g1"].shape), _const_spec(cst["g1i"].shape),
                  _const_spec(cst["g2"].shape), _const_spec(cst["g2i"].shape)],
        out_specs=pl.BlockSpec((2, length, cb), lambda c, q: (q, 0, c)),
        out_shape=jax.ShapeDtypeStruct((bsz, length, width), F32),
        scratch_shapes=[pltpu.VMEM((n, cb), F32), pltpu.VMEM((n, cb), F32)],
        compiler_params=_params(("parallel", "arbitrary")),
        name="hy_conv2",
    )(va, xa, kr, ki, bias.reshape(1, width), _bf16(cst["g1"]), _bf16(cst["g1i"]), _bf16(cst["g2"]),
      _bf16(cst["g2i"]))


def _hyconv1_kernel(v_ref, x_ref, kr_ref, ki_ref, bias_ref, gf_ref, gi_ref, o_ref):
    length = v_ref.shape[1]
    n = 2 * length
    x = jnp.concatenate([v_ref[0], v_ref[1]], axis=0).astype(BF16)
    s = jnp.dot(gf_ref[...], x, preferred_element_type=F32)
    sr, si = s[:n], s[n:]
    kr, ki = kr_ref[...], ki_ref[...]
    y = jnp.concatenate([sr * kr - si * ki, sr * ki + si * kr], axis=0).astype(BF16)
    out = jnp.dot(gi_ref[...], y, preferred_element_type=F32)
    bias = bias_ref[...]
    for b in range(2):
        o_ref[b] = x_ref[b] * (out[b * length:(b + 1) * length] + bias * v_ref[b])


def _hyconv1(va, v_col, xa, x_col, kr, ki, k_col, bias, length):
    bsz = va.shape[0]
    width = bias.shape[-1]
    cst = _dft1_consts(length)
    n = 2 * length
    cb = LANES
    return pl.pallas_call(
        _hyconv1_kernel,
        grid=(width // cb, bsz // 2),
        in_specs=[pl.BlockSpec((2, length, cb), lambda c, q: (q, 0, v_col + c)),
                  pl.BlockSpec((2, length, cb), lambda c, q: (q, 0, x_col + c)),
                  pl.BlockSpec((n, cb), lambda c, q: (0, k_col + c)),
                  pl.BlockSpec((n, cb), lambda c, q: (0, k_col + c)),
                  pl.BlockSpec((1, cb), lambda c, q: (0, c)),
                  _const_spec(cst["gf"].shape), _const_spec(cst["gi"].shape)],
        out_specs=pl.BlockSpec((2, length, cb), lambda c, q: (q, 0, c)),
        out_shape=jax.ShapeDtypeStruct((bsz, length, width), F32),
        compiler_params=_params(("parallel", "arbitrary")),
        name="hy_conv1",
    )(va, xa, kr, ki, bias.reshape(1, width), _bf16(cst["gf"]), _bf16(cst["gi"]))


def _hyena(a, conv_w, conv_b, w1, b1, w2, b2, w3, freq, bias, width):
    bsz, length, _ = a.shape
    p = _shortconv(a, conv_w, conv_b, 3 * width)
    taps = _hyena_taps(length, w1, b1, w2, b2, w3, freq, width)
    two_stage = (2 * length) % (LANES * 16) == 0
    kr, ki = (_filter_spectrum2 if two_stage else _filter_spectrum1)(taps)
    conv = _hyconv2 if two_stage else _hyconv1
    ncb = width // LANES
    z = conv(p, 0, p, ncb, kr, ki, 0, bias[0], length)
    return conv(z, 0, p, 2 * ncb, kr, ki, ncb, bias[1], length)


def _ret_kernel(lg_ref, q_ref, k_ref, v_ref, g_ref, s0_ref, o_ref, sfin_ref, sb_ref, *, length, chunk, dk):
    hp = pl.program_id(1)
    nc = length // chunk
    dv = LANES
    row = lax.broadcasted_iota(jnp.int32, (chunk, chunk), 0)
    col = lax.broadcasted_iota(jnp.int32, (chunk, chunk), 1)
    diff = (row - col).astype(F32)
    lane = lax.broadcasted_iota(jnp.int32, (chunk, 2 * dk), 1)
    pos = lax.broadcasted_iota(jnp.int32, (chunk, 2 * dk), 0).astype(F32)
    ones_s = jnp.ones((2 * dk, dv), F32)
    kscale = dk ** -0.5
    tn = (((0,), (0,)), ((), ()))
    nt = (((1,), (1,)), ((), ()))
    hd = []
    for hh in range(2):
        lgf = lg_ref[0, hp * 2 + hh]
        lgb = lg_ref[1, hp * 2 + hh]
        hd.append(dict(
            decay=jnp.where(diff >= 0.0, jnp.exp(jnp.maximum(diff, 0.0) * lgf),
                            jnp.exp(jnp.maximum(-diff, 0.0) * lgb)),
            qmask=(lane >= dk * hh) & (lane < dk * (hh + 1)),
            qf=jnp.exp((pos + 1.0) * lgf), qb=jnp.exp((chunk - pos) * lgb),
            kf=jnp.exp((chunk - 1.0 - pos) * lgf) * kscale, kb=jnp.exp(pos * lgb) * kscale,
            cdf=jnp.exp(ones_s * (chunk * lgf)), cdb=jnp.exp(ones_s * (chunk * lgb)),
            vs=slice(dv * hh, dv * (hh + 1))))

    def bstep(i, states):
        n = nc - 1 - i
        r0 = pl.multiple_of(n * chunk, chunk)
        k = k_ref[0, pl.ds(r0, chunk), :].astype(F32)
        out = []
        for hh, c in enumerate(hd):
            sb_ref[hh, n] = states[hh]
            v = v_ref[0, pl.ds(r0, chunk), c['vs']]
            inc = lax.dot_general((k * c['kb']).astype(BF16), v, tn, preferred_element_type=F32)
            out.append(c['cdb'] * states[hh] + inc)
        return tuple(out)

    fin = lax.fori_loop(0, nc, bstep, (s0_ref[0, 1, 0], s0_ref[0, 1, 1]))
    sfin_ref[0, 1, 0] = fin[0]
    sfin_ref[0, 1, 1] = fin[1]

    def fstep(n, states):
        r0 = pl.multiple_of(n * chunk, chunk)
        qa = q_ref[0, pl.ds(r0, chunk), :].astype(F32)
        k = k_ref[0, pl.ds(r0, chunk), :].astype(F32)
        ks = (k * kscale).astype(BF16)
        out = []
        for hh, c in enumerate(hd):
            q = jnp.where(c['qmask'], qa, 0.0)
            v = v_ref[0, pl.ds(r0, chunk), c['vs']]
            s = lax.dot_general(q.astype(BF16), ks, nt, preferred_element_type=F32)
            y = jnp.dot((s * c['decay']).astype(BF16), v, preferred_element_type=F32)
            y = y + jnp.dot((q * c['qf']).astype(BF16), states[hh].astype(BF16), preferred_element_type=F32)
            y = y + jnp.dot((q * c['qb']).astype(BF16), sb_ref[hh, n].astype(BF16), preferred_element_type=F32)
            y = y * lax.rsqrt(jnp.mean(y * y, axis=-1, keepdims=True) + EPS)
            g = g_ref[0, pl.ds(r0, chunk), c['vs']].astype(F32)
            o_ref[0, pl.ds(r0, chunk), c['vs']] = (jax.nn.silu(g) * y).astype(o_ref.dtype)
            inc = lax.dot_general((k * c['kf']).astype(BF16), v, tn, preferred_element_type=F32)
            out.append(c['cdf'] * states[hh] + inc)
        return tuple(out)

    fin = lax.fori_loop(0, nc, fstep, (s0_ref[0, 0, 0], s0_ref[0, 0, 1]), unroll=2 if nc % 2 == 0 else 1)
    sfin_ref[0, 0, 0] = fin[0]
    sfin_ref[0, 0, 1] = fin[1]


def _retention(a, log_gamma, s0, q_off, dk):
    bsz, length, _ = a.shape
    heads = RET_HEADS
    dv = 2 * dk
    assert dv == LANES
    chunk = min(RET_CHUNK, length)
    qb = q_off // (2 * dk)
    kb = qb + heads // 2
    vb = (q_off + 2 * heads * dk) // (2 * dv)
    gb = vb + heads // 2
    seq = lambda blk, off: pl.BlockSpec((1, length, blk), lambda b, h, lg: (b, 0, off + h))
    st = pl.BlockSpec((1, 2, 2, 2 * dk, dv), lambda b, h, lg: (b, 0, h, 0, 0))
    grid_spec = pltpu.PrefetchScalarGridSpec(
        num_scalar_prefetch=1,
        grid=(bsz, heads // 2),
        in_specs=[seq(2 * dk, qb), seq(2 * dk, kb), seq(2 * dv, vb), seq(2 * dv, gb), st],
        out_specs=[pl.BlockSpec((1, length, 2 * dv), lambda b, h, lg: (b, 0, h)), st],
        scratch_shapes=[pltpu.VMEM((2, length // chunk, 2 * dk, dv), F32)],
    )
    return pl.pallas_call(
        functools.partial(_ret_kernel, length=length, chunk=chunk, dk=dk),
        grid_spec=grid_spec,
        out_shape=[jax.ShapeDtypeStruct((bsz, length, heads * dv), BF16),
                   jax.ShapeDtypeStruct(s0.shape, F32)],
        compiler_params=_params(("parallel", "parallel")),
        name="retention",
    )(log_gamma, a, a, a, a, s0)


@functools.lru_cache(maxsize=None)
def _s5_expanders():
    gl = LANES // S5_GROUP
    t_len = S5_CHUNK
    ex_to = np.zeros((t_len * S5_GROUP, t_len * LANES), np.float32)
    for t in range(t_len):
        for h in range(gl):
            for o in range(S5_GROUP):
                ex_to[t * S5_GROUP + o, t * LANES + h * S5_GROUP + o] = 1.0
    ex_p = np.tile(np.eye(S5_STATE, dtype=np.float32), (1, gl))
    g_sgi = (np.arange(t_len * LANES) // S5_GROUP) % gl
    g_tho = (np.arange(t_len * LANES) // S5_GROUP) % gl
    g_hp = np.arange(gl * S5_STATE) // S5_STATE
    eq = lambda a, b: (a[:, None] == b[None, :]).astype(np.float32)
    return ex_to, ex_p, eq(g_sgi, g_tho), eq(g_sgi, g_hp), eq(g_hp, g_tho)


def _s5_mats(lam_re, lam_im, log_step, b_re, b_im, c_re, c_im):
    t_len = S5_CHUNK
    lr = jnp.minimum(lam_re.astype(F32), -1e-4)
    li = lam_im.astype(F32)
    step = jnp.exp(log_step.astype(F32))[..., None]
    dr, di = lr * step, li * step
    d = jnp.arange(t_len + 1, dtype=F32)[:, None, None, None]
    mag = jnp.exp(d * dr)
    pr, pi = mag * jnp.cos(d * di), mag * jnp.sin(d * di)
    nr, ni = pr[1] - 1.0, pi[1]
    den = lr * lr + li * li
    cr, ci = (nr * lr + ni * li) / den, (ni * lr - nr * li) / den
    bbr = cr[..., None] * b_re - ci[..., None] * b_im
    bbi = cr[..., None] * b_im + ci[..., None] * b_re
    clr = c_re[None] * pr[:, :, :, None, :] - c_im[None] * pi[:, :, :, None, :]
    cli = c_re[None] * pi[:, :, :, None, :] + c_im[None] * pr[:, :, :, None, :]
    kern = (jnp.einsum('dxgop,xgpi->dxgoi', clr, bbr, precision=HIGHEST)
            - jnp.einsum('dxgop,xgpi->dxgoi', cli, bbi, precision=HIGHEST))
    s_i = jnp.arange(t_len)[:, None]
    t_i = jnp.arange(t_len)[None, :]
    lag = t_i - s_i
    tf = jnp.where((lag >= 0)[:, :, None, None, None], kern[jnp.clip(lag, 0, t_len), 0], 0.0)
    tb = jnp.where((lag <= 0)[:, :, None, None, None], kern[jnp.clip(-lag, 0, t_len), 1], 0.0)
    groups = lr.shape[1]
    gl = LANES // S5_GROUP
    nq = groups // gl
    tq = lambda e: e.reshape((nq, gl) + e.shape[1:])
    ex_to, ex_p, m_rc, m_rp, m_pc = _s5_expanders()

    def spread(compact, expand, mask):
        return jnp.matmul(compact, expand) * mask

    toe = tq(jnp.transpose(tf + tb, (2, 0, 4, 1, 3)))
    toe = jnp.transpose(toe, (0, 2, 1, 3, 4, 5)).reshape(nq, t_len * LANES, t_len * S5_GROUP)
    w1t = spread(toe, ex_to, m_rc)
    sf = jnp.arange(t_len - 1, -1, -1)
    sb = jnp.arange(t_len)
    ef_r = pr[sf, 0][..., None] * bbr[0][None] - pi[sf, 0][..., None] * bbi[0][None]
    ef_i = pr[sf, 0][..., None] * bbi[0][None] + pi[sf, 0][..., None] * bbr[0][None]
    eb_r = pr[sb, 1][..., None] * bbr[1][None] - pi[sb, 1][..., None] * bbi[1][None]
    eb_i = pr[sb, 1][..., None] * bbi[1][None] + pi[sb, 1][..., None] * bbr[1][None]
    pk = lambda e: spread(jnp.transpose(tq(jnp.transpose(e, (1, 0, 3, 2))), (0, 2, 1, 3, 4)).reshape(
        nq, t_len * LANES, S5_STATE), ex_p, m_rp)
    tfw = jnp.arange(1, t_len + 1)
    tbw = jnp.arange(t_len, 0, -1)
    qk = lambda e: spread(tq(jnp.transpose(e, (1, 3, 0, 2))).reshape(nq, gl * S5_STATE, t_len * S5_GROUP),
                          ex_to, m_pc)
    w1 = jnp.concatenate([w1t, pk(ef_r), pk(ef_i), pk(eb_r), pk(eb_i)], axis=2)
    w2 = jnp.concatenate([qk(clr[tfw, 0]), qk(-cli[tfw, 0]), qk(clr[tbw, 1]), qk(-cli[tbw, 1])], axis=1)
    pl2 = lambda e: e.reshape(nq, gl * S5_STATE)
    lam_t = jnp.stack([pl2(pr[t_len, 0]), pl2(pi[t_len, 0]), pl2(pr[t_len, 1]), pl2(pi[t_len, 1])], axis=1)
    return w1.astype(BF16), w2.astype(BF16), lam_t


def _s5_kernel(u_ref, w1_ref, w2_ref, lam_ref, s0_ref, y_ref, sfin_ref, r_ref, *, nc):
    sw = (LANES // S5_GROUP) * S5_STATE
    yw = S5_CHUNK * LANES
    nb = u_ref.shape[0]
    r_ref[...] = jnp.dot(u_ref[...].reshape(nb * nc, yw), w1_ref[0], preferred_element_type=F32)
    lfr, lfi, lbr, lbi = (lam_ref[0, i:i + 1, :] for i in range(4))
    cols = [slice(yw + i * sw, yw + (i + 1) * sw) for i in range(4)]
    rid = lax.broadcasted_iota(jnp.int32, (SUBLANES, sw), 0)
    nblk = nc // SUBLANES

    def block(kb, carry):
        out = []
        for b in range(nb):
            fr, fi, br, bi = carry[4 * b:4 * b + 4]
            rf = pl.ds(pl.multiple_of(b * nc + kb * SUBLANES, SUBLANES), SUBLANES)
            rb = pl.ds(pl.multiple_of(b * nc + (nblk - 1 - kb) * SUBLANES, SUBLANES), SUBLANES)
            efr, efi = r_ref[rf, cols[0]], r_ref[rf, cols[1]]
            ebr, ebi = r_ref[rb, cols[2]], r_ref[rb, cols[3]]
            xfr, xfi, xbr, xbi = efr, efi, ebr, ebi
            for s in range(SUBLANES):
                xfr = jnp.where(rid == s, fr, xfr)
                xfi = jnp.where(rid == s, fi, xfi)
                fr, fi = (lfr * fr - lfi * fi + efr[s:s + 1], lfr * fi + lfi * fr + efi[s:s + 1])
                z = SUBLANES - 1 - s
                xbr = jnp.where(rid == z, br, xbr)
                xbi = jnp.where(rid == z, bi, xbi)
                br, bi = (lbr * br - lbi * bi + ebr[z:z + 1], lbr * bi + lbi * br + ebi[z:z + 1])
            r_ref[rf, cols[0]] = xfr
            r_ref[rf, cols[1]] = xfi
            r_ref[rb, cols[2]] = xbr
            r_ref[rb, cols[3]] = xbi
            out += [fr, fi, br, bi]
        return tuple(out)

    fin = lax.fori_loop(0, nblk, block, tuple(s0_ref[b, 0, i:i + 1, :] for b in range(nb) for i in range(4)))
    for b in range(nb):
        for i in range(4):
            sfin_ref[b, 0, i:i + 1, :] = fin[4 * b + i]
    xin = r_ref[:, yw:yw + 4 * sw].astype(BF16)
    y = r_ref[:, 0:yw] + jnp.dot(xin, w2_ref[0], preferred_element_type=F32)
    y_ref[...] = y.reshape(nb, nc, yw)


def _s5(uc, w1, w2, lam_t, s0):
    bsz, nc, cw = uc.shape
    nq = w1.shape[0]
    tw = cw // nq
    sw = lam_t.shape[-1]
    nb = max(1, min(bsz, 512 // nc))
    assert bsz % nb == 0
    st = pl.BlockSpec((nb, 1, 4, sw), lambda q, b: (b, q, 0, 0))
    single = pl.Buffered(1)
    return pl.pallas_call(
        functools.partial(_s5_kernel, nc=nc),
        grid=(nq, bsz // nb),
        in_specs=[pl.BlockSpec((nb, nc, tw), lambda q, b: (b, 0, q)),
                  pl.BlockSpec((1,) + w1.shape[1:], lambda q, b: (q, 0, 0), pipeline_mode=single),
                  pl.BlockSpec((1,) + w2.shape[1:], lambda q, b: (q, 0, 0), pipeline_mode=single),
                  pl.BlockSpec((1, 4, sw), lambda q, b: (q, 0, 0)),
                  st],
        out_specs=[pl.BlockSpec((nb, nc, tw), lambda q, b: (b, 0, q)), st],
        out_shape=[jax.ShapeDtypeStruct((bsz, nc, cw), F32), jax.ShapeDtypeStruct(s0.shape, F32)],
        scratch_shapes=[pltpu.VMEM((nb * nc, w1.shape[2]), F32)],
        compiler_params=_params(("parallel", "arbitrary")),
        name="s5_scan",
    )(uc, w1, w2, lam_t, s0)


def _s5out_kernel(y_ref, u_ref, d_ref, w_ref, b_ref, o_ref, tok_ref):
    nchunk = y_ref.shape[0]
    nq = tok_ref.shape[0]
    for q in range(nq):
        for t in range(S5_CHUNK):
            c0 = (q * S5_CHUNK + t) * LANES
            tok_ref[q, pl.ds(t, nchunk, stride=S5_CHUNK), :] = y_ref[:, c0:c0 + LANES]
    y = jnp.concatenate([tok_ref[q] for q in range(nq)], axis=-1)
    z = jax.nn.gelu(y + d_ref[...] * u_ref[...].astype(F32))
    gate = jnp.dot(z.astype(BF16), w_ref[...], preferred_element_type=F32) + b_ref[...]
    o_ref[...] = (z * jax.nn.sigmoid(gate)).astype(o_ref.dtype)


def _s5_out(yc, a, u_col, d, glu_w, glu_b):
    width = d.shape[-1]
    r = a.shape[0]
    bm = min(1024, r)
    return pl.pallas_call(
        _s5out_kernel,
        grid=(r // bm,),
        in_specs=[pl.BlockSpec((bm // S5_CHUNK, S5_CHUNK * width), lambda i: (i, 0)),
                  pl.BlockSpec((bm, width), lambda i: (i, u_col)),
                  _const_spec((1, width)), _const_spec((width, width)), _const_spec((1, width))],
        out_specs=pl.BlockSpec((bm, width), lambda i: (i, 0)),
        out_shape=jax.ShapeDtypeStruct((r, width), BF16),
        scratch_shapes=[pltpu.VMEM((width // LANES, bm, LANES), F32)],
        compiler_params=_params(("parallel",)),
        name="s5_out",
    )(yc, a, d.reshape(1, width).astype(F32), glu_w.astype(BF16), glu_b.reshape(1, width).astype(F32))


def _ffndown_kernel(*refs, rows, width, chunk, nk):
    halo = rows > 1
    if halo:
        g_ref, ga_ref, gb_ref, v_ref, cw_ref, cb_ref, w_ref, h_ref, gate_ref, o_ref, act_ref, pad_ref = refs
    else:
        g_ref, v_ref, cw_ref, cb_ref, w_ref, h_ref, gate_ref, o_ref, act_ref = refs
    i = pl.program_id(0)
    s = pl.program_id(1)
    bm, fk = g_ref.shape

    @pl.when(s == 0)
    def _():
        act_ref[1] = jnp.zeros((bm, fk), BF16)
        o_ref[...] = jnp.zeros(o_ref.shape, F32)

    if halo:
        per_img = (rows * width) // bm
        top = (i % per_img) == 0
        bottom = (i % per_img) == per_img - 1
        pad_ref[0:width, :] = jnp.where(top, 0.0, ga_ref[...].astype(F32))
        pad_ref[width:width + bm, :] = g_ref[...].astype(F32)
        pad_ref[width + bm:2 * width + bm, :] = jnp.where(bottom, 0.0, gb_ref[...].astype(F32))
    col = lax.broadcasted_iota(jnp.int32, (chunk, fk), 0) % width
    not_first = col != 0
    not_last = col != width - 1
    bias = cb_ref[...]
    wt = [cw_ref[t:t + 1, :] for t in range(9)]
    slot = s % 2
    for c in range(bm // chunk):
        r = c * chunk
        if halo:
            up = pad_ref[r:r + chunk, :]
            mid = pad_ref[width + r:width + r + chunk, :]
            dn = pad_ref[2 * width + r:2 * width + r + chunk, :]
            v0, v1, v2 = (up * wt[dw] + mid * wt[3 + dw] + dn * wt[6 + dw] for dw in range(3))
        else:
            mid = g_ref[r:r + chunk, :].astype(F32)
            v0, v1, v2 = (mid * wt[3 + dw] for dw in range(3))
        left = jnp.where(not_first, pltpu.roll(v0, 1, 0), 0.0)
        right = jnp.where(not_last, pltpu.roll(v2, chunk - 1, 0), 0.0)
        gate = jax.nn.gelu(v1 + left + right + bias)
        act_ref[slot, r:r + chunk, :] = (gate * v_ref[r:r + chunk, :].astype(F32)).astype(BF16)

    o_ref[...] += jnp.dot(act_ref[1 - slot], w_ref[...].astype(BF16), preferred_element_type=F32)

    @pl.when(s == nk)
    def _():
        o_ref[...] = h_ref[...] + gate_ref[0] * o_ref[...]


def _ffn_down(gv, conv_w, conv_b, w_down, layer, h, gate, rows_per_mod, rows, width):
    r, f2 = gv.shape
    dff = f2 // 2
    d = h.shape[1]
    bm = min(1024, r)
    fk = 512
    nk = dff // fk
    chunk = width
    assert bm % chunk == 0 and ((rows * width) % bm == 0 if rows > 1 else bm % width == 0)
    kc = lambda s: jnp.minimum(s, nk - 1)
    in_specs = [pl.BlockSpec((bm, fk), lambda i, s: (i, kc(s)))]
    args = [gv]
    scratch = [pltpu.VMEM((2, bm, fk), BF16)]
    if rows > 1:
        per = bm // width
        last = r // width - 1
        in_specs += [pl.BlockSpec((width, fk), lambda i, s: (jnp.maximum(i * per - 1, 0), kc(s))),
                     pl.BlockSpec((width, fk), lambda i, s: (jnp.minimum((i + 1) * per, last), kc(s)))]
        args += [gv, gv]
        scratch.append(pltpu.VMEM((bm + 2 * width, fk), F32))
    in_specs += [pl.BlockSpec((bm, fk), lambda i, s: (i, nk + kc(s))),
                 pl.BlockSpec((9, fk), lambda i, s: (0, kc(s))),
                 pl.BlockSpec((1, fk), lambda i, s: (0, kc(s))),
                 pl.BlockSpec((None, fk, d), lambda i, s: (layer, jnp.maximum(s - 1, 0), 0)),
                 pl.BlockSpec((bm, d), lambda i, s: (i, 0), pipeline_mode=pl.Buffered(1)),
                 pl.BlockSpec((1, 1, d), lambda i, s: ((i * bm) // rows_per_mod, 0, 0))]
    args += [gv, conv_w.reshape(9, dff).astype(F32), conv_b.reshape(1, dff).astype(F32), w_down, h, gate]
    return pl.pallas_call(
        functools.partial(_ffndown_kernel, rows=rows, width=width, chunk=chunk, nk=nk),
        grid=(r // bm, nk + 1),
        in_specs=in_specs,
        out_specs=pl.BlockSpec((bm, d), lambda i, s: (i, 0)),
        out_shape=jax.ShapeDtypeStruct((r, d), F32),
        scratch_shapes=scratch,
        compiler_params=_params(("parallel", "arbitrary")),
        name="ffn_down",
    )(*args)


def _rms_kernel(x_ref, g_ref, o_ref):
    x = x_ref[...]
    o_ref[...] = x * lax.rsqrt(jnp.mean(x * x, axis=-1, keepdims=True) + EPS) * g_ref[...]


def _rmsnorm(x, g):
    r, d = x.shape
    bm = min(1024, r)
    return pl.pallas_call(
        _rms_kernel,
        grid=(r // bm,),
        in_specs=[pl.BlockSpec((bm, d), lambda i: (i, 0)), _const_spec((1, d))],
        out_specs=pl.BlockSpec((bm, d), lambda i: (i, 0)),
        out_shape=jax.ShapeDtypeStruct((r, d), F32),
        compiler_params=_params(("parallel",)),
        name="final_norm",
    )(x, g.reshape(1, d).astype(F32))


def _mixer(h, mods, i, p, states, img_rows, img_width, full):
    bsz, length, d = h.shape
    r = bsz * length
    rpm = r // mods.shape[0]
    width = d // 4
    dk = d // 2 // RET_HEADS // 2
    h2 = h.reshape(r, d)
    q_off = 3 * width
    u_off = q_off + 2 * RET_HEADS * dk + 2 * (d // 2)
    a, uc = _modmm(h2, p['norm1_g'][i], mods[:, 0], mods[:, 1], p['w_in'], i, rpm, 1024, 1024,
                   chunk_cols=(u_off, width))
    a3 = a.reshape(bsz, length, -1)
    ret, ret_fin = _retention(a3, p['log_gamma'][i], states[0], q_off, dk)
    y5, s5_fin = _s5(uc.reshape(bsz, length // S5_CHUNK, -1), *p['s5_mats'][i], states[1])
    if not full:
        return None, (ret_fin, s5_fin)
    hy = _hyena(a3, p['hy_conv_w'][i], p['hy_conv_b'][i], p['hy_w1'][i], p['hy_b1'][i], p['hy_w2'][i],
                p['hy_b2'][i], p['hy_w3'][i], p['hy_freq'][i], p['hy_bias'][i], width)
    s5o = _s5_out(y5.reshape(r // S5_CHUNK, -1), a, u_off // width, p['s5_d'][i], p['s5_glu_w'][i],
                  p['s5_glu_b'][i])
    h2 = _resmm([hy.reshape(r, width), ret.reshape(r, d // 2), s5o], p['w_out'], i, h2, mods[:, 2], rpm, 512, d)
    gv = _modmm(h2, p['norm2_g'][i], mods[:, 3], mods[:, 4], p['ffn_w_up'], i, rpm, 1024, 1024)
    h2 = _ffn_down(gv, p['ffn_conv_w'][i], p['ffn_conv_b'][i], p['ffn_w_down'], i, h2, mods[:, 5], rpm,
                   img_rows, img_width)
    return h2.reshape(bsz, length, d), (ret_fin, s5_fin)


def kernel(x, c, ctx, c_ctx, ada_w, ada_b, norm1_g, w_in, hy_conv_w, hy_conv_b, hy_w1, hy_b1, hy_w2, hy_b2,
           hy_w3, hy_freq, hy_bias, ret_decay, s5_lam_re, s5_lam_im, s5_log_step, s5_b_re, s5_b_im, s5_c_re,
           s5_c_im, s5_d, s5_glu_w, s5_glu_b, w_out, norm2_g, ffn_w_up, ffn_conv_w, ffn_conv_b, ffn_w_down,
           norm_f):
    bsz, length, d = x.shape
    depth = ada_w.shape[0]
    ctx_len = ctx.shape[1]
    dk = d // 2 // RET_HEADS // 2
    pairs = d // 4 // (2 * S5_GROUP)

    cc = jnp.concatenate([c_ctx[None], c, jnp.zeros((8 - 1 - bsz, d), F32)], axis=0)
    mods = _modulation(cc, ada_w, ada_b).reshape(depth, 8, N_MOD, 1, d)
    p = dict(norm1_g=norm1_g, norm2_g=norm2_g, hy_conv_w=hy_conv_w, hy_conv_b=hy_conv_b, hy_w1=hy_w1,
             hy_b1=hy_b1, hy_w2=hy_w2, hy_b2=hy_b2, hy_w3=hy_w3, hy_freq=hy_freq, hy_bias=hy_bias,
             s5_d=s5_d, s5_glu_w=s5_glu_w, s5_glu_b=s5_glu_b, ffn_conv_w=ffn_conv_w, ffn_conv_b=ffn_conv_b,
             w_in=w_in, w_out=w_out, ffn_w_up=ffn_w_up, ffn_w_down=ffn_w_down,
             log_gamma=-jnp.exp(ret_decay.astype(F32)),
             s5_mats=[_s5_mats(s5_lam_re[i], s5_lam_im[i], s5_log_step[i], s5_b_re[i], s5_b_im[i],
                               s5_c_re[i], s5_c_im[i]) for i in range(depth)])
    zero_states = (jnp.zeros((bsz, 2, RET_HEADS, 2 * dk, 2 * dk), F32),
                   jnp.zeros((bsz, d // 4 // LANES, 4, (LANES // S5_GROUP) * S5_STATE), F32))
    h_lat, h_ctx = x, ctx
    for i in range(depth):
        last = i == depth - 1
        h_ctx, ctx_states = _mixer(h_ctx, mods[i, 0:1], i, p, zero_states, 1, ctx_len, not last)
        h_lat, _ = _mixer(h_lat, mods[i, 1:1 + bsz], i, p, ctx_states, length // GRID_W, GRID_W, True)
    return _rmsnorm(h_lat.reshape(bsz * length, d), norm_f).reshape(bsz, length, d)
```

```python
import functools
import math

import numpy as np
import jax
import jax.numpy as jnp
from jax import lax
from jax.experimental import pallas as pl
from jax.experimental.pallas import tpu as pltpu

F32 = jnp.float32
BF16 = jnp.bfloat16
HIGHEST = lax.Precision.HIGHEST

EPS = 1e-6
N_MOD = 6
GRID_W = 64
LANES = 128
SUBLANES = 8
MXU_COLS = 256
VMEM_LIMIT_MB = 56

HY_ORDER = 2
HY_BANDS = 16
HY_FAST_DECAY = 0.3
HY_SLOW_DECAY = 1.5
HY_TARGET = 1e-2
RET_HEADS = 8
RET_CHUNK = 256
S5_GROUP = 16
S5_STATE = 64
S5_CHUNK = 16


def _params(sem, vmem_mb=VMEM_LIMIT_MB):
    return pltpu.CompilerParams(dimension_semantics=sem, vmem_limit_bytes=vmem_mb << 20)


def _const_spec(shape):
    nd = len(shape)
    return pl.BlockSpec(shape, lambda *_: (0,) * nd)


def _mod_kernel(c_ref, w_ref, b_ref, o_ref):
    s = jax.nn.silu(c_ref[...])
    o_ref[0] = jnp.dot(s, w_ref[0], preferred_element_type=F32, precision=HIGHEST) + b_ref[0]


def _modulation(cc, ada_w, ada_b):
    depth, d, n = ada_w.shape
    bn = 1536
    return pl.pallas_call(
        _mod_kernel,
        grid=(depth, n // bn),
        in_specs=[pl.BlockSpec((8, d), lambda i, j: (0, 0)),
                  pl.BlockSpec((1, d, bn), lambda i, j: (i, 0, j)),
                  pl.BlockSpec((1, 1, bn), lambda i, j: (i, 0, j))],
        out_specs=pl.BlockSpec((1, 8, bn), lambda i, j: (i, 0, j)),
        out_shape=jax.ShapeDtypeStruct((depth, 8, n), F32),
        compiler_params=_params(("parallel", "parallel")),
        name="adaln_mod",
    )(cc, ada_w, ada_b.reshape(depth, 1, n))


def _modmm_kernel(h_ref, g_ref, sh_ref, sc_ref, w_ref, o_ref, *rest, chunk_cols):
    xm_ref = rest[-1]

    @pl.when(pl.program_id(1) == 0)
    def _():
        x = h_ref[...]
        y = x * lax.rsqrt(jnp.mean(x * x, axis=-1, keepdims=True) + EPS)
        y = y * g_ref[...]
        xm_ref[...] = (y * (1.0 + sc_ref[0]) + sh_ref[0]).astype(BF16)

    res = jnp.dot(xm_ref[...], w_ref[...].astype(BF16), preferred_element_type=F32)
    o_ref[...] = res.astype(o_ref.dtype)
    if chunk_cols is not None:
        oc_ref, tok_ref = rest[0], rest[1]
        jblk, lo, width = chunk_cols

        @pl.when(pl.program_id(1) == jblk)
        def _():
            nchunk = oc_ref.shape[0]
            for q in range(width // LANES):
                tok_ref[q] = res[:, lo + q * LANES:lo + (q + 1) * LANES]
                for t in range(S5_CHUNK):
                    c0 = (q * S5_CHUNK + t) * LANES
                    oc_ref[:, c0:c0 + LANES] = tok_ref[q, pl.ds(t, nchunk, stride=S5_CHUNK), :].astype(oc_ref.dtype)


def _modmm(h, g, shift, scale, w, layer, rows_per_mod, bm, bn, chunk_cols=None):
    r, d = h.shape
    n = w.shape[2]
    bm = min(bm, r)
    mod_idx = lambda i, j: ((i * bm) // rows_per_mod, 0, 0)
    out_specs = pl.BlockSpec((bm, bn), lambda i, j: (i, j))
    out_shape = jax.ShapeDtypeStruct((r, n), BF16)
    scratch = [pltpu.VMEM((bm, d), BF16)]
    cc = None
    if chunk_cols is not None:
        start, width = chunk_cols
        assert start // bn == (start + width - 1) // bn
        cc = (start // bn, start % bn, width)
        out_specs = [out_specs, pl.BlockSpec((bm // S5_CHUNK, S5_CHUNK * width), lambda i, j: (i, 0))]
        out_shape = [out_shape, jax.ShapeDtypeStruct((r // S5_CHUNK, S5_CHUNK * width), BF16)]
        scratch = [pltpu.VMEM((width // LANES, bm, LANES), F32)] + scratch
    return pl.pallas_call(
        functools.partial(_modmm_kernel, chunk_cols=cc),
        grid=(r // bm, n // bn),
        in_specs=[pl.BlockSpec((bm, d), lambda i, j: (i, 0)),
                  pl.BlockSpec((1, d), lambda i, j: (0, 0)),
                  pl.BlockSpec((1, 1, d), mod_idx),
                  pl.BlockSpec((1, 1, d), mod_idx),
                  pl.BlockSpec((None, d, bn), lambda i, j: (layer, 0, j))],
        out_specs=out_specs,
        out_shape=out_shape,
        scratch_shapes=scratch,
        compiler_params=_params(("parallel", "arbitrary")),
        name="modmm",
    )(h, g.reshape(1, d), shift, scale, w)


def _resmm_kernel(*refs, n_in):
    x_refs = refs[:n_in]
    w_ref, h_ref, gate_ref, o_ref = refs[n_in:]
    off = 0
    acc = None
    for x_ref in x_refs:
        k = x_ref.shape[1]
        part = jnp.dot(x_ref[...].astype(BF16), w_ref[off:off + k, :].astype(BF16), preferred_element_type=F32)
        acc = part if acc is None else acc + part
        off += k
    o_ref[...] = h_ref[...] + gate_ref[0] * acc


def _resmm(xs, w, layer, h, gate, rows_per_mod, bm, bn):
    r, n = h.shape
    bm = min(bm, r)
    k = w.shape[1]
    w_mode = pl.Buffered(1) if bn == n else None
    in_specs = [pl.BlockSpec((bm, x.shape[1]), lambda i, j: (i, 0)) for x in xs]
    in_specs += [pl.BlockSpec((None, k, bn), lambda i, j: (layer, 0, j), pipeline_mode=w_mode),
                 pl.BlockSpec((bm, bn), lambda i, j: (i, j)),
                 pl.BlockSpec((1, 1, bn), lambda i, j: ((i * bm) // rows_per_mod, 0, j))]
    return pl.pallas_call(
        functools.partial(_resmm_kernel, n_in=len(xs)),
        grid=(r // bm, n // bn),
        in_specs=in_specs,
        out_specs=pl.BlockSpec((bm, bn), lambda i, j: (i, j)),
        out_shape=jax.ShapeDtypeStruct((r, n), F32),
        compiler_params=_params(("parallel", "arbitrary")),
        name="resmm",
    )(*xs, w, h, gate)


def _shortconv_kernel(x_ref, w_ref, b_ref, o_ref, pad_ref, *, length, chunk):
    cb = x_ref.shape[-1]
    zeros = jnp.zeros((8, cb), F32)
    pad_ref[0:8, :] = zeros
    pad_ref[length + 8:length + 16, :] = zeros
    pad_ref[8:length + 8, :] = x_ref[0].astype(F32)
    w0, w1, w2 = w_ref[0:1, :], w_ref[1:2, :], w_ref[2:3, :]
    b = b_ref[...]
    for c in range(length // chunk):
        r = c * chunk
        o_ref[0, r:r + chunk, :] = (pad_ref[r + 7:r + 7 + chunk, :] * w0 + pad_ref[r + 8:r + 8 + chunk, :] * w1
                                    + pad_ref[r + 9:r + 9 + chunk, :] * w2 + b)


def _shortconv(a, w, b, width):
    bsz, length, _ = a.shape
    cb = 256
    chunk = min(512, length)
    return pl.pallas_call(
        functools.partial(_shortconv_kernel, length=length, chunk=chunk),
        grid=(bsz, width // cb),
        in_specs=[pl.BlockSpec((1, length, cb), lambda i, j: (i, 0, j)),
                  pl.BlockSpec((3, cb), lambda i, j: (0, j)),
                  pl.BlockSpec((1, cb), lambda i, j: (0, j))],
        out_specs=pl.BlockSpec((1, length, cb), lambda i, j: (i, 0, j)),
        out_shape=jax.ShapeDtypeStruct((bsz, length, width), F32),
        scratch_shapes=[pltpu.VMEM((length + 16, cb), F32)],
        compiler_params=_params(("parallel", "parallel")),
        name="hy_shortconv",
    )(a, w, b.reshape(1, width))


def _taps_kernel(z_ref, w1_ref, b1_ref, w2_ref, b2_ref, w3_ref, f_ref, dl_ref, o_ref, *, length):
    z = z_ref[...]
    h = jnp.sin(f_ref[0:1, :] * (jnp.dot(z, w1_ref[...], preferred_element_type=F32, precision=HIGHEST)
                                  + b1_ref[...]))
    h = jnp.sin(f_ref[1:2, :] * (jnp.dot(h, w2_ref[...], preferred_element_type=F32, precision=HIGHEST)
                                  + b2_ref[...]))
    h = jnp.dot(h, w3_ref[...], preferred_element_type=F32, precision=HIGHEST)
    t = z[:, 0:1]
    h = h * jnp.exp(-t * jnp.abs(dl_ref[...]))
    rb = z.shape[0]
    row = lax.broadcasted_iota(jnp.int32, h.shape, 0) + pl.program_id(0) * rb
    o_ref[...] = jnp.where(row == length, 0.0, h)


def _hyena_taps(length, w1, b1, w2, b2, w3, freq, width):
    n = 2 * length
    hid = w1.shape[1]
    pos = jnp.arange(n)
    idx = jnp.where(pos < length, pos, n - pos).astype(F32)
    t = (idx / max(length - 1, 1))[:, None]
    bands = jnp.linspace(1e-4, HY_BANDS - 1, HY_BANDS, dtype=F32)
    ang = (2.0 * math.pi * idx / length)[:, None] * bands[None]
    emb = 1 + 2 * HY_BANDS
    z = jnp.concatenate([t, jnp.cos(ang), -jnp.sin(ang), jnp.zeros((n, 64 - emb), F32)], axis=-1)
    w1p = jnp.concatenate([w1.astype(F32), jnp.zeros((64 - emb, hid), F32)], axis=0)
    max_decay = math.log(HY_TARGET) / HY_FAST_DECAY
    min_decay = math.log(HY_TARGET) / HY_SLOW_DECAY
    deltas = jnp.tile(jnp.linspace(min_decay, max_decay, width, dtype=F32), HY_ORDER)[None]
    oc = HY_ORDER * width
    rb = min(1024, length)
    half = length // rb
    return pl.pallas_call(
        functools.partial(_taps_kernel, length=length),
        grid=(n // rb,),
        in_specs=[pl.BlockSpec((rb, 64), lambda i: (i, 0)),
                  _const_spec((64, hid)), _const_spec((1, hid)),
                  _const_spec((hid, hid)), _const_spec((1, hid)),
                  pl.BlockSpec((hid, oc), lambda i: (0, i // half)),
                  _const_spec((2, hid)), _const_spec((1, oc))],
        out_specs=pl.BlockSpec((rb, oc), lambda i: (i, 0)),
        out_shape=jax.ShapeDtypeStruct((n, oc), F32),
        compiler_params=_params(("parallel",)),
        name="hy_taps",
    )(z, w1p, b1.reshape(1, hid).astype(F32), w2.astype(F32), b2.reshape(1, hid).astype(F32),
      w3.astype(F32), freq.astype(F32), deltas)


def _bf16(x):
    return jnp.asarray(x).astype(BF16)


@functools.lru_cache(maxsize=None)
def _dft2_consts(length):
    n = 2 * length
    n2 = LANES
    n1 = n // n2
    hf = n1 // 2
    j = np.arange(n2)[:, None, None]
    k1 = np.arange(n1)[None, :, None]
    m1 = np.arange(n1)[None, None, :]
    ph = -2.0 * np.pi * (j * k1 / n + (m1 * k1 % n1) / n1)
    mr, mi = np.cos(ph), np.sin(ph)
    g1 = np.concatenate([np.concatenate([mr[:, :, :hf], -mi[:, :, :hf]], 2),
                         np.concatenate([mi[:, :, :hf], mr[:, :, :hf]], 2)], 1)
    g1f = np.concatenate([mr, mi], 1)
    mrt = np.transpose(mr, (0, 2, 1))[:, :hf] / n
    mit = -np.transpose(mi, (0, 2, 1))[:, :hf] / n
    g1i = np.concatenate([np.concatenate([mrt, -mit], 2),
                          np.concatenate([mit, mrt], 2)], 1)
    a = np.arange(n2)
    ph2 = -2.0 * np.pi * ((a[:, None] * a[None, :]) % n2) / n2
    fr, fi = np.cos(ph2), np.sin(ph2)
    g2 = np.block([[fr, -fi], [fi, fr]])
    g2i = np.block([[fr, fi], [-fi, fr]])
    f32 = lambda m: np.asarray(m, np.float32)
    return dict(g1=f32(g1), g1i=f32(g1i), g2=f32(g2), g2i=f32(g2i), g1f=f32(g1f))


@functools.lru_cache(maxsize=None)
def _dft1_consts(length):
    n = 2 * length
    a = np.arange(n)
    ph = -2.0 * np.pi * ((a[:, None] * a[None, :]) % n) / n
    fr, fi = np.cos(ph), np.sin(ph)
    gf = np.block([[fr[:, :length], -fi[:, :length]], [fi[:, :length], fr[:, :length]]])
    gi = np.block([[fr[:length], fi[:length]], [-fi[:length], fr[:length]]]) / n
    gff = np.concatenate([fr, fi], 0)
    f32 = lambda m: np.asarray(m, np.float32)
    return dict(gf=f32(gf), gi=f32(gi), gff=f32(gff))


def _fspec2_kernel(k_ref, g1_ref, g2_ref, kr_ref, ki_ref, *, n1):
    inv = 1.0 / jnp.sum(jnp.abs(k_ref[...]), axis=0, keepdims=True)

    def s1(j, c):
        x = k_ref[pl.ds(j, n1, stride=LANES), :].astype(BF16)
        a = jnp.dot(g1_ref[j], x, preferred_element_type=F32)
        kr_ref[pl.ds(j, n1, stride=LANES), :] = a[:n1]
        ki_ref[pl.ds(j, n1, stride=LANES), :] = a[n1:]
        return c

    lax.fori_loop(0, LANES, s1, 0, unroll=8)

    def s2(k1, c):
        r0 = pl.multiple_of(k1 * LANES, LANES)
        a = jnp.concatenate([kr_ref[pl.ds(r0, LANES), :], ki_ref[pl.ds(r0, LANES), :]], axis=0).astype(BF16)
        x = jnp.dot(g2_ref[...], a, preferred_element_type=F32) * inv
        kr_ref[pl.ds(r0, LANES), :] = x[:LANES]
        ki_ref[pl.ds(r0, LANES), :] = x[LANES:]
        return c

    lax.fori_loop(0, n1, s2, 0, unroll=4)


def _filter_spectrum2(taps):
    n, oc = taps.shape
    cst = _dft2_consts(n // 2)
    n1 = n // LANES
    cb = LANES
    g1, g2 = _bf16(cst["g1f"]), _bf16(cst["g2"])
    out = jax.ShapeDtypeStruct((n, oc), F32)
    return pl.pallas_call(
        functools.partial(_fspec2_kernel, n1=n1),
        grid=(oc // cb,),
        in_specs=[pl.BlockSpec((n, cb), lambda i: (0, i)), _const_spec(g1.shape), _const_spec(g2.shape)],
        out_specs=[pl.BlockSpec((n, cb), lambda i: (0, i))] * 2,
        out_shape=[out, out],
        compiler_params=_params(("parallel",)),
        name="hy_fspec2",
    )(taps, g1, g2)


def _fspec1_kernel(k_ref, g_ref, kr_ref, ki_ref):
    k = k_ref[...]
    n = k.shape[0]
    inv = 1.0 / jnp.sum(jnp.abs(k), axis=0, keepdims=True)
    x = jnp.dot(g_ref[...], k.astype(BF16), preferred_element_type=F32) * inv
    kr_ref[...] = x[:n]
    ki_ref[...] = x[n:]


def _filter_spectrum1(taps):
    n, oc = taps.shape
    g = _bf16(_dft1_consts(n // 2)["gff"])
    cb = 256
    out = jax.ShapeDtypeStruct((n, oc), F32)
    return pl.pallas_call(
        _fspec1_kernel,
        grid=(oc // cb,),
        in_specs=[pl.BlockSpec((n, cb), lambda i: (0, i)), _const_spec(g.shape)],
        out_specs=[pl.BlockSpec((n, cb), lambda i: (0, i))] * 2,
        out_shape=[out, out],
        compiler_params=_params(("parallel",)),
        name="hy_fspec1",
    )(taps, g)


def _hyconv2_kernel(v_ref, x_ref, kr_ref, ki_ref, bias_ref, g1_ref, g1i_ref, g2_ref, g2i_ref, o_ref,
                    ar_ref, ai_ref, *, n1):
    hf = n1 // 2

    def s1(j, c):
        xa = v_ref[0, pl.ds(j, hf, stride=LANES), :]
        xb = v_ref[1, pl.ds(j, hf, stride=LANES), :]
        x = jnp.concatenate([xa, xb], axis=0).astype(BF16)
        a = jnp.dot(g1_ref[j], x, preferred_element_type=F32)
        ar_ref[pl.ds(j, n1, stride=LANES), :] = a[:n1]
        ai_ref[pl.ds(j, n1, stride=LANES), :] = a[n1:]
        return c

    lax.fori_loop(0, LANES, s1, 0, unroll=8)

    def s2(k1, c):
        r0 = pl.multiple_of(k1 * LANES, LANES)
        a = jnp.concatenate([ar_ref[pl.ds(r0, LANES), :], ai_ref[pl.ds(r0, LANES), :]], axis=0).astype(BF16)
        x = jnp.dot(g2_ref[...], a, preferred_element_type=F32)
        xr, xi = x[:LANES], x[LANES:]
        kr = kr_ref[pl.ds(r0, LANES), :]
        ki = ki_ref[pl.ds(r0, LANES), :]
        y = jnp.concatenate([xr * kr - xi * ki, xr * ki + xi * kr], axis=0).astype(BF16)
        b = jnp.dot(g2i_ref[...], y, preferred_element_type=F32)
        ar_ref[pl.ds(r0, LANES), :] = b[:LANES]
        ai_ref[pl.ds(r0, LANES), :] = b[LANES:]
        return c

    lax.fori_loop(0, n1, s2, 0, unroll=4)

    def s3(j, c):
        b = jnp.concatenate([ar_ref[pl.ds(j, n1, stride=LANES), :], ai_ref[pl.ds(j, n1, stride=LANES), :]],
                            axis=0).astype(BF16)
        y = jnp.dot(g1i_ref[j], b, preferred_element_type=F32)
        o_ref[0, pl.ds(j, hf, stride=LANES), :] = y[:hf]
        o_ref[1, pl.ds(j, hf, stride=LANES), :] = y[hf:]
        return c

    lax.fori_loop(0, LANES, s3, 0, unroll=8)
    bias = bias_ref[...]
    for b in range(2):
        o_ref[b] = x_ref[b] * (o_ref[b] + bias * v_ref[b])


def _hyconv2(va, v_col, xa, x_col, kr, ki, k_col, bias, length):
    bsz = va.shape[0]
    width = bias.shape[-1]
    cst = _dft2_consts(length)
    n = 2 * length
    n1 = n // LANES
    cb = LANES
    ncb = width // cb
    return pl.pallas_call(
        functools.partial(_hyconv2_kernel, n1=n1),
        grid=(ncb, bsz // 2),
        in_specs=[pl.BlockSpec((2, length, cb), lambda c, q: (q, 0, v_col + c)),
                  pl.BlockSpec((2, length, cb), lambda c, q: (q, 0, x_col + c)),
                  pl.BlockSpec((n, cb), lambda c, q: (0, k_col + c), pipeline_mode=pl.Buffered(1)),
                  pl.BlockSpec((n, cb), lambda c, q: (0, k_col + c), pipeline_mode=pl.Buffered(1)),
                  pl.BlockSpec((1, cb), lambda c, q: (0, c)),
                  _const_spec(cst["g1"].shape), _const_spec(cst["g1i"].shape),
                  _const_spec(cst["g2"].shape), _const_spec(cst["g2i"].shape)],
        out_specs=pl.BlockSpec((2, length, cb), lambda c, q: (q, 0, c)),
        out_shape=jax.ShapeDtypeStruct((bsz, length, width), F32),
        scratch_shapes=[pltpu.VMEM((n, cb), F32), pltpu.VMEM((n, cb), F32)],
        compiler_params=_params(("parallel", "arbitrary")),
        name="hy_conv2",
    )(va, xa, kr, ki, bias.reshape(1, width), _bf16(cst["g1"]), _bf16(cst["g1i"]), _bf16(cst["g2"]),
      _bf16(cst["g2i"]))


def _hyconv1_kernel(v_ref, x_ref, kr_ref, ki_ref, bias_ref, gf_ref, gi_ref, o_ref):
    length = v_ref.shape[1]
    n = 2 * length
    x = jnp.concatenate([v_ref[0], v_ref[1]], axis=0).astype(BF16)
    s = jnp.dot(gf_ref[...], x, preferred_element_type=F32)
    sr, si = s[:n], s[n:]
    kr, ki = kr_ref[...], ki_ref[...]
    y = jnp.concatenate([sr * kr - si * ki, sr * ki + si * kr], axis=0).astype(BF16)
    out = jnp.dot(gi_ref[...], y, preferred_element_type=F32)
    bias = bias_ref[...]
    for b in range(2):
        o_ref[b] = x_ref[b] * (out[b * length:(b + 1) * length] + bias * v_ref[b])


def _hyconv1(va, v_col, xa, x_col, kr, ki, k_col, bias, length):
    bsz = va.shape[0]
    width = bias.shape[-1]
    cst = _dft1_consts(length)
    n = 2 * length
    cb = LANES
    return pl.pallas_call(
        _hyconv1_kernel,
        grid=(width // cb, bsz // 2),
        in_specs=[pl.BlockSpec((2, length, cb), lambda c, q: (q, 0, v_col + c)),
                  pl.BlockSpec((2, length, cb), lambda c, q: (q, 0, x_col + c)),
                  pl.BlockSpec((n, cb), lambda c, q: (0, k_col + c)),
                  pl.BlockSpec((n, cb), lambda c, q: (0, k_col + c)),
                  pl.BlockSpec((1, cb), lambda c, q: (0, c)),
                  _const_spec(cst["gf"].shape), _const_spec(cst["gi"].shape)],
        out_specs=pl.BlockSpec((2, length, cb), lambda c, q: (q, 0, c)),
        out_shape=jax.ShapeDtypeStruct((bsz, length, width), F32),
        compiler_params=_params(("parallel", "arbitrary")),
        name="hy_conv1",
    )(va, xa, kr, ki, bias.reshape(1, width), _bf16(cst["gf"]), _bf16(cst["gi"]))


def _hyena(a, conv_w, conv_b, w1, b1, w2, b2, w3, freq, bias, width):
    bsz, length, _ = a.shape
    p = _shortconv(a, conv_w, conv_b, 3 * width)
    taps = _hyena_taps(length, w1, b1, w2, b2, w3, freq, width)
    two_stage = (2 * length) % (LANES * 16) == 0
    kr, ki = (_filter_spectrum2 if two_stage else _filter_spectrum1)(taps)
    conv = _hyconv2 if two_stage else _hyconv1
    ncb = width // LANES
    z = conv(p, 0, p, ncb, kr, ki, 0, bias[0], length)
    return conv(z, 0, p, 2 * ncb, kr, ki, ncb, bias[1], length)


def _ret_kernel(lg_ref, q_ref, k_ref, v_ref, g_ref, s0_ref, o_ref, sfin_ref, sb_ref, *, length, chunk, dk):
    hp = pl.program_id(1)
    nc = length // chunk
    dv = LANES
    row = lax.broadcasted_iota(jnp.int32, (chunk, chunk), 0)
    col = lax.broadcasted_iota(jnp.int32, (chunk, chunk), 1)
    diff = (row - col).astype(F32)
    lane = lax.broadcasted_iota(jnp.int32, (chunk, 2 * dk), 1)
    pos = lax.broadcasted_iota(jnp.int32, (chunk, 2 * dk), 0).astype(F32)
    ones_s = jnp.ones((2 * dk, dv), F32)
    kscale = dk ** -0.5
    tn = (((0,), (0,)), ((), ()))
    nt = (((1,), (1,)), ((), ()))
    hd = []
    for hh in range(2):
        lgf = lg_ref[0, hp * 2 + hh]
        lgb = lg_ref[1, hp * 2 + hh]
        hd.append(dict(
            decay=jnp.where(diff >= 0.0, jnp.exp(jnp.maximum(diff, 0.0) * lgf),
                            jnp.exp(jnp.maximum(-diff, 0.0) * lgb)),
            qmask=(lane >= dk * hh) & (lane < dk * (hh + 1)),
            qf=jnp.exp((pos + 1.0) * lgf), qb=jnp.exp((chunk - pos) * lgb),
            kf=jnp.exp((chunk - 1.0 - pos) * lgf) * kscale, kb=jnp.exp(pos * lgb) * kscale,
            cdf=jnp.exp(ones_s * (chunk * lgf)), cdb=jnp.exp(ones_s * (chunk * lgb)),
            vs=slice(dv * hh, dv * (hh + 1))))

    def bstep(i, states):
        n = nc - 1 - i
        r0 = pl.multiple_of(n * chunk, chunk)
        k = k_ref[0, pl.ds(r0, chunk), :].astype(F32)
        out = []
        for hh, c in enumerate(hd):
            sb_ref[hh, n] = states[hh]
            v = v_ref[0, pl.ds(r0, chunk), c['vs']]
            inc = lax.dot_general((k * c['kb']).astype(BF16), v, tn, preferred_element_type=F32)
            out.append(c['cdb'] * states[hh] + inc)
        return tuple(out)

    fin = lax.fori_loop(0, nc, bstep, (s0_ref[0, 1, 0], s0_ref[0, 1, 1]))
    sfin_ref[0, 1, 0] = fin[0]
    sfin_ref[0, 1, 1] = fin[1]

    def fstep(n, states):
        r0 = pl.multiple_of(n * chunk, chunk)
        qa = q_ref[0, pl.ds(r0, chunk), :].astype(F32)
        k = k_ref[0, pl.ds(r0, chunk), :].astype(F32)
        ks = (k * kscale).astype(BF16)
        out = []
        for hh, c in enumerate(hd):
            q = jnp.where(c['qmask'], qa, 0.0)
            v = v_ref[0, pl.ds(r0, chunk), c['vs']]
            s = lax.dot_general(q.astype(BF16), ks, nt, preferred_element_type=F32)
            y = jnp.dot((s * c['decay']).astype(BF16), v, preferred_element_type=F32)
            y = y + jnp.dot((q * c['qf']).astype(BF16), states[hh].astype(BF16), preferred_element_type=F32)
            y = y + jnp.dot((q * c['qb']).astype(BF16), sb_ref[hh, n].astype(BF16), preferred_element_type=F32)
            y = y * lax.rsqrt(jnp.mean(y * y, axis=-1, keepdims=True) + EPS)
            g = g_ref[0, pl.ds(r0, chunk), c['vs']].astype(F32)
            o_ref[0, pl.ds(r0, chunk), c['vs']] = (jax.nn.silu(g) * y).astype(o_ref.dtype)
            inc = lax.dot_general((k * c['kf']).astype(BF16), v, tn, preferred_element_type=F32)
            out.append(c['cdf'] * states[hh] + inc)
        return tuple(out)

    fin = lax.fori_loop(0, nc, fstep, (s0_ref[0, 0, 0], s0_ref[0, 0, 1]), unroll=2 if nc % 2 == 0 else 1)
    sfin_ref[0, 0, 0] = fin[0]
    sfin_ref[0, 0, 1] = fin[1]


def _retention(a, log_gamma, s0, q_off, dk):
    bsz, length, _ = a.shape
    heads = RET_HEADS
    dv = 2 * dk
    assert dv == LANES
    chunk = min(RET_CHUNK, length)
    qb = q_off // (2 * dk)
    kb = qb + heads // 2
    vb = (q_off + 2 * heads * dk) // (2 * dv)
    gb = vb + heads // 2
    seq = lambda blk, off: pl.BlockSpec((1, length, blk), lambda b, h, lg: (b, 0, off + h))
    st = pl.BlockSpec((1, 2, 2, 2 * dk, dv), lambda b, h, lg: (b, 0, h, 0, 0))
    grid_spec = pltpu.PrefetchScalarGridSpec(
        num_scalar_prefetch=1,
        grid=(bsz, heads // 2),
        in_specs=[seq(2 * dk, qb), seq(2 * dk, kb), seq(2 * dv, vb), seq(2 * dv, gb), st],
        out_specs=[pl.BlockSpec((1, length, 2 * dv), lambda b, h, lg: (b, 0, h)), st],
        scratch_shapes=[pltpu.VMEM((2, length // chunk, 2 * dk, dv), F32)],
    )
    return pl.pallas_call(
        functools.partial(_ret_kernel, length=length, chunk=chunk, dk=dk),
        grid_spec=grid_spec,
        out_shape=[jax.ShapeDtypeStruct((bsz, length, heads * dv), BF16),
                   jax.ShapeDtypeStruct(s0.shape, F32)],
        compiler_params=_params(("parallel", "parallel")),
        name="retention",
    )(log_gamma, a, a, a, a, s0)


@functools.lru_cache(maxsize=None)
def _s5_expanders():
    gl = LANES // S5_GROUP
    t_len = S5_CHUNK
    ex_to = np.zeros((t_len * S5_GROUP, t_len * LANES), np.float32)
    for t in range(t_len):
        for h in range(gl):
            for o in range(S5_GROUP):
                ex_to[t * S5_GROUP + o, t * LANES + h * S5_GROUP + o] = 1.0
    ex_p = np.kron(np.eye(4, dtype=np.float32), np.tile(np.eye(S5_STATE, dtype=np.float32), (1, gl)))
    g_sgi = (np.arange(t_len * LANES) // S5_GROUP) % gl
    g_tho = (np.arange(t_len * LANES) // S5_GROUP) % gl
    g_hp = np.arange(gl * S5_STATE) // S5_STATE
    eq = lambda a, b: (a[:, None] == b[None, :]).astype(np.float32)
    return ex_to, ex_p, eq(g_sgi, g_tho), eq(g_sgi, np.tile(g_hp, 4)), eq(g_hp, g_tho)


def _s5_mats(lam_re, lam_im, log_step, b_re, b_im, c_re, c_im):
    t_len = S5_CHUNK
    lr = jnp.minimum(lam_re.astype(F32), -1e-4)
    li = lam_im.astype(F32)
    step = jnp.exp(log_step.astype(F32))[..., None]
    dr, di = lr * step, li * step
    d = jnp.arange(t_len + 1, dtype=F32)[:, None, None, None]
    mag = jnp.exp(d * dr)
    pr, pi = mag * jnp.cos(d * di), mag * jnp.sin(d * di)
    nr, ni = pr[1] - 1.0, pi[1]
    den = lr * lr + li * li
    cr, ci = (nr * lr + ni * li) / den, (ni * lr - nr * li) / den
    bbr = cr[..., None] * b_re - ci[..., None] * b_im
    bbi = cr[..., None] * b_im + ci[..., None] * b_re
    ctr = jnp.swapaxes(c_re.astype(F32), 2, 3)
    cti = jnp.swapaxes(c_im.astype(F32), 2, 3)
    groups = lr.shape[1]
    gl = LANES // S5_GROUP
    nq = groups // gl
    tw = t_len * S5_GROUP
    tq = lambda e: e.reshape((nq, gl) + e.shape[1:])
    ex_to, ex_p, m_rc, m_rp, m_pc = _s5_expanders()

    def spread(compact, expand, mask):
        return jnp.matmul(compact.astype(BF16), jnp.asarray(expand, BF16),
                          preferred_element_type=BF16) * jnp.asarray(mask, BF16)

    mr = (bbr[..., :, None] * ctr[..., None, :] - bbi[..., :, None] * cti[..., None, :]).reshape(
        2, groups, S5_STATE, S5_GROUP * S5_GROUP)
    mi = (bbr[..., :, None] * cti[..., None, :] + bbi[..., :, None] * ctr[..., None, :]).reshape(
        2, groups, S5_STATE, S5_GROUP * S5_GROUP)
    kern = (jnp.einsum('dxgp,xgpn->xgdn', pr, mr, precision=HIGHEST)
            - jnp.einsum('dxgp,xgpn->xgdn', pi, mi, precision=HIGHEST))
    kern = jnp.transpose(kern.reshape(2, groups, t_len + 1, S5_GROUP, S5_GROUP), (0, 1, 3, 2, 4))
    kf = kern[0, :, :, :t_len].reshape(groups, S5_GROUP, tw)
    kb = kern[1, :, :, :t_len][:, :, ::-1].reshape(groups, S5_GROUP, tw)
    zpad = jnp.zeros((groups, S5_GROUP, tw - S5_GROUP), F32)
    kf = jnp.concatenate([zpad, kf], axis=-1)
    kb = jnp.concatenate([kb, zpad], axis=-1)
    toe = jnp.stack([kf[..., (t_len - 1 - s) * S5_GROUP:(t_len - 1 - s) * S5_GROUP + tw]
                     + kb[..., (t_len - 1 - s) * S5_GROUP:(t_len - 1 - s) * S5_GROUP + tw]
                     for s in range(t_len)], axis=1)
    rows_sgi = lambda e: jnp.transpose(tq(e), (0, 2, 1, 3, 4)).reshape(nq, t_len * LANES, e.shape[-1])
    w1t = spread(rows_sgi(toe), ex_to, m_rc)
    btr, bti = jnp.swapaxes(bbr, 2, 3), jnp.swapaxes(bbi, 2, 3)

    def e_part(x, order):
        pw_r = jnp.transpose(pr[order, x], (1, 0, 2))[:, :, None, :]
        pw_i = jnp.transpose(pi[order, x], (1, 0, 2))[:, :, None, :]
        br, bi = btr[x][:, None], bti[x][:, None]
        return [rows_sgi(e) for e in (pw_r * br - pw_i * bi, pw_r * bi + pw_i * br)]

    prt, pit = jnp.transpose(pr, (1, 2, 3, 0)), jnp.transpose(pi, (1, 2, 3, 0))

    def q_part(x, order):
        pw_r, pw_i = prt[x][..., order][..., None], pit[x][..., order][..., None]
        c_r, c_i = ctr[x][:, :, None, :], cti[x][:, :, None, :]
        return [spread(tq(e).reshape(nq, gl * S5_STATE, tw), ex_to, m_pc)
                for e in (c_r * pw_r - c_i * pw_i, -(c_r * pw_i + c_i * pw_r))]

    ecat = jnp.concatenate(e_part(0, jnp.arange(t_len - 1, -1, -1)) + e_part(1, jnp.arange(t_len)), axis=2)
    w1e = spread(ecat, ex_p, m_rp)
    w2 = jnp.concatenate(q_part(0, jnp.arange(1, t_len + 1)) + q_part(1, jnp.arange(t_len, 0, -1)), axis=1)
    pl2 = lambda e: e.reshape(nq, gl * S5_STATE)
    lam_t = jnp.stack([pl2(pr[t_len, 0]), pl2(pi[t_len, 0]), pl2(pr[t_len, 1]), pl2(pi[t_len, 1])], axis=1)
    return w1t.astype(BF16), w1e.astype(BF16), w2.astype(BF16), lam_t


def _s5_kernel(u_ref, w1t_ref, w1e_ref, w2_ref, lam_ref, s0_ref, y_ref, sfin_ref, r_ref, *, nc):
    sw = (LANES // S5_GROUP) * S5_STATE
    yw = S5_CHUNK * LANES
    nb = u_ref.shape[0]
    u = u_ref[...].reshape(nb * nc, yw)
    r_ref[:, 0:yw] = jnp.dot(u, w1t_ref[0], preferred_element_type=F32)
    r_ref[:, yw:yw + 4 * sw] = jnp.dot(u, w1e_ref[0], preferred_element_type=F32)
    lfr, lfi, lbr, lbi = (lam_ref[0, i:i + 1, :] for i in range(4))
    cols = [slice(yw + i * sw, yw + (i + 1) * sw) for i in range(4)]
    rid = lax.broadcasted_iota(jnp.int32, (SUBLANES, sw), 0)
    nblk = nc // SUBLANES

    def block(kb, carry):
        out = []
        for b in range(nb):
            fr, fi, br, bi = carry[4 * b:4 * b + 4]
            rf = pl.ds(pl.multiple_of(b * nc + kb * SUBLANES, SUBLANES), SUBLANES)
            rb = pl.ds(pl.multiple_of(b * nc + (nblk - 1 - kb) * SUBLANES, SUBLANES), SUBLANES)
            efr, efi = r_ref[rf, cols[0]], r_ref[rf, cols[1]]
            ebr, ebi = r_ref[rb, cols[2]], r_ref[rb, cols[3]]
            xfr, xfi, xbr, xbi = efr, efi, ebr, ebi
            for s in range(SUBLANES):
                xfr = jnp.where(rid == s, fr, xfr)
                xfi = jnp.where(rid == s, fi, xfi)
                fr, fi = (lfr * fr - lfi * fi + efr[s:s + 1], lfr * fi + lfi * fr + efi[s:s + 1])
                z = SUBLANES - 1 - s
                xbr = jnp.where(rid == z, br, xbr)
                xbi = jnp.where(rid == z, bi, xbi)
                br, bi = (lbr * br - lbi * bi + ebr[z:z + 1], lbr * bi + lbi * br + ebi[z:z + 1])
            r_ref[rf, cols[0]] = xfr
            r_ref[rf, cols[1]] = xfi
            r_ref[rb, cols[2]] = xbr
            r_ref[rb, cols[3]] = xbi
            out += [fr, fi, br, bi]
        return tuple(out)

    fin = lax.fori_loop(0, nblk, block, tuple(s0_ref[b, 0, i:i + 1, :] for b in range(nb) for i in range(4)))
    for b in range(nb):
        for i in range(4):
            sfin_ref[b, 0, i:i + 1, :] = fin[4 * b + i]
    xin = r_ref[:, yw:yw + 4 * sw].astype(BF16)
    y = r_ref[:, 0:yw] + jnp.dot(xin, w2_ref[0], preferred_element_type=F32)
    y_ref[...] = y.reshape(nb, nc, yw)


def _s5(uc, w1t, w1e, w2, lam_t, s0):
    bsz, nc, cw = uc.shape
    nq = w1t.shape[0]
    tw = cw // nq
    sw = lam_t.shape[-1]
    nb = max(1, min(bsz, 512 // nc))
    assert bsz % nb == 0
    st = pl.BlockSpec((nb, 1, 4, sw), lambda q, b: (b, q, 0, 0))
    single = pl.Buffered(1)
    return pl.pallas_call(
        functools.partial(_s5_kernel, nc=nc),
        grid=(nq, bsz // nb),
        in_specs=[pl.BlockSpec((nb, nc, tw), lambda q, b: (b, 0, q)),
                  pl.BlockSpec((1,) + w1t.shape[1:], lambda q, b: (q, 0, 0), pipeline_mode=single),
                  pl.BlockSpec((1,) + w1e.shape[1:], lambda q, b: (q, 0, 0), pipeline_mode=single),
                  pl.BlockSpec((1,) + w2.shape[1:], lambda q, b: (q, 0, 0), pipeline_mode=single),
                  pl.BlockSpec((1, 4, sw), lambda q, b: (q, 0, 0)),
                  st],
        out_specs=[pl.BlockSpec((nb, nc, tw), lambda q, b: (b, 0, q)), st],
        out_shape=[jax.ShapeDtypeStruct((bsz, nc, cw), F32), jax.ShapeDtypeStruct(s0.shape, F32)],
        scratch_shapes=[pltpu.VMEM((nb * nc, w1t.shape[2] + w1e.shape[2]), F32)],
        compiler_params=_params(("parallel", "arbitrary")),
        name="s5_scan",
    )(uc, w1t, w1e, w2, lam_t, s0)


def _s5out_kernel(y_ref, u_ref, d_ref, w_ref, b_ref, o_ref, tok_ref):
    nchunk = y_ref.shape[0]
    nq = tok_ref.shape[0]
    for q in range(nq):
        for t in range(S5_CHUNK):
            c0 = (q * S5_CHUNK + t) * LANES
            tok_ref[q, pl.ds(t, nchunk, stride=S5_CHUNK), :] = y_ref[:, c0:c0 + LANES]
    y = jnp.concatenate([tok_ref[q] for q in range(nq)], axis=-1)
    z = jax.nn.gelu(y + d_ref[...] * u_ref[...].astype(F32))
    gate = jnp.dot(z.astype(BF16), w_ref[...], preferred_element_type=F32) + b_ref[...]
    o_ref[...] = (z * jax.nn.sigmoid(gate)).astype(o_ref.dtype)


def _s5_out(yc, a, u_col, d, glu_w, glu_b):
    width = d.shape[-1]
    r = a.shape[0]
    bm = min(1024, r)
    return pl.pallas_call(
        _s5out_kernel,
        grid=(r // bm,),
        in_specs=[pl.BlockSpec((bm // S5_CHUNK, S5_CHUNK * width), lambda i: (i, 0)),
                  pl.BlockSpec((bm, width), lambda i: (i, u_col)),
                  _const_spec((1, width)), _const_spec((width, width)), _const_spec((1, width))],
        out_specs=pl.BlockSpec((bm, width), lambda i: (i, 0)),
        out_shape=jax.ShapeDtypeStruct((r, width), BF16),
        scratch_shapes=[pltpu.VMEM((width // LANES, bm, LANES), F32)],
        compiler_params=_params(("parallel",)),
        name="s5_out",
    )(yc, a, d.reshape(1, width).astype(F32), glu_w.astype(BF16), glu_b.reshape(1, width).astype(F32))


def _ffndown_kernel(*refs, rows, width, chunk, nk):
    halo = rows > 1
    if halo:
        g_ref, ga_ref, gb_ref, v_ref, cw_ref, cb_ref, w_ref, h_ref, gate_ref, o_ref, acta_ref, actb_ref, pad_ref = refs
    else:
        g_ref, v_ref, cw_ref, cb_ref, w_ref, h_ref, gate_ref, o_ref, acta_ref, actb_ref = refs
    i = pl.program_id(0)
    s = pl.program_id(1)
    bm, fk = g_ref.shape

    @pl.when(s == 0)
    def _():
        actb_ref[...] = jnp.zeros((bm, fk), BF16)
        o_ref[...] = jnp.zeros(o_ref.shape, F32)

    if halo:
        per_img = (rows * width) // bm
        top = (i % per_img) == 0
        bottom = (i % per_img) == per_img - 1
        pad_ref[0:width, :] = jnp.where(top, 0.0, ga_ref[...].astype(F32))
        pad_ref[width:width + bm, :] = g_ref[...].astype(F32)
        pad_ref[width + bm:2 * width + bm, :] = jnp.where(bottom, 0.0, gb_ref[...].astype(F32))
    col = lax.broadcasted_iota(jnp.int32, (chunk, LANES), 0) % width
    not_first = col != 0
    not_last = col != width - 1
    bias = cb_ref[...]
    wt = [cw_ref[t:t + 1, :] for t in range(9)]

    def phase(new_ref, old_ref):
        nchunks = bm // chunk
        ntiles = o_ref.shape[1] // MXU_COLS
        for c in range(nchunks):
            r = c * chunk
            for t in range(c * ntiles // nchunks, (c + 1) * ntiles // nchunks):
                nt = slice(t * MXU_COLS, (t + 1) * MXU_COLS)
                o_ref[:, nt] += jnp.dot(old_ref[...], w_ref[:, nt].astype(BF16), preferred_element_type=F32)
            for lb in range(fk // LANES):
                ls = slice(lb * LANES, (lb + 1) * LANES)
                w9 = [t[:, ls] for t in wt]
                if halo:
                    up = pad_ref[r:r + chunk, ls]
                    mid = pad_ref[width + r:width + r + chunk, ls]
                    dn = pad_ref[2 * width + r:2 * width + r + chunk, ls]
                    v0, v1, v2 = (up * w9[dw] + mid * w9[3 + dw] + dn * w9[6 + dw] for dw in range(3))
                else:
                    mid = g_ref[r:r + chunk, ls].astype(F32)
                    v0, v1, v2 = (mid * w9[3 + dw] for dw in range(3))
                left = jnp.where(not_first, pltpu.roll(v0, 1, 0), 0.0)
                right = jnp.where(not_last, pltpu.roll(v2, chunk - 1, 0), 0.0)
                gate = jax.nn.gelu(v1 + left + right + bias[:, ls])
                new_ref[r:r + chunk, ls] = (gate * v_ref[r:r + chunk, ls].astype(F32)).astype(BF16)

    @pl.when(s % 2 == 0)
    def _():
        phase(acta_ref, actb_ref)

    @pl.when(s % 2 == 1)
    def _():
        phase(actb_ref, acta_ref)

    @pl.when(s == nk)
    def _():
        o_ref[...] = h_ref[...] + gate_ref[0] * o_ref[...]


def _ffn_down(gv, conv_w, conv_b, w_down, layer, h, gate, rows_per_mod, rows, width):
    r, f2 = gv.shape
    dff = f2 // 2
    d = h.shape[1]
    bm = min(1024, r)
    fk = 512
    nk = dff // fk
    chunk = width
    assert bm % chunk == 0 and ((rows * width) % bm == 0 if rows > 1 else bm % width == 0)
    kc = lambda s: jnp.minimum(s, nk - 1)
    in_specs = [pl.BlockSpec((bm, fk), lambda i, s: (i, kc(s)))]
    args = [gv]
    scratch = [pltpu.VMEM((bm, fk), BF16), pltpu.VMEM((bm, fk), BF16)]
    if rows > 1:
        per = bm // width
        last = r // width - 1
        in_specs += [pl.BlockSpec((width, fk), lambda i, s: (jnp.maximum(i * per - 1, 0), kc(s))),
                     pl.BlockSpec((width, fk), lambda i, s: (jnp.minimum((i + 1) * per, last), kc(s)))]
        args += [gv, gv]
        scratch.append(pltpu.VMEM((bm + 2 * width, fk), F32))
    in_specs += [pl.BlockSpec((bm, fk), lambda i, s: (i, nk + kc(s))),
                 pl.BlockSpec((9, fk), lambda i, s: (0, kc(s))),
                 pl.BlockSpec((1, fk), lambda i, s: (0, kc(s))),
                 pl.BlockSpec((None, fk, d), lambda i, s: (layer, jnp.maximum(s - 1, 0), 0)),
                 pl.BlockSpec((bm, d), lambda i, s: (i, 0), pipeline_mode=pl.Buffered(1)),
                 pl.BlockSpec((1, 1, d), lambda i, s: ((i * bm) // rows_per_mod, 0, 0))]
    args += [gv, conv_w.reshape(9, dff).astype(F32), conv_b.reshape(1, dff).astype(F32), w_down, h, gate]
    return pl.pallas_call(
        functools.partial(_ffndown_kernel, rows=rows, width=width, chunk=chunk, nk=nk),
        grid=(r // bm, nk + 1),
        in_specs=in_specs,
        out_specs=pl.BlockSpec((bm, d), lambda i, s: (i, 0)),
        out_shape=jax.ShapeDtypeStruct((r, d), F32),
        scratch_shapes=scratch,
        compiler_params=_params(("parallel", "arbitrary")),
        name="ffn_down",
    )(*args)


def _rms_kernel(x_ref, g_ref, o_ref):
    x = x_ref[...]
    o_ref[...] = x * lax.rsqrt(jnp.mean(x * x, axis=-1, keepdims=True) + EPS) * g_ref[...]


def _rmsnorm(x, g):
    r, d = x.shape
    bm = min(1024, r)
    return pl.pallas_call(
        _rms_kernel,
        grid=(r // bm,),
        in_specs=[pl.BlockSpec((bm, d), lambda i: (i, 0)), _const_spec((1, d))],
        out_specs=pl.BlockSpec((bm, d), lambda i: (i, 0)),
        out_shape=jax.ShapeDtypeStruct((r, d), F32),
        compiler_params=_params(("parallel",)),
        name="final_norm",
    )(x, g.reshape(1, d).astype(F32))


def _mixer(h, mods, i, p, states, img_rows, img_width, full):
    bsz, length, d = h.shape
    r = bsz * length
    rpm = r // mods.shape[0]
    width = d // 4
    dk = d // 2 // RET_HEADS // 2
    h2 = h.reshape(r, d)
    q_off = 3 * width
    u_off = q_off + 2 * RET_HEADS * dk + 2 * (d // 2)
    a, uc = _modmm(h2, p['norm1_g'][i], mods[:, 0], mods[:, 1], p['w_in'], i, rpm, 1024, 1024,
                   chunk_cols=(u_off, width))
    a3 = a.reshape(bsz, length, -1)
    ret, ret_fin = _retention(a3, p['log_gamma'][i], states[0], q_off, dk)
    y5, s5_fin = _s5(uc.reshape(bsz, length // S5_CHUNK, -1), *p['s5_mats'][i], states[1])
    if not full:
        return None, (ret_fin, s5_fin)
    hy = _hyena(a3, p['hy_conv_w'][i], p['hy_conv_b'][i], p['hy_w1'][i], p['hy_b1'][i], p['hy_w2'][i],
                p['hy_b2'][i], p['hy_w3'][i], p['hy_freq'][i], p['hy_bias'][i], width)
    s5o = _s5_out(y5.reshape(r // S5_CHUNK, -1), a, u_off // width, p['s5_d'][i], p['s5_glu_w'][i],
                  p['s5_glu_b'][i])
    h2 = _resmm([hy.reshape(r, width), ret.reshape(r, d // 2), s5o], p['w_out'], i, h2, mods[:, 2], rpm, 512, d)
    gv = _modmm(h2, p['norm2_g'][i], mods[:, 3], mods[:, 4], p['ffn_w_up'], i, rpm, 1024, 1024)
    h2 = _ffn_down(gv, p['ffn_conv_w'][i], p['ffn_conv_b'][i], p['ffn_w_down'], i, h2, mods[:, 5], rpm,
                   img_rows, img_width)
    return h2.reshape(bsz, length, d), (ret_fin, s5_fin)


def kernel(x, c, ctx, c_ctx, ada_w, ada_b, norm1_g, w_in, hy_conv_w, hy_conv_b, hy_w1, hy_b1, hy_w2, hy_b2,
           hy_w3, hy_freq, hy_bias, ret_decay, s5_lam_re, s5_lam_im, s5_log_step, s5_b_re, s5_b_im, s5_c_re,
           s5_c_im, s5_d, s5_glu_w, s5_glu_b, w_out, norm2_g, ffn_w_up, ffn_conv_w, ffn_conv_b, ffn_w_down,
           norm_f):
    bsz, length, d = x.shape
    depth = ada_w.shape[0]
    ctx_len = ctx.shape[1]
    dk = d // 2 // RET_HEADS // 2
    pairs = d // 4 // (2 * S5_GROUP)

    cc = jnp.concatenate([c_ctx[None], c, jnp.zeros((8 - 1 - bsz, d), F32)], axis=0)
    mods = _modulation(cc, ada_w, ada_b).reshape(depth, 8, N_MOD, 1, d)
    p = dict(norm1_g=norm1_g, norm2_g=norm2_g, hy_conv_w=hy_conv_w, hy_conv_b=hy_conv_b, hy_w1=hy_w1,
             hy_b1=hy_b1, hy_w2=hy_w2, hy_b2=hy_b2, hy_w3=hy_w3, hy_freq=hy_freq, hy_bias=hy_bias,
             s5_d=s5_d, s5_glu_w=s5_glu_w, s5_glu_b=s5_glu_b, ffn_conv_w=ffn_conv_w, ffn_conv_b=ffn_conv_b,
             w_in=w_in, w_out=w_out, ffn_w_up=ffn_w_up, ffn_w_down=ffn_w_down,
             log_gamma=-jnp.exp(ret_decay.astype(F32)),
             s5_mats=[_s5_mats(s5_lam_re[i], s5_lam_im[i], s5_log_step[i], s5_b_re[i], s5_b_im[i],
                               s5_c_re[i], s5_c_im[i]) for i in range(depth)])
    zero_states = (jnp.zeros((bsz, 2, RET_HEADS, 2 * dk, 2 * dk), F32),
                   jnp.zeros((bsz, d // 4 // LANES, 4, (LANES // S5_GROUP) * S5_STATE), F32))
    h_lat, h_ctx = x, ctx
    for i in range(depth):
        last = i == depth - 1
        h_ctx, ctx_states = _mixer(h_ctx, mods[i, 0:1], i, p, zero_states, 1, ctx_len, not last)
        h_lat, _ = _mixer(h_lat, mods[i, 1:1 + bsz], i, p, ctx_states, length // GRID_W, GRID_W, True)
    return _rmsnorm(h_lat.reshape(bsz * length, d), norm_f).reshape(bsz, length, d)
```

```python
import functools
import math

import numpy as np
import jax
import jax.numpy as jnp
from jax import lax
from jax.experimental import pallas as pl
from jax.experimental.pallas import tpu as pltpu

F32 = jnp.float32
BF16 = jnp.bfloat16
HIGHEST = lax.Precision.HIGHEST

EPS = 1e-6
N_MOD = 6
GRID_W = 64
LANES = 128
SUBLANES = 8
VMEM_LIMIT_MB = 56

HY_ORDER = 2
HY_BANDS = 16
HY_FAST_DECAY = 0.3
HY_SLOW_DECAY = 1.5
HY_TARGET = 1e-2
RET_HEADS = 8
RET_CHUNK = 256
S5_GROUP = 16
S5_STATE = 64
S5_CHUNK = 16


def _params(sem, vmem_mb=VMEM_LIMIT_MB):
    return pltpu.CompilerParams(dimension_semantics=sem, vmem_limit_bytes=vmem_mb << 20)


def _const_spec(shape):
    nd = len(shape)
    return pl.BlockSpec(shape, lambda *_: (0,) * nd)


def _mod_kernel(c_ref, w_ref, b_ref, o_ref):
    s = jax.nn.silu(c_ref[...])
    o_ref[0] = jnp.dot(s, w_ref[0], preferred_element_type=F32, precision=HIGHEST) + b_ref[0]


def _modulation(cc, ada_w, ada_b):
    depth, d, n = ada_w.shape
    bn = 1536
    return pl.pallas_call(
        _mod_kernel,
        grid=(depth, n // bn),
        in_specs=[pl.BlockSpec((8, d), lambda i, j: (0, 0)),
                  pl.BlockSpec((1, d, bn), lambda i, j: (i, 0, j)),
                  pl.BlockSpec((1, 1, bn), lambda i, j: (i, 0, j))],
        out_specs=pl.BlockSpec((1, 8, bn), lambda i, j: (i, 0, j)),
        out_shape=jax.ShapeDtypeStruct((depth, 8, n), F32),
        compiler_params=_params(("parallel", "parallel")),
        name="adaln_mod",
    )(cc, ada_w, ada_b.reshape(depth, 1, n))


def _modmm_kernel(h_ref, g_ref, sh_ref, sc_ref, w_ref, o_ref, *rest, chunk_cols):
    xm_ref = rest[-1]

    @pl.when(pl.program_id(1) == 0)
    def _():
        x = h_ref[...]
        y = x * lax.rsqrt(jnp.mean(x * x, axis=-1, keepdims=True) + EPS)
        y = y * g_ref[...]
        xm_ref[...] = (y * (1.0 + sc_ref[0]) + sh_ref[0]).astype(BF16)

    res = jnp.dot(xm_ref[...], w_ref[...].astype(BF16), preferred_element_type=F32)
    o_ref[...] = res.astype(o_ref.dtype)
    if chunk_cols is not None:
        oc_ref, tok_ref = rest[0], rest[1]
        jblk, lo, width = chunk_cols

        @pl.when(pl.program_id(1) == jblk)
        def _():
            nchunk = oc_ref.shape[0]
            for q in range(width // LANES):
                tok_ref[q] = res[:, lo + q * LANES:lo + (q + 1) * LANES]
                for t in range(S5_CHUNK):
                    c0 = (q * S5_CHUNK + t) * LANES
                    oc_ref[:, c0:c0 + LANES] = tok_ref[q, pl.ds(t, nchunk, stride=S5_CHUNK), :].astype(oc_ref.dtype)


def _modmm(h, g, shift, scale, w, layer, rows_per_mod, bm, bn, chunk_cols=None):
    r, d = h.shape
    n = w.shape[2]
    bm = min(bm, r)
    mod_idx = lambda i, j: ((i * bm) // rows_per_mod, 0, 0)
    out_specs = pl.BlockSpec((bm, bn), lambda i, j: (i, j))
    out_shape = jax.ShapeDtypeStruct((r, n), BF16)
    scratch = [pltpu.VMEM((bm, d), BF16)]
    cc = None
    if chunk_cols is not None:
        start, width = chunk_cols
        assert start // bn == (start + width - 1) // bn
        cc = (start // bn, start % bn, width)
        out_specs = [out_specs, pl.BlockSpec((bm // S5_CHUNK, S5_CHUNK * width), lambda i, j: (i, 0))]
        out_shape = [out_shape, jax.ShapeDtypeStruct((r // S5_CHUNK, S5_CHUNK * width), BF16)]
        scratch = [pltpu.VMEM((width // LANES, bm, LANES), F32)] + scratch
    return pl.pallas_call(
        functools.partial(_modmm_kernel, chunk_cols=cc),
        grid=(r // bm, n // bn),
        in_specs=[pl.BlockSpec((bm, d), lambda i, j: (i, 0)),
                  pl.BlockSpec((1, d), lambda i, j: (0, 0)),
                  pl.BlockSpec((1, 1, d), mod_idx),
                  pl.BlockSpec((1, 1, d), mod_idx),
                  pl.BlockSpec((None, d, bn), lambda i, j: (layer, 0, j))],
        out_specs=out_specs,
        out_shape=out_shape,
        scratch_shapes=scratch,
        compiler_params=_params(("parallel", "arbitrary")),
        name="modmm",
    )(h, g.reshape(1, d), shift, scale, w)


def _resmm_kernel(*refs, n_in):
    x_refs = refs[:n_in]
    w_ref, h_ref, gate_ref, o_ref = refs[n_in:]
    off = 0
    acc = None
    for x_ref in x_refs:
        k = x_ref.shape[1]
        part = jnp.dot(x_ref[...].astype(BF16), w_ref[off:off + k, :].astype(BF16), preferred_element_type=F32)
        acc = part if acc is None else acc + part
        off += k
    o_ref[...] = h_ref[...] + gate_ref[0] * acc


def _resmm(xs, w, layer, h, gate, rows_per_mod, bm, bn):
    r, n = h.shape
    bm = min(bm, r)
    k = w.shape[1]
    w_mode = pl.Buffered(1) if bn == n else None
    in_specs = [pl.BlockSpec((bm, x.shape[1]), lambda i, j: (i, 0)) for x in xs]
    in_specs += [pl.BlockSpec((None, k, bn), lambda i, j: (layer, 0, j), pipeline_mode=w_mode),
                 pl.BlockSpec((bm, bn), lambda i, j: (i, j)),
                 pl.BlockSpec((1, 1, bn), lambda i, j: ((i * bm) // rows_per_mod, 0, j))]
    return pl.pallas_call(
        functools.partial(_resmm_kernel, n_in=len(xs)),
        grid=(r // bm, n // bn),
        in_specs=in_specs,
        out_specs=pl.BlockSpec((bm, bn), lambda i, j: (i, j)),
        out_shape=jax.ShapeDtypeStruct((r, n), F32),
        compiler_params=_params(("parallel", "arbitrary")),
        name="resmm",
    )(*xs, w, h, gate)


def _shortconv_kernel(x_ref, w_ref, b_ref, o_ref, pad_ref, *, length, chunk):
    cb = x_ref.shape[-1]
    zeros = jnp.zeros((8, cb), F32)
    pad_ref[0:8, :] = zeros
    pad_ref[length + 8:length + 16, :] = zeros
    pad_ref[8:length + 8, :] = x_ref[0].astype(F32)
    w0, w1, w2 = w_ref[0:1, :], w_ref[1:2, :], w_ref[2:3, :]
    b = b_ref[...]
    for c in range(length // chunk):
        r = c * chunk
        o_ref[0, r:r + chunk, :] = (pad_ref[r + 7:r + 7 + chunk, :] * w0 + pad_ref[r + 8:r + 8 + chunk, :] * w1
                                    + pad_ref[r + 9:r + 9 + chunk, :] * w2 + b)


def _shortconv(a, w, b, width):
    bsz, length, _ = a.shape
    cb = 256
    chunk = min(512, length)
    return pl.pallas_call(
        functools.partial(_shortconv_kernel, length=length, chunk=chunk),
        grid=(bsz, width // cb),
        in_specs=[pl.BlockSpec((1, length, cb), lambda i, j: (i, 0, j)),
                  pl.BlockSpec((3, cb), lambda i, j: (0, j)),
                  pl.BlockSpec((1, cb), lambda i, j: (0, j))],
        out_specs=pl.BlockSpec((1, length, cb), lambda i, j: (i, 0, j)),
        out_shape=jax.ShapeDtypeStruct((bsz, length, width), F32),
        scratch_shapes=[pltpu.VMEM((length + 16, cb), F32)],
        compiler_params=_params(("parallel", "parallel")),
        name="hy_shortconv",
    )(a, w, b.reshape(1, width))


def _taps_kernel(z_ref, w1_ref, b1_ref, w2_ref, b2_ref, w3_ref, f_ref, dl_ref, o_ref, *, length):
    z = z_ref[...]
    h = jnp.sin(f_ref[0:1, :] * (jnp.dot(z, w1_ref[...], preferred_element_type=F32, precision=HIGHEST)
                                  + b1_ref[...]))
    h = jnp.sin(f_ref[1:2, :] * (jnp.dot(h, w2_ref[...], preferred_element_type=F32, precision=HIGHEST)
                                  + b2_ref[...]))
    h = jnp.dot(h, w3_ref[...], preferred_element_type=F32, precision=HIGHEST)
    t = z[:, 0:1]
    h = h * jnp.exp(-t * jnp.abs(dl_ref[...]))
    rb = z.shape[0]
    row = lax.broadcasted_iota(jnp.int32, h.shape, 0) + pl.program_id(0) * rb
    o_ref[...] = jnp.where(row == length, 0.0, h)


def _hyena_taps(length, w1, b1, w2, b2, w3, freq, width):
    n = 2 * length
    hid = w1.shape[1]
    pos = jnp.arange(n)
    idx = jnp.where(pos < length, pos, n - pos).astype(F32)
    t = (idx / max(length - 1, 1))[:, None]
    bands = jnp.linspace(1e-4, HY_BANDS - 1, HY_BANDS, dtype=F32)
    ang = (2.0 * math.pi * idx / length)[:, None] * bands[None]
    emb = 1 + 2 * HY_BANDS
    z = jnp.concatenate([t, jnp.cos(ang), -jnp.sin(ang), jnp.zeros((n, 64 - emb), F32)], axis=-1)
    w1p = jnp.concatenate([w1.astype(F32), jnp.zeros((64 - emb, hid), F32)], axis=0)
    max_decay = math.log(HY_TARGET) / HY_FAST_DECAY
    min_decay = math.log(HY_TARGET) / HY_SLOW_DECAY
    deltas = jnp.tile(jnp.linspace(min_decay, max_decay, width, dtype=F32), HY_ORDER)[None]
    oc = HY_ORDER * width
    rb = min(1024, length)
    half = length // rb
    return pl.pallas_call(
        functools.partial(_taps_kernel, length=length),
        grid=(n // rb,),
        in_specs=[pl.BlockSpec((rb, 64), lambda i: (i, 0)),
                  _const_spec((64, hid)), _const_spec((1, hid)),
                  _const_spec((hid, hid)), _const_spec((1, hid)),
                  pl.BlockSpec((hid, oc), lambda i: (0, i // half)),
                  _const_spec((2, hid)), _const_spec((1, oc))],
        out_specs=pl.BlockSpec((rb, oc), lambda i: (i, 0)),
        out_shape=jax.ShapeDtypeStruct((n, oc), F32),
        compiler_params=_params(("parallel",)),
        name="hy_taps",
    )(z, w1p, b1.reshape(1, hid).astype(F32), w2.astype(F32), b2.reshape(1, hid).astype(F32),
      w3.astype(F32), freq.astype(F32), deltas)


def _bf16(x):
    return jnp.asarray(x).astype(BF16)


@functools.lru_cache(maxsize=None)
def _dft2_consts(length):
    n = 2 * length
    n2 = LANES
    n1 = n // n2
    hf = n1 // 2
    j = np.arange(n2)[:, None, None]
    k1 = np.arange(n1)[None, :, None]
    m1 = np.arange(n1)[None, None, :]
    ph = -2.0 * np.pi * (j * k1 / n + (m1 * k1 % n1) / n1)
    mr, mi = np.cos(ph), np.sin(ph)
    g1 = np.concatenate([np.concatenate([mr[:, :, :hf], -mi[:, :, :hf]], 2),
                         np.concatenate([mi[:, :, :hf], mr[:, :, :hf]], 2)], 1)
    g1f = np.concatenate([mr, mi], 1)
    mrt = np.transpose(mr, (0, 2, 1))[:, :hf] / n
    mit = -np.transpose(mi, (0, 2, 1))[:, :hf] / n
    g1i = np.concatenate([np.concatenate([mrt, -mit], 2),
                          np.concatenate([mit, mrt], 2)], 1)
    a = np.arange(n2)
    ph2 = -2.0 * np.pi * ((a[:, None] * a[None, :]) % n2) / n2
    fr, fi = np.cos(ph2), np.sin(ph2)
    g2 = np.block([[fr, -fi], [fi, fr]])
    g2i = np.block([[fr, fi], [-fi, fr]])
    f32 = lambda m: np.asarray(m, np.float32)
    return dict(g1=f32(g1), g1i=f32(g1i), g2=f32(g2), g2i=f32(g2i), g1f=f32(g1f))


@functools.lru_cache(maxsize=None)
def _dft1_consts(length):
    n = 2 * length
    a = np.arange(n)
    ph = -2.0 * np.pi * ((a[:, None] * a[None, :]) % n) / n
    fr, fi = np.cos(ph), np.sin(ph)
    gf = np.block([[fr[:, :length], -fi[:, :length]], [fi[:, :length], fr[:, :length]]])
    gi = np.block([[fr[:length], fi[:length]], [-fi[:length], fr[:length]]]) / n
    gff = np.concatenate([fr, fi], 0)
    f32 = lambda m: np.asarray(m, np.float32)
    return dict(gf=f32(gf), gi=f32(gi), gff=f32(gff))


def _fspec2_kernel(k_ref, g1_ref, g2_ref, kr_ref, ki_ref, *, n1):
    inv = 1.0 / jnp.sum(jnp.abs(k_ref[...]), axis=0, keepdims=True)

    def s1(j, c):
        x = k_ref[pl.ds(j, n1, stride=LANES), :].astype(BF16)
        a = jnp.dot(g1_ref[j], x, preferred_element_type=F32)
        kr_ref[pl.ds(j, n1, stride=LANES), :] = a[:n1]
        ki_ref[pl.ds(j, n1, stride=LANES), :] = a[n1:]
        return c

    lax.fori_loop(0, LANES, s1, 0, unroll=8)

    def s2(k1, c):
        r0 = pl.multiple_of(k1 * LANES, LANES)
        a = jnp.concatenate([kr_ref[pl.ds(r0, LANES), :], ki_ref[pl.ds(r0, LANES), :]], axis=0).astype(BF16)
        x = jnp.dot(g2_ref[...], a, preferred_element_type=F32) * inv
        kr_ref[pl.ds(r0, LANES), :] = x[:LANES]
        ki_ref[pl.ds(r0, LANES), :] = x[LANES:]
        return c

    lax.fori_loop(0, n1, s2, 0, unroll=4)


def _filter_spectrum2(taps):
    n, oc = taps.shape
    cst = _dft2_consts(n // 2)
    n1 = n // LANES
    cb = LANES
    g1, g2 = _bf16(cst["g1f"]), _bf16(cst["g2"])
    out = jax.ShapeDtypeStruct((n, oc), F32)
    return pl.pallas_call(
        functools.partial(_fspec2_kernel, n1=n1),
        grid=(oc // cb,),
        in_specs=[pl.BlockSpec((n, cb), lambda i: (0, i)), _const_spec(g1.shape), _const_spec(g2.shape)],
        out_specs=[pl.BlockSpec((n, cb), lambda i: (0, i))] * 2,
        out_shape=[out, out],
        compiler_params=_params(("parallel",)),
        name="hy_fspec2",
    )(taps, g1, g2)


def _fspec1_kernel(k_ref, g_ref, kr_ref, ki_ref):
    k = k_ref[...]
    n = k.shape[0]
    inv = 1.0 / jnp.sum(jnp.abs(k), axis=0, keepdims=True)
    x = jnp.dot(g_ref[...], k.astype(BF16), preferred_element_type=F32) * inv
    kr_ref[...] = x[:n]
    ki_ref[...] = x[n:]


def _filter_spectrum1(taps):
    n, oc = taps.shape
    g = _bf16(_dft1_consts(n // 2)["gff"])
    cb = 256
    out = jax.ShapeDtypeStruct((n, oc), F32)
    return pl.pallas_call(
        _fspec1_kernel,
        grid=(oc // cb,),
        in_specs=[pl.BlockSpec((n, cb), lambda i: (0, i)), _const_spec(g.shape)],
        out_specs=[pl.BlockSpec((n, cb), lambda i: (0, i))] * 2,
        out_shape=[out, out],
        compiler_params=_params(("parallel",)),
        name="hy_fspec1",
    )(taps, g)


def _hyconv2_kernel(v_ref, x_ref, kr_ref, ki_ref, bias_ref, g1_ref, g1i_ref, g2_ref, g2i_ref, o_ref,
                    ar_ref, ai_ref, *, n1):
    hf = n1 // 2

    def s1(j, c):
        xa = v_ref[0, pl.ds(j, hf, stride=LANES), :]
        xb = v_ref[1, pl.ds(j, hf, stride=LANES), :]
        x = jnp.concatenate([xa, xb], axis=0).astype(BF16)
        a = jnp.dot(g1_ref[j], x, preferred_element_type=F32)
        ar_ref[pl.ds(j, n1, stride=LANES), :] = a[:n1]
        ai_ref[pl.ds(j, n1, stride=LANES), :] = a[n1:]
        return c

    lax.fori_loop(0, LANES, s1, 0, unroll=8)

    def s2(k1, c):
        r0 = pl.multiple_of(k1 * LANES, LANES)
        a = jnp.concatenate([ar_ref[pl.ds(r0, LANES), :], ai_ref[pl.ds(r0, LANES), :]], axis=0).astype(BF16)
        x = jnp.dot(g2_ref[...], a, preferred_element_type=F32)
        xr, xi = x[:LANES], x[LANES:]
        kr = kr_ref[pl.ds(r0, LANES), :]
        ki = ki_ref[pl.ds(r0, LANES), :]
        y = jnp.concatenate([xr * kr - xi * ki, xr * ki + xi * kr], axis=0).astype(BF16)
        b = jnp.dot(g2i_ref[...], y, preferred_element_type=F32)
        ar_ref[pl.ds(r0, LANES), :] = b[:LANES]
        ai_ref[pl.ds(r0, LANES), :] = b[LANES:]
        return c

    lax.fori_loop(0, n1, s2, 0, unroll=4)

    def s3(j, c):
        b = jnp.concatenate([ar_ref[pl.ds(j, n1, stride=LANES), :], ai_ref[pl.ds(j, n1, stride=LANES), :]],
                            axis=0).astype(BF16)
        y = jnp.dot(g1i_ref[j], b, preferred_element_type=F32)
        o_ref[0, pl.ds(j, hf, stride=LANES), :] = y[:hf]
        o_ref[1, pl.ds(j, hf, stride=LANES), :] = y[hf:]
        return c

    lax.fori_loop(0, LANES, s3, 0, unroll=8)
    bias = bias_ref[...]
    for b in range(2):
        o_ref[b] = x_ref[b] * (o_ref[b] + bias * v_ref[b])


def _hyconv2(va, v_col, xa, x_col, kr, ki, k_col, bias, length):
    bsz = va.shape[0]
    width = bias.shape[-1]
    cst = _dft2_consts(length)
    n = 2 * length
    n1 = n // LANES
    cb = LANES
    ncb = width // cb
    return pl.pallas_call(
        functools.partial(_hyconv2_kernel, n1=n1),
        grid=(ncb, bsz // 2),
        in_specs=[pl.BlockSpec((2, length, cb), lambda c, q: (q, 0, v_col + c)),
                  pl.BlockSpec((2, length, cb), lambda c, q: (q, 0, x_col + c)),
                  pl.BlockSpec((n, cb), lambda c, q: (0, k_col + c), pipeline_mode=pl.Buffered(1)),
                  pl.BlockSpec((n, cb), lambda c, q: (0, k_col + c), pipeline_mode=pl.Buffered(1)),
                  pl.BlockSpec((1, cb), lambda c, q: (0, c)),
                  _const_spec(cst["g1"].shape), _const_spec(cst["g1i"].shape),
                  _const_spec(cst["g2"].shape), _const_spec(cst["g2i"].shape)],
        out_specs=pl.BlockSpec((2, length, cb), lambda c, q: (q, 0, c)),
        out_shape=jax.ShapeDtypeStruct((bsz, length, width), F32),
        scratch_shapes=[pltpu.VMEM((n, cb), F32), pltpu.VMEM((n, cb), F32)],
        compiler_params=_params(("parallel", "arbitrary")),
        name="hy_conv2",
    )(va, xa, kr, ki, bias.reshape(1, width), _bf16(cst["g1"]), _bf16(cst["g1i"]), _bf16(cst["g2"]),
      _bf16(cst["g2i"]))


def _hyconv1_kernel(v_ref, x_ref, kr_ref, ki_ref, bias_ref, gf_ref, gi_ref, o_ref):
    length = v_ref.shape[1]
    n = 2 * length
    x = jnp.concatenate([v_ref[0], v_ref[1]], axis=0).astype(BF16)
    s = jnp.dot(gf_ref[...], x, preferred_element_type=F32)
    sr, si = s[:n], s[n:]
    kr, ki = kr_ref[...], ki_ref[...]
    y = jnp.concatenate([sr * kr - si * ki, sr * ki + si * kr], axis=0).astype(BF16)
    out = jnp.dot(gi_ref[...], y, preferred_element_type=F32)
    bias = bias_ref[...]
    for b in range(2):
        o_ref[b] = x_ref[b] * (out[b * length:(b + 1) * length] + bias * v_ref[b])


def _hyconv1(va, v_col, xa, x_col, kr, ki, k_col, bias, length):
    bsz = va.shape[0]
    width = bias.shape[-1]
    cst = _dft1_consts(length)
    n = 2 * length
    cb = LANES
    return pl.pallas_call(
        _hyconv1_kernel,
        grid=(width // cb, bsz // 2),
        in_specs=[pl.BlockSpec((2, length, cb), lambda c, q: (q, 0, v_col + c)),
                  pl.BlockSpec((2, length, cb), lambda c, q: (q, 0, x_col + c)),
                  pl.BlockSpec((n, cb), lambda c, q: (0, k_col + c)),
                  pl.BlockSpec((n, cb), lambda c, q: (0, k_col + c)),
                  pl.BlockSpec((1, cb), lambda c, q: (0, c)),
                  _const_spec(cst["gf"].shape), _const_spec(cst["gi"].shape)],
        out_specs=pl.BlockSpec((2, length, cb), lambda c, q: (q, 0, c)),
        out_shape=jax.ShapeDtypeStruct((bsz, length, width), F32),
        compiler_params=_params(("parallel", "arbitrary")),
        name="hy_conv1",
    )(va, xa, kr, ki, bias.reshape(1, width), _bf16(cst["gf"]), _bf16(cst["gi"]))


def _hyena(a, conv_w, conv_b, w1, b1, w2, b2, w3, freq, bias, width):
    bsz, length, _ = a.shape
    p = _shortconv(a, conv_w, conv_b, 3 * width)
    taps = _hyena_taps(length, w1, b1, w2, b2, w3, freq, width)
    two_stage = (2 * length) % (LANES * 16) == 0
    kr, ki = (_filter_spectrum2 if two_stage else _filter_spectrum1)(taps)
    conv = _hyconv2 if two_stage else _hyconv1
    ncb = width // LANES
    z = conv(p, 0, p, ncb, kr, ki, 0, bias[0], length)
    return conv(z, 0, p, 2 * ncb, kr, ki, ncb, bias[1], length)


def _ret_kernel(lg_ref, q_ref, k_ref, v_ref, g_ref, s0_ref, o_ref, sfin_ref, sb_ref, *, length, chunk, dk):
    hp = pl.program_id(1)
    nc = length // chunk
    dv = LANES
    row = lax.broadcasted_iota(jnp.int32, (chunk, chunk), 0)
    col = lax.broadcasted_iota(jnp.int32, (chunk, chunk), 1)
    diff = (row - col).astype(F32)
    lane = lax.broadcasted_iota(jnp.int32, (chunk, 2 * dk), 1)
    pos = lax.broadcasted_iota(jnp.int32, (chunk, 2 * dk), 0).astype(F32)
    ones_s = jnp.ones((2 * dk, dv), F32)
    kscale = dk ** -0.5
    tn = (((0,), (0,)), ((), ()))
    nt = (((1,), (1,)), ((), ()))
    hd = []
    for hh in range(2):
        lgf = lg_ref[0, hp * 2 + hh]
        lgb = lg_ref[1, hp * 2 + hh]
        hd.append(dict(
            decay=jnp.where(diff >= 0.0, jnp.exp(jnp.maximum(diff, 0.0) * lgf),
                            jnp.exp(jnp.maximum(-diff, 0.0) * lgb)),
            qmask=(lane >= dk * hh) & (lane < dk * (hh + 1)),
            qf=jnp.exp((pos + 1.0) * lgf), qb=jnp.exp((chunk - pos) * lgb),
            kf=jnp.exp((chunk - 1.0 - pos) * lgf) * kscale, kb=jnp.exp(pos * lgb) * kscale,
            cdf=jnp.exp(ones_s * (chunk * lgf)), cdb=jnp.exp(ones_s * (chunk * lgb)),
            vs=slice(dv * hh, dv * (hh + 1))))

    def bstep(i, states):
        n = nc - 1 - i
        r0 = pl.multiple_of(n * chunk, chunk)
        k = k_ref[0, pl.ds(r0, chunk), :].astype(F32)
        out = []
        for hh, c in enumerate(hd):
            sb_ref[hh, n] = states[hh]
            v = v_ref[0, pl.ds(r0, chunk), c['vs']]
            inc = lax.dot_general((k * c['kb']).astype(BF16), v, tn, preferred_element_type=F32)
            out.append(c['cdb'] * states[hh] + inc)
        return tuple(out)

    fin = lax.fori_loop(0, nc, bstep, (s0_ref[0, 1, 0], s0_ref[0, 1, 1]))
    sfin_ref[0, 1, 0] = fin[0]
    sfin_ref[0, 1, 1] = fin[1]

    def fstep(n, states):
        r0 = pl.multiple_of(n * chunk, chunk)
        qa = q_ref[0, pl.ds(r0, chunk), :].astype(F32)
        k = k_ref[0, pl.ds(r0, chunk), :].astype(F32)
        ks = (k * kscale).astype(BF16)
        out = []
        for hh, c in enumerate(hd):
            q = jnp.where(c['qmask'], qa, 0.0)
            v = v_ref[0, pl.ds(r0, chunk), c['vs']]
            s = lax.dot_general(q.astype(BF16), ks, nt, preferred_element_type=F32)
            y = jnp.dot((s * c['decay']).astype(BF16), v, preferred_element_type=F32)
            y = y + jnp.dot((q * c['qf']).astype(BF16), states[hh].astype(BF16), preferred_element_type=F32)
            y = y + jnp.dot((q * c['qb']).astype(BF16), sb_ref[hh, n].astype(BF16), preferred_element_type=F32)
            y = y * lax.rsqrt(jnp.mean(y * y, axis=-1, keepdims=True) + EPS)
            g = g_ref[0, pl.ds(r0, chunk), c['vs']].astype(F32)
            o_ref[0, pl.ds(r0, chunk), c['vs']] = (jax.nn.silu(g) * y).astype(o_ref.dtype)
            inc = lax.dot_general((k * c['kf']).astype(BF16), v, tn, preferred_element_type=F32)
            out.append(c['cdf'] * states[hh] + inc)
        return tuple(out)

    fin = lax.fori_loop(0, nc, fstep, (s0_ref[0, 0, 0], s0_ref[0, 0, 1]), unroll=2 if nc % 2 == 0 else 1)
    sfin_ref[0, 0, 0] = fin[0]
    sfin_ref[0, 0, 1] = fin[1]


def _retention(a, log_gamma, s0, q_off, dk):
    bsz, length, _ = a.shape
    heads = RET_HEADS
    dv = 2 * dk
    assert dv == LANES
    chunk = min(RET_CHUNK, length)
    qb = q_off // (2 * dk)
    kb = qb + heads // 2
    vb = (q_off + 2 * heads * dk) // (2 * dv)
    gb = vb + heads // 2
    seq = lambda blk, off: pl.BlockSpec((1, length, blk), lambda b, h, lg: (b, 0, off + h))
    st = pl.BlockSpec((1, 2, 2, 2 * dk, dv), lambda b, h, lg: (b, 0, h, 0, 0))
    grid_spec = pltpu.PrefetchScalarGridSpec(
        num_scalar_prefetch=1,
        grid=(bsz, heads // 2),
        in_specs=[seq(2 * dk, qb), seq(2 * dk, kb), seq(2 * dv, vb), seq(2 * dv, gb), st],
        out_specs=[pl.BlockSpec((1, length, 2 * dv), lambda b, h, lg: (b, 0, h)), st],
        scratch_shapes=[pltpu.VMEM((2, length // chunk, 2 * dk, dv), F32)],
    )
    return pl.pallas_call(
        functools.partial(_ret_kernel, length=length, chunk=chunk, dk=dk),
        grid_spec=grid_spec,
        out_shape=[jax.ShapeDtypeStruct((bsz, length, heads * dv), BF16),
                   jax.ShapeDtypeStruct(s0.shape, F32)],
        compiler_params=_params(("parallel", "parallel")),
        name="retention",
    )(log_gamma, a, a, a, a, s0)


@functools.lru_cache(maxsize=None)
def _s5_expanders():
    gl = LANES // S5_GROUP
    t_len = S5_CHUNK
    ex_to = np.zeros((t_len * S5_GROUP, t_len * LANES), np.float32)
    for t in range(t_len):
        for h in range(gl):
            for o in range(S5_GROUP):
                ex_to[t * S5_GROUP + o, t * LANES + h * S5_GROUP + o] = 1.0
    ex_p = np.kron(np.eye(4, dtype=np.float32), np.tile(np.eye(S5_STATE, dtype=np.float32), (1, gl)))
    g_sgi = (np.arange(t_len * LANES) // S5_GROUP) % gl
    g_tho = (np.arange(t_len * LANES) // S5_GROUP) % gl
    g_hp = np.arange(gl * S5_STATE) // S5_STATE
    eq = lambda a, b: (a[:, None] == b[None, :]).astype(np.float32)
    return ex_to, ex_p, eq(g_sgi, g_tho), eq(g_sgi, np.tile(g_hp, 4)), eq(g_hp, g_tho)


def _s5_mats(lam_re, lam_im, log_step, b_re, b_im, c_re, c_im):
    t_len = S5_CHUNK
    lr = jnp.minimum(lam_re.astype(F32), -1e-4)
    li = lam_im.astype(F32)
    step = jnp.exp(log_step.astype(F32))[..., None]
    dr, di = lr * step, li * step
    d = jnp.arange(t_len + 1, dtype=F32)[:, None, None, None]
    mag = jnp.exp(d * dr)
    pr, pi = mag * jnp.cos(d * di), mag * jnp.sin(d * di)
    nr, ni = pr[1] - 1.0, pi[1]
    den = lr * lr + li * li
    cr, ci = (nr * lr + ni * li) / den, (ni * lr - nr * li) / den
    bbr = cr[..., None] * b_re - ci[..., None] * b_im
    bbi = cr[..., None] * b_im + ci[..., None] * b_re
    ctr = jnp.swapaxes(c_re.astype(F32), 2, 3)
    cti = jnp.swapaxes(c_im.astype(F32), 2, 3)
    groups = lr.shape[1]
    gl = LANES // S5_GROUP
    nq = groups // gl
    tw = t_len * S5_GROUP
    tq = lambda e: e.reshape((nq, gl) + e.shape[1:])
    ex_to, ex_p, m_rc, m_rp, m_pc = _s5_expanders()

    def spread(compact, expand, mask):
        return jnp.matmul(compact.astype(BF16), jnp.asarray(expand, BF16),
                          preferred_element_type=BF16) * jnp.asarray(mask, BF16)

    mr = (bbr[..., :, None] * ctr[..., None, :] - bbi[..., :, None] * cti[..., None, :]).reshape(
        2, groups, S5_STATE, S5_GROUP * S5_GROUP)
    mi = (bbr[..., :, None] * cti[..., None, :] + bbi[..., :, None] * ctr[..., None, :]).reshape(
        2, groups, S5_STATE, S5_GROUP * S5_GROUP)
    kern = (jnp.einsum('dxgp,xgpn->xgdn', pr, mr, precision=HIGHEST)
            - jnp.einsum('dxgp,xgpn->xgdn', pi, mi, precision=HIGHEST))
    kern = jnp.transpose(kern.reshape(2, groups, t_len + 1, S5_GROUP, S5_GROUP), (0, 1, 3, 2, 4))
    kf = kern[0, :, :, :t_len].reshape(groups, S5_GROUP, tw)
    kb = kern[1, :, :, :t_len][:, :, ::-1].reshape(groups, S5_GROUP, tw)
    zpad = jnp.zeros((groups, S5_GROUP, tw - S5_GROUP), F32)
    kf = jnp.concatenate([zpad, kf], axis=-1)
    kb = jnp.concatenate([kb, zpad], axis=-1)
    toe = jnp.stack([kf[..., (t_len - 1 - s) * S5_GROUP:(t_len - 1 - s) * S5_GROUP + tw]
                     + kb[..., (t_len - 1 - s) * S5_GROUP:(t_len - 1 - s) * S5_GROUP + tw]
                     for s in range(t_len)], axis=1)
    rows_sgi = lambda e: jnp.transpose(tq(e), (0, 2, 1, 3, 4)).reshape(nq, t_len * LANES, e.shape[-1])
    w1t = spread(rows_sgi(toe), ex_to, m_rc)
    btr, bti = jnp.swapaxes(bbr, 2, 3), jnp.swapaxes(bbi, 2, 3)

    def e_part(x, order):
        pw_r = jnp.transpose(pr[order, x], (1, 0, 2))[:, :, None, :]
        pw_i = jnp.transpose(pi[order, x], (1, 0, 2))[:, :, None, :]
        br, bi = btr[x][:, None], bti[x][:, None]
        return [rows_sgi(e) for e in (pw_r * br - pw_i * bi, pw_r * bi + pw_i * br)]

    prt, pit = jnp.transpose(pr, (1, 2, 3, 0)), jnp.transpose(pi, (1, 2, 3, 0))

    def q_part(x, order):
        pw_r, pw_i = prt[x][..., order][..., None], pit[x][..., order][..., None]
        c_r, c_i = ctr[x][:, :, None, :], cti[x][:, :, None, :]
        return [spread(tq(e).reshape(nq, gl * S5_STATE, tw), ex_to, m_pc)
                for e in (c_r * pw_r - c_i * pw_i, -(c_r * pw_i + c_i * pw_r))]

    ecat = jnp.concatenate(e_part(0, jnp.arange(t_len - 1, -1, -1)) + e_part(1, jnp.arange(t_len)), axis=2)
    w1e = spread(ecat, ex_p, m_rp)
    w2 = jnp.concatenate(q_part(0, jnp.arange(1, t_len + 1)) + q_part(1, jnp.arange(t_len, 0, -1)), axis=1)
    pl2 = lambda e: e.reshape(nq, gl * S5_STATE)
    lam_t = jnp.stack([pl2(pr[t_len, 0]), pl2(pi[t_len, 0]), pl2(pr[t_len, 1]), pl2(pi[t_len, 1])], axis=1)
    return w1t.astype(BF16), w1e.astype(BF16), w2.astype(BF16), lam_t


def _s5_kernel(u_ref, w1t_ref, w1e_ref, w2_ref, lam_ref, s0_ref, y_ref, sfin_ref, r_ref, *, nc):
    sw = (LANES // S5_GROUP) * S5_STATE
    yw = S5_CHUNK * LANES
    nb = u_ref.shape[0]
    u = u_ref[...].reshape(nb * nc, yw)
    r_ref[:, 0:yw] = jnp.dot(u, w1t_ref[0], preferred_element_type=F32)
    r_ref[:, yw:yw + 4 * sw] = jnp.dot(u, w1e_ref[0], preferred_element_type=F32)
    lfr, lfi, lbr, lbi = (lam_ref[0, i:i + 1, :] for i in range(4))
    cols = [slice(yw + i * sw, yw + (i + 1) * sw) for i in range(4)]
    rid = lax.broadcasted_iota(jnp.int32, (SUBLANES, sw), 0)
    nblk = nc // SUBLANES

    def block(kb, carry):
        out = []
        for b in range(nb):
            fr, fi, br, bi = carry[4 * b:4 * b + 4]
            rf = pl.ds(pl.multiple_of(b * nc + kb * SUBLANES, SUBLANES), SUBLANES)
            rb = pl.ds(pl.multiple_of(b * nc + (nblk - 1 - kb) * SUBLANES, SUBLANES), SUBLANES)
            efr, efi = r_ref[rf, cols[0]], r_ref[rf, cols[1]]
            ebr, ebi = r_ref[rb, cols[2]], r_ref[rb, cols[3]]
            xfr, xfi, xbr, xbi = efr, efi, ebr, ebi
            for s in range(SUBLANES):
                xfr = jnp.where(rid == s, fr, xfr)
                xfi = jnp.where(rid == s, fi, xfi)
                fr, fi = (lfr * fr - lfi * fi + efr[s:s + 1], lfr * fi + lfi * fr + efi[s:s + 1])
                z = SUBLANES - 1 - s
                xbr = jnp.where(rid == z, br, xbr)
                xbi = jnp.where(rid == z, bi, xbi)
                br, bi = (lbr * br - lbi * bi + ebr[z:z + 1], lbr * bi + lbi * br + ebi[z:z + 1])
            r_ref[rf, cols[0]] = xfr
            r_ref[rf, cols[1]] = xfi
            r_ref[rb, cols[2]] = xbr
            r_ref[rb, cols[3]] = xbi
            out += [fr, fi, br, bi]
        return tuple(out)

    fin = lax.fori_loop(0, nblk, block, tuple(s0_ref[b, 0, i:i + 1, :] for b in range(nb) for i in range(4)))
    for b in range(nb):
        for i in range(4):
            sfin_ref[b, 0, i:i + 1, :] = fin[4 * b + i]
    xin = r_ref[:, yw:yw + 4 * sw].astype(BF16)
    y = r_ref[:, 0:yw] + jnp.dot(xin, w2_ref[0], preferred_element_type=F32)
    y_ref[...] = y.reshape(nb, nc, yw)


def _s5(uc, w1t, w1e, w2, lam_t, s0):
    bsz, nc, cw = uc.shape
    nq = w1t.shape[0]
    tw = cw // nq
    sw = lam_t.shape[-1]
    nb = max(1, min(bsz, 512 // nc))
    assert bsz % nb == 0
    st = pl.BlockSpec((nb, 1, 4, sw), lambda q, b: (b, q, 0, 0))
    single = pl.Buffered(1)
    return pl.pallas_call(
        functools.partial(_s5_kernel, nc=nc),
        grid=(nq, bsz // nb),
        in_specs=[pl.BlockSpec((nb, nc, tw), lambda q, b: (b, 0, q)),
                  pl.BlockSpec((1,) + w1t.shape[1:], lambda q, b: (q, 0, 0), pipeline_mode=single),
                  pl.BlockSpec((1,) + w1e.shape[1:], lambda q, b: (q, 0, 0), pipeline_mode=single),
                  pl.BlockSpec((1,) + w2.shape[1:], lambda q, b: (q, 0, 0), pipeline_mode=single),
                  pl.BlockSpec((1, 4, sw), lambda q, b: (q, 0, 0)),
                  st],
        out_specs=[pl.BlockSpec((nb, nc, tw), lambda q, b: (b, 0, q)), st],
        out_shape=[jax.ShapeDtypeStruct((bsz, nc, cw), F32), jax.ShapeDtypeStruct(s0.shape, F32)],
        scratch_shapes=[pltpu.VMEM((nb * nc, w1t.shape[2] + w1e.shape[2]), F32)],
        compiler_params=_params(("parallel", "arbitrary")),
        name="s5_scan",
    )(uc, w1t, w1e, w2, lam_t, s0)


def _s5out_kernel(y_ref, u_ref, d_ref, w_ref, b_ref, o_ref, tok_ref):
    nchunk = y_ref.shape[0]
    nq = tok_ref.shape[0]
    for q in range(nq):
        for t in range(S5_CHUNK):
            c0 = (q * S5_CHUNK + t) * LANES
            tok_ref[q, pl.ds(t, nchunk, stride=S5_CHUNK), :] = y_ref[:, c0:c0 + LANES]
    y = jnp.concatenate([tok_ref[q] for q in range(nq)], axis=-1)
    z = jax.nn.gelu(y + d_ref[...] * u_ref[...].astype(F32))
    gate = jnp.dot(z.astype(BF16), w_ref[...], preferred_element_type=F32) + b_ref[...]
    o_ref[...] = (z * jax.nn.sigmoid(gate)).astype(o_ref.dtype)


def _s5_out(yc, a, u_col, d, glu_w, glu_b):
    width = d.shape[-1]
    r = a.shape[0]
    bm = min(1024, r)
    return pl.pallas_call(
        _s5out_kernel,
        grid=(r // bm,),
        in_specs=[pl.BlockSpec((bm // S5_CHUNK, S5_CHUNK * width), lambda i: (i, 0)),
                  pl.BlockSpec((bm, width), lambda i: (i, u_col)),
                  _const_spec((1, width)), _const_spec((width, width)), _const_spec((1, width))],
        out_specs=pl.BlockSpec((bm, width), lambda i: (i, 0)),
        out_shape=jax.ShapeDtypeStruct((r, width), BF16),
        scratch_shapes=[pltpu.VMEM((width // LANES, bm, LANES), F32)],
        compiler_params=_params(("parallel",)),
        name="s5_out",
    )(yc, a, d.reshape(1, width).astype(F32), glu_w.astype(BF16), glu_b.reshape(1, width).astype(F32))


def _ffnact_kernel(*refs, rows, width, chunk):
    halo = rows > 1
    if halo:
        g_ref, ga_ref, gb_ref, v_ref, cw_ref, cb_ref, o_ref, pad_ref = refs
    else:
        g_ref, v_ref, cw_ref, cb_ref, o_ref = refs
    i = pl.program_id(0)
    bm, fk = g_ref.shape

    if halo:
        per_img = (rows * width) // bm
        top = (i % per_img) == 0
        bottom = (i % per_img) == per_img - 1
        pad_ref[0:width, :] = jnp.where(top, 0.0, ga_ref[...].astype(F32))
        pad_ref[width:width + bm, :] = g_ref[...].astype(F32)
        pad_ref[width + bm:2 * width + bm, :] = jnp.where(bottom, 0.0, gb_ref[...].astype(F32))
    col = lax.broadcasted_iota(jnp.int32, (chunk, LANES), 0) % width
    not_first = col != 0
    not_last = col != width - 1
    bias = cb_ref[...]
    wt = [cw_ref[t:t + 1, :] for t in range(9)]

    def body(c, carry):
        r = pl.multiple_of(c * chunk, chunk)
        for lb in range(fk // LANES):
            ls = slice(lb * LANES, (lb + 1) * LANES)
            w9 = [t[:, ls] for t in wt]
            if halo:
                up = pad_ref[pl.ds(r, chunk), ls]
                mid = pad_ref[pl.ds(width + r, chunk), ls]
                dn = pad_ref[pl.ds(2 * width + r, chunk), ls]
                v0, v1, v2 = (up * w9[dw] + mid * w9[3 + dw] + dn * w9[6 + dw] for dw in range(3))
            else:
                mid = g_ref[pl.ds(r, chunk), ls].astype(F32)
                v0, v1, v2 = (mid * w9[3 + dw] for dw in range(3))
            left = jnp.where(not_first, pltpu.roll(v0, 1, 0), 0.0)
            right = jnp.where(not_last, pltpu.roll(v2, chunk - 1, 0), 0.0)
            gate = jax.nn.gelu(v1 + left + right + bias[:, ls])
            o_ref[pl.ds(r, chunk), ls] = (gate * v_ref[pl.ds(r, chunk), ls].astype(F32)).astype(o_ref.dtype)
        return carry

    lax.fori_loop(0, bm // chunk, body, 0, unroll=2 if (bm // chunk) % 2 == 0 else 1)


def _ffn_act(gv, conv_w, conv_b, rows, width):
    r, f2 = gv.shape
    dff = f2 // 2
    bm = min(1024, r)
    fk = 512
    nk = dff // fk
    chunk = width
    assert bm % chunk == 0 and ((rows * width) % bm == 0 if rows > 1 else bm % width == 0)
    in_specs = [pl.BlockSpec((bm, fk), lambda i, k: (i, k))]
    args = [gv]
    scratch = []
    if rows > 1:
        per = bm // width
        last = r // width - 1
        in_specs += [pl.BlockSpec((width, fk), lambda i, k: (jnp.maximum(i * per - 1, 0), k)),
                     pl.BlockSpec((width, fk), lambda i, k: (jnp.minimum((i + 1) * per, last), k))]
        args += [gv, gv]
        scratch.append(pltpu.VMEM((bm + 2 * width, fk), F32))
    in_specs += [pl.BlockSpec((bm, fk), lambda i, k: (i, nk + k)),
                 pl.BlockSpec((9, fk), lambda i, k: (0, k)),
                 pl.BlockSpec((1, fk), lambda i, k: (0, k))]
    args += [gv, conv_w.reshape(9, dff).astype(F32), conv_b.reshape(1, dff).astype(F32)]
    return pl.pallas_call(
        functools.partial(_ffnact_kernel, rows=rows, width=width, chunk=chunk),
        grid=(r // bm, nk),
        in_specs=in_specs,
        out_specs=pl.BlockSpec((bm, fk), lambda i, k: (i, k)),
        out_shape=jax.ShapeDtypeStruct((r, dff), BF16),
        scratch_shapes=scratch,
        compiler_params=_params(("parallel", "parallel")),
        name="ffn_act",
    )(*args)


def _rms_kernel(x_ref, g_ref, o_ref):
    x = x_ref[...]
    o_ref[...] = x * lax.rsqrt(jnp.mean(x * x, axis=-1, keepdims=True) + EPS) * g_ref[...]


def _rmsnorm(x, g):
    r, d = x.shape
    bm = min(1024, r)
    return pl.pallas_call(
        _rms_kernel,
        grid=(r // bm,),
        in_specs=[pl.BlockSpec((bm, d), lambda i: (i, 0)), _const_spec((1, d))],
        out_specs=pl.BlockSpec((bm, d), lambda i: (i, 0)),
        out_shape=jax.ShapeDtypeStruct((r, d), F32),
        compiler_params=_params(("parallel",)),
        name="final_norm",
    )(x, g.reshape(1, d).astype(F32))


def _mixer(h, mods, i, p, states, img_rows, img_width, full):
    bsz, length, d = h.shape
    r = bsz * length
    rpm = r // mods.shape[0]
    width = d // 4
    dk = d // 2 // RET_HEADS // 2
    h2 = h.reshape(r, d)
    q_off = 3 * width
    u_off = q_off + 2 * RET_HEADS * dk + 2 * (d // 2)
    a, uc = _modmm(h2, p['norm1_g'][i], mods[:, 0], mods[:, 1], p['w_in'], i, rpm, 1024, 1024,
                   chunk_cols=(u_off, width))
    a3 = a.reshape(bsz, length, -1)
    ret, ret_fin = _retention(a3, p['log_gamma'][i], states[0], q_off, dk)
    y5, s5_fin = _s5(uc.reshape(bsz, length // S5_CHUNK, -1), *p['s5_mats'][i], states[1])
    if not full:
        return None, (ret_fin, s5_fin)
    hy = _hyena(a3, p['hy_conv_w'][i], p['hy_conv_b'][i], p['hy_w1'][i], p['hy_b1'][i], p['hy_w2'][i],
                p['hy_b2'][i], p['hy_w3'][i], p['hy_freq'][i], p['hy_bias'][i], width)
    s5o = _s5_out(y5.reshape(r // S5_CHUNK, -1), a, u_off // width, p['s5_d'][i], p['s5_glu_w'][i],
                  p['s5_glu_b'][i])
    h2 = _resmm([hy.reshape(r, width), ret.reshape(r, d // 2), s5o], p['w_out'], i, h2, mods[:, 2], rpm, 512, d)
    gv = _modmm(h2, p['norm2_g'][i], mods[:, 3], mods[:, 4], p['ffn_w_up'], i, rpm, 1024, 1024)
    act = _ffn_act(gv, p['ffn_conv_w'][i], p['ffn_conv_b'][i], img_rows, img_width)
    h2 = _resmm([act], p['ffn_w_down'], i, h2, mods[:, 5], rpm, 1024, 512)
    return h2.reshape(bsz, length, d), (ret_fin, s5_fin)


def kernel(x, c, ctx, c_ctx, ada_w, ada_b, norm1_g, w_in, hy_conv_w, hy_conv_b, hy_w1, hy_b1, hy_w2, hy_b2,
           hy_w3, hy_freq, hy_bias, ret_decay, s5_lam_re, s5_lam_im, s5_log_step, s5_b_re, s5_b_im, s5_c_re,
           s5_c_im, s5_d, s5_glu_w, s5_glu_b, w_out, norm2_g, ffn_w_up, ffn_conv_w, ffn_conv_b, ffn_w_down,
           norm_f):
    bsz, length, d = x.shape
    depth = ada_w.shape[0]
    ctx_len = ctx.shape[1]
    dk = d // 2 // RET_HEADS // 2
    pairs = d // 4 // (2 * S5_GROUP)

    cc = jnp.concatenate([c_ctx[None], c, jnp.zeros((8 - 1 - bsz, d), F32)], axis=0)
    mods = _modulation(cc, ada_w, ada_b).reshape(depth, 8, N_MOD, 1, d)
    p = dict(norm1_g=norm1_g, norm2_g=norm2_g, hy_conv_w=hy_conv_w, hy_conv_b=hy_conv_b, hy_w1=hy_w1,
             hy_b1=hy_b1, hy_w2=hy_w2, hy_b2=hy_b2, hy_w3=hy_w3, hy_freq=hy_freq, hy_bias=hy_bias,
             s5_d=s5_d, s5_glu_w=s5_glu_w, s5_glu_b=s5_glu_b, ffn_conv_w=ffn_conv_w, ffn_conv_b=ffn_conv_b,
             w_in=w_in, w_out=w_out, ffn_w_up=ffn_w_up,
             ffn_w_down=ffn_w_down.astype(BF16),
             log_gamma=-jnp.exp(ret_decay.astype(F32)),
             s5_mats=[_s5_mats(s5_lam_re[i], s5_lam_im[i], s5_log_step[i], s5_b_re[i], s5_b_im[i],
                               s5_c_re[i], s5_c_im[i]) for i in range(depth)])
    zero_states = (jnp.zeros((bsz, 2, RET_HEADS, 2 * dk, 2 * dk), F32),
                   jnp.zeros((bsz, d // 4 // LANES, 4, (LANES // S5_GROUP) * S5_STATE), F32))
    h_lat, h_ctx = x, ctx
    for i in range(depth):
        last = i == depth - 1
        h_ctx, ctx_states = _mixer(h_ctx, mods[i, 0:1], i, p, zero_states, 1, ctx_len, not last)
        h_lat, _ = _mixer(h_lat, mods[i, 1:1 + bsz], i, p, ctx_states, length // GRID_W, GRID_W, True)
    return _rmsnorm(h_lat.reshape(bsz * length, d), norm_f).reshape(bsz, length, d)
```

```python
import functools
import math

import numpy as np
import jax
import jax.numpy as jnp
from jax import lax
from jax.experimental import pallas as pl
from jax.experimental.pallas import tpu as pltpu

F32 = jnp.float32
BF16 = jnp.bfloat16
HIGHEST = lax.Precision.HIGHEST

EPS = 1e-6
N_MOD = 6
GRID_W = 64
LANES = 128
SUBLANES = 8
VMEM_LIMIT_MB = 56

HY_ORDER = 2
HY_BANDS = 16
HY_FAST_DECAY = 0.3
HY_SLOW_DECAY = 1.5
HY_TARGET = 1e-2
RET_HEADS = 8
RET_CHUNK = 256
S5_GROUP = 16
S5_STATE = 64
S5_CHUNK = 16


def _params(sem, vmem_mb=VMEM_LIMIT_MB):
    return pltpu.CompilerParams(dimension_semantics=sem, vmem_limit_bytes=vmem_mb << 20)


def _const_spec(shape):
    nd = len(shape)
    return pl.BlockSpec(shape, lambda *_: (0,) * nd)


def _mod_kernel(c_ref, w_ref, b_ref, o_ref):
    s = jax.nn.silu(c_ref[...])
    o_ref[0] = jnp.dot(s, w_ref[0], preferred_element_type=F32, precision=HIGHEST) + b_ref[0]


def _modulation(cc, ada_w, ada_b):
    depth, d, n = ada_w.shape
    bn = 1536
    return pl.pallas_call(
        _mod_kernel,
        grid=(depth, n // bn),
        in_specs=[pl.BlockSpec((8, d), lambda i, j: (0, 0)),
                  pl.BlockSpec((1, d, bn), lambda i, j: (i, 0, j)),
                  pl.BlockSpec((1, 1, bn), lambda i, j: (i, 0, j))],
        out_specs=pl.BlockSpec((1, 8, bn), lambda i, j: (i, 0, j)),
        out_shape=jax.ShapeDtypeStruct((depth, 8, n), F32),
        compiler_params=_params(("parallel", "parallel")),
        name="adaln_mod",
    )(cc, ada_w, ada_b.reshape(depth, 1, n))


def _modmm_kernel(h_ref, g_ref, sh_ref, sc_ref, w_ref, o_ref, *rest, chunk_cols):
    xm_ref = rest[-1]

    @pl.when(pl.program_id(1) == 0)
    def _():
        x = h_ref[...]
        y = x * lax.rsqrt(jnp.mean(x * x, axis=-1, keepdims=True) + EPS)
        y = y * g_ref[...]
        xm_ref[...] = (y * (1.0 + sc_ref[0]) + sh_ref[0]).astype(BF16)

    res = jnp.dot(xm_ref[...], w_ref[...].astype(BF16), preferred_element_type=F32)
    o_ref[...] = res.astype(o_ref.dtype)
    if chunk_cols is not None:
        oc_ref, tok_ref = rest[0], rest[1]
        jblk, lo, width = chunk_cols

        @pl.when(pl.program_id(1) == jblk)
        def _():
            nchunk = oc_ref.shape[0]
            for q in range(width // LANES):
                tok_ref[q] = res[:, lo + q * LANES:lo + (q + 1) * LANES]
                for t in range(S5_CHUNK):
                    c0 = (q * S5_CHUNK + t) * LANES
                    oc_ref[:, c0:c0 + LANES] = tok_ref[q, pl.ds(t, nchunk, stride=S5_CHUNK), :].astype(oc_ref.dtype)


def _modmm(h, g, shift, scale, w, layer, rows_per_mod, bm, bn, chunk_cols=None):
    r, d = h.shape
    n = w.shape[2]
    bm = min(bm, r)
    mod_idx = lambda i, j: ((i * bm) // rows_per_mod, 0, 0)
    out_specs = pl.BlockSpec((bm, bn), lambda i, j: (i, j))
    out_shape = jax.ShapeDtypeStruct((r, n), BF16)
    scratch = [pltpu.VMEM((bm, d), BF16)]
    cc = None
    if chunk_cols is not None:
        start, width = chunk_cols
        assert start // bn == (start + width - 1) // bn
        cc = (start // bn, start % bn, width)
        out_specs = [out_specs, pl.BlockSpec((bm // S5_CHUNK, S5_CHUNK * width), lambda i, j: (i, 0))]
        out_shape = [out_shape, jax.ShapeDtypeStruct((r // S5_CHUNK, S5_CHUNK * width), BF16)]
        scratch = [pltpu.VMEM((width // LANES, bm, LANES), F32)] + scratch
    return pl.pallas_call(
        functools.partial(_modmm_kernel, chunk_cols=cc),
        grid=(r // bm, n // bn),
        in_specs=[pl.BlockSpec((bm, d), lambda i, j: (i, 0)),
                  pl.BlockSpec((1, d), lambda i, j: (0, 0)),
                  pl.BlockSpec((1, 1, d), mod_idx),
                  pl.BlockSpec((1, 1, d), mod_idx),
                  pl.BlockSpec((None, d, bn), lambda i, j: (layer, 0, j))],
        out_specs=out_specs,
        out_shape=out_shape,
        scratch_shapes=scratch,
        compiler_params=_params(("parallel", "arbitrary")),
        name="modmm",
    )(h, g.reshape(1, d), shift, scale, w)


def _resmm_kernel(*refs, n_in):
    x_refs = refs[:n_in]
    w_ref, h_ref, gate_ref, o_ref = refs[n_in:]
    off = 0
    acc = None
    for x_ref in x_refs:
        k = x_ref.shape[1]
        part = jnp.dot(x_ref[...].astype(BF16), w_ref[off:off + k, :].astype(BF16), preferred_element_type=F32)
        acc = part if acc is None else acc + part
        off += k
    o_ref[...] = h_ref[...] + gate_ref[0] * acc


def _resmm(xs, w, layer, h, gate, rows_per_mod, bm, bn):
    r, n = h.shape
    bm = min(bm, r)
    k = w.shape[1]
    w_mode = pl.Buffered(1) if bn == n else None
    in_specs = [pl.BlockSpec((bm, x.shape[1]), lambda i, j: (i, 0)) for x in xs]
    in_specs += [pl.BlockSpec((None, k, bn), lambda i, j: (layer, 0, j), pipeline_mode=w_mode),
                 pl.BlockSpec((bm, bn), lambda i, j: (i, j)),
                 pl.BlockSpec((1, 1, bn), lambda i, j: ((i * bm) // rows_per_mod, 0, j))]
    return pl.pallas_call(
        functools.partial(_resmm_kernel, n_in=len(xs)),
        grid=(r // bm, n // bn),
        in_specs=in_specs,
        out_specs=pl.BlockSpec((bm, bn), lambda i, j: (i, j)),
        out_shape=jax.ShapeDtypeStruct((r, n), F32),
        compiler_params=_params(("parallel", "arbitrary")),
        name="resmm",
    )(*xs, w, h, gate)


def _shortconv_kernel(x_ref, w_ref, b_ref, o_ref, pad_ref, *, length, chunk):
    cb = x_ref.shape[-1]
    zeros = jnp.zeros((8, cb), F32)
    pad_ref[0:8, :] = zeros
    pad_ref[length + 8:length + 16, :] = zeros
    pad_ref[8:length + 8, :] = x_ref[0].astype(F32)
    w0, w1, w2 = w_ref[0:1, :], w_ref[1:2, :], w_ref[2:3, :]
    b = b_ref[...]
    for c in range(length // chunk):
        r = c * chunk
        o_ref[0, r:r + chunk, :] = (pad_ref[r + 7:r + 7 + chunk, :] * w0 + pad_ref[r + 8:r + 8 + chunk, :] * w1
                                    + pad_ref[r + 9:r + 9 + chunk, :] * w2 + b)


def _shortconv(a, w, b, width):
    bsz, length, _ = a.shape
    cb = 256
    chunk = min(512, length)
    return pl.pallas_call(
        functools.partial(_shortconv_kernel, length=length, chunk=chunk),
        grid=(bsz, width // cb),
        in_specs=[pl.BlockSpec((1, length, cb), lambda i, j: (i, 0, j)),
                  pl.BlockSpec((3, cb), lambda i, j: (0, j)),
                  pl.BlockSpec((1, cb), lambda i, j: (0, j))],
        out_specs=pl.BlockSpec((1, length, cb), lambda i, j: (i, 0, j)),
        out_shape=jax.ShapeDtypeStruct((bsz, length, width), F32),
        scratch_shapes=[pltpu.VMEM((length + 16, cb), F32)],
        compiler_params=_params(("parallel", "parallel")),
        name="hy_shortconv",
    )(a, w, b.reshape(1, width))


def _taps_kernel(z_ref, w1_ref, b1_ref, w2_ref, b2_ref, w3_ref, f_ref, dl_ref, o_ref, *, length):
    z = z_ref[...]
    h = jnp.sin(f_ref[0:1, :] * (jnp.dot(z, w1_ref[...], preferred_element_type=F32, precision=HIGHEST)
                                  + b1_ref[...]))
    h = jnp.sin(f_ref[1:2, :] * (jnp.dot(h, w2_ref[...], preferred_element_type=F32, precision=HIGHEST)
                                  + b2_ref[...]))
    w3 = w3_ref[...]
    h_hi, w_hi = h.astype(BF16), w3.astype(BF16)
    h_lo, w_lo = (h - h_hi.astype(F32)).astype(BF16), (w3 - w_hi.astype(F32)).astype(BF16)
    h = (jnp.dot(h_hi, w_hi, preferred_element_type=F32) + jnp.dot(h_lo, w_hi, preferred_element_type=F32)
         + jnp.dot(h_hi, w_lo, preferred_element_type=F32))
    t = z[:, 0:1]
    h = h * jnp.exp(-t * jnp.abs(dl_ref[...]))
    rb = z.shape[0]
    row = lax.broadcasted_iota(jnp.int32, h.shape, 0) + pl.program_id(0) * rb
    o_ref[...] = jnp.where(row == length, 0.0, h)


def _hyena_taps(length, w1, b1, w2, b2, w3, freq, width):
    n = 2 * length
    hid = w1.shape[1]
    pos = jnp.arange(n)
    idx = jnp.where(pos < length, pos, n - pos).astype(F32)
    t = (idx / max(length - 1, 1))[:, None]
    bands = jnp.linspace(1e-4, HY_BANDS - 1, HY_BANDS, dtype=F32)
    ang = (2.0 * math.pi * idx / length)[:, None] * bands[None]
    emb = 1 + 2 * HY_BANDS
    z = jnp.concatenate([t, jnp.cos(ang), -jnp.sin(ang), jnp.zeros((n, 64 - emb), F32)], axis=-1)
    w1p = jnp.concatenate([w1.astype(F32), jnp.zeros((64 - emb, hid), F32)], axis=0)
    max_decay = math.log(HY_TARGET) / HY_FAST_DECAY
    min_decay = math.log(HY_TARGET) / HY_SLOW_DECAY
    deltas = jnp.tile(jnp.linspace(min_decay, max_decay, width, dtype=F32), HY_ORDER)[None]
    oc = HY_ORDER * width
    rb = min(1024, length)
    half = length // rb
    return pl.pallas_call(
        functools.partial(_taps_kernel, length=length),
        grid=(n // rb,),
        in_specs=[pl.BlockSpec((rb, 64), lambda i: (i, 0)),
                  _const_spec((64, hid)), _const_spec((1, hid)),
                  _const_spec((hid, hid)), _const_spec((1, hid)),
                  pl.BlockSpec((hid, oc), lambda i: (0, i // half)),
                  _const_spec((2, hid)), _const_spec((1, oc))],
        out_specs=pl.BlockSpec((rb, oc), lambda i: (i, 0)),
        out_shape=jax.ShapeDtypeStruct((n, oc), F32),
        compiler_params=_params(("parallel",)),
        name="hy_taps",
    )(z, w1p, b1.reshape(1, hid).astype(F32), w2.astype(F32), b2.reshape(1, hid).astype(F32),
      w3.astype(F32), freq.astype(F32), deltas)


def _bf16(x):
    return jnp.asarray(x).astype(BF16)


@functools.lru_cache(maxsize=None)
def _dft2_consts(length):
    n = 2 * length
    n2 = LANES
    n1 = n // n2
    hf = n1 // 2
    j = np.arange(n2)[:, None, None]
    k1 = np.arange(n1)[None, :, None]
    m1 = np.arange(n1)[None, None, :]
    ph = -2.0 * np.pi * (j * k1 / n + (m1 * k1 % n1) / n1)
    mr, mi = np.cos(ph), np.sin(ph)
    g1 = np.concatenate([np.concatenate([mr[:, :, :hf], -mi[:, :, :hf]], 2),
                         np.concatenate([mi[:, :, :hf], mr[:, :, :hf]], 2)], 1)
    g1f = np.concatenate([mr, mi], 1)
    mrt = np.transpose(mr, (0, 2, 1))[:, :hf] / n
    mit = -np.transpose(mi, (0, 2, 1))[:, :hf] / n
    g1i = np.concatenate([np.concatenate([mrt, -mit], 2),
                          np.concatenate([mit, mrt], 2)], 1)
    a = np.arange(n2)
    ph2 = -2.0 * np.pi * ((a[:, None] * a[None, :]) % n2) / n2
    fr, fi = np.cos(ph2), np.sin(ph2)
    g2 = np.block([[fr, -fi], [fi, fr]])
    g2i = np.block([[fr, fi], [-fi, fr]])
    f32 = lambda m: np.asarray(m, np.float32)
    return dict(g1=f32(g1), g1i=f32(g1i), g2=f32(g2), g2i=f32(g2i), g1f=f32(g1f))


@functools.lru_cache(maxsize=None)
def _dft1_consts(length):
    n = 2 * length
    a = np.arange(n)
    ph = -2.0 * np.pi * ((a[:, None] * a[None, :]) % n) / n
    fr, fi = np.cos(ph), np.sin(ph)
    gf = np.block([[fr[:, :length], -fi[:, :length]], [fi[:, :length], fr[:, :length]]])
    gi = np.block([[fr[:length], fi[:length]], [-fi[:length], fr[:length]]]) / n
    gff = np.concatenate([fr, fi], 0)
    f32 = lambda m: np.asarray(m, np.float32)
    return dict(gf=f32(gf), gi=f32(gi), gff=f32(gff))


def _fspec2_kernel(k_ref, g1_ref, g2_ref, kr_ref, ki_ref, *, n1):
    inv = 1.0 / jnp.sum(jnp.abs(k_ref[...]), axis=0, keepdims=True)

    def s1(j, c):
        x = k_ref[pl.ds(j, n1, stride=LANES), :].astype(BF16)
        a = jnp.dot(g1_ref[j], x, preferred_element_type=F32)
        kr_ref[pl.ds(j, n1, stride=LANES), :] = a[:n1]
        ki_ref[pl.ds(j, n1, stride=LANES), :] = a[n1:]
        return c

    lax.fori_loop(0, LANES, s1, 0, unroll=8)

    def s2(k1, c):
        r0 = pl.multiple_of(k1 * LANES, LANES)
        a = jnp.concatenate([kr_ref[pl.ds(r0, LANES), :], ki_ref[pl.ds(r0, LANES), :]], axis=0).astype(BF16)
        x = jnp.dot(g2_ref[...], a, preferred_element_type=F32) * inv
        kr_ref[pl.ds(r0, LANES), :] = x[:LANES]
        ki_ref[pl.ds(r0, LANES), :] = x[LANES:]
        return c

    lax.fori_loop(0, n1, s2, 0, unroll=8)


def _filter_spectrum2(taps):
    n, oc = taps.shape
    cst = _dft2_consts(n // 2)
    n1 = n // LANES
    cb = LANES
    g1, g2 = _bf16(cst["g1f"]), _bf16(cst["g2"])
    out = jax.ShapeDtypeStruct((n, oc), F32)
    return pl.pallas_call(
        functools.partial(_fspec2_kernel, n1=n1),
        grid=(oc // cb,),
        in_specs=[pl.BlockSpec((n, cb), lambda i: (0, i)), _const_spec(g1.shape), _const_spec(g2.shape)],
        out_specs=[pl.BlockSpec((n, cb), lambda i: (0, i))] * 2,
        out_shape=[out, out],
        compiler_params=_params(("parallel",)),
        name="hy_fspec2",
    )(taps, g1, g2)


def _fspec1_kernel(k_ref, g_ref, kr_ref, ki_ref):
    k = k_ref[...]
    n = k.shape[0]
    inv = 1.0 / jnp.sum(jnp.abs(k), axis=0, keepdims=True)
    x = jnp.dot(g_ref[...], k.astype(BF16), preferred_element_type=F32) * inv
    kr_ref[...] = x[:n]
    ki_ref[...] = x[n:]


def _filter_spectrum1(taps):
    n, oc = taps.shape
    g = _bf16(_dft1_consts(n // 2)["gff"])
    cb = 256
    out = jax.ShapeDtypeStruct((n, oc), F32)
    return pl.pallas_call(
        _fspec1_kernel,
        grid=(oc // cb,),
        in_specs=[pl.BlockSpec((n, cb), lambda i: (0, i)), _const_spec(g.shape)],
        out_specs=[pl.BlockSpec((n, cb), lambda i: (0, i))] * 2,
        out_shape=[out, out],
        compiler_params=_params(("parallel",)),
        name="hy_fspec1",
    )(taps, g)


def _hyconv2_kernel(v_ref, x_ref, kr_ref, ki_ref, bias_ref, g1_ref, g1i_ref, g2_ref, g2i_ref, o_ref,
                    ar_ref, ai_ref, *, n1):
    hf = n1 // 2

    def s1(j, c):
        xa = v_ref[0, pl.ds(j, hf, stride=LANES), :]
        xb = v_ref[1, pl.ds(j, hf, stride=LANES), :]
        x = jnp.concatenate([xa, xb], axis=0).astype(BF16)
        a = jnp.dot(g1_ref[j], x, preferred_element_type=F32)
        ar_ref[pl.ds(j, n1, stride=LANES), :] = a[:n1]
        ai_ref[pl.ds(j, n1, stride=LANES), :] = a[n1:]
        return c

    lax.fori_loop(0, LANES, s1, 0, unroll=8)

    def s2(k1, c):
        r0 = pl.multiple_of(k1 * LANES, LANES)
        a = jnp.concatenate([ar_ref[pl.ds(r0, LANES), :], ai_ref[pl.ds(r0, LANES), :]], axis=0).astype(BF16)
        x = jnp.dot(g2_ref[...], a, preferred_element_type=F32)
        xr, xi = x[:LANES], x[LANES:]
        kr = kr_ref[pl.ds(r0, LANES), :]
        ki = ki_ref[pl.ds(r0, LANES), :]
        y = jnp.concatenate([xr * kr - xi * ki, xr * ki + xi * kr], axis=0).astype(BF16)
        b = jnp.dot(g2i_ref[...], y, preferred_element_type=F32)
        ar_ref[pl.ds(r0, LANES), :] = b[:LANES]
        ai_ref[pl.ds(r0, LANES), :] = b[LANES:]
        return c

    lax.fori_loop(0, n1, s2, 0, unroll=8)

    def s3(j, c):
        b = jnp.concatenate([ar_ref[pl.ds(j, n1, stride=LANES), :], ai_ref[pl.ds(j, n1, stride=LANES), :]],
                            axis=0).astype(BF16)
        y = jnp.dot(g1i_ref[j], b, preferred_element_type=F32)
        o_ref[0, pl.ds(j, hf, stride=LANES), :] = y[:hf]
        o_ref[1, pl.ds(j, hf, stride=LANES), :] = y[hf:]
        return c

    lax.fori_loop(0, LANES, s3, 0, unroll=8)
    bias = bias_ref[...]
    for b in range(2):
        o_ref[b] = x_ref[b] * (o_ref[b] + bias * v_ref[b])


def _hyconv2(va, v_col, xa, x_col, kr, ki, k_col, bias, length):
    bsz = va.shape[0]
    width = bias.shape[-1]
    cst = _dft2_consts(length)
    n = 2 * length
    n1 = n // LANES
    cb = LANES
    ncb = width // cb
    return pl.pallas_call(
        functools.partial(_hyconv2_kernel, n1=n1),
        grid=(ncb, bsz // 2),
        in_specs=[pl.BlockSpec((2, length, cb), lambda c, q: (q, 0, v_col + c)),
                  pl.BlockSpec((2, length, cb), lambda c, q: (q, 0, x_col + c)),
                  pl.BlockSpec((n, cb), lambda c, q: (0, k_col + c), pipeline_mode=pl.Buffered(1)),
                  pl.BlockSpec((n, cb), lambda c, q: (0, k_col + c), pipeline_mode=pl.Buffered(1)),
                  pl.BlockSpec((1, cb), lambda c, q: (0, c)),
                  _const_spec(cst["g1"].shape), _const_spec(cst["g1i"].shape),
                  _const_spec(cst["g2"].shape), _const_spec(cst["g2i"].shape)],
        out_specs=pl.BlockSpec((2, length, cb), lambda c, q: (q, 0, c)),
        out_shape=jax.ShapeDtypeStruct((bsz, length, width), F32),
        scratch_shapes=[pltpu.VMEM((n, cb), F32), pltpu.VMEM((n, cb), F32)],
        compiler_params=_params(("parallel", "arbitrary")),
        name="hy_conv2",
    )(va, xa, kr, ki, bias.reshape(1, width), _bf16(cst["g1"]), _bf16(cst["g1i"]), _bf16(cst["g2"]),
      _bf16(cst["g2i"]))


def _hyconv1_kernel(v_ref, x_ref, kr_ref, ki_ref, bias_ref, gf_ref, gi_ref, o_ref):
    length = v_ref.shape[1]
    n = 2 * length
    x = jnp.concatenate([v_ref[0], v_ref[1]], axis=0).astype(BF16)
    s = jnp.dot(gf_ref[...], x, preferred_element_type=F32)
    sr, si = s[:n], s[n:]
    kr, ki = kr_ref[...], ki_ref[...]
    y = jnp.concatenate([sr * kr - si * ki, sr * ki + si * kr], axis=0).astype(BF16)
    out = jnp.dot(gi_ref[...], y, preferred_element_type=F32)
    bias = bias_ref[...]
    for b in range(2):
        o_ref[b] = x_ref[b] * (out[b * length:(b + 1) * length] + bias * v_ref[b])


def _hyconv1(va, v_col, xa, x_col, kr, ki, k_col, bias, length):
    bsz = va.shape[0]
    width = bias.shape[-1]
    cst = _dft1_consts(length)
    n = 2 * length
    cb = LANES
    return pl.pallas_call(
        _hyconv1_kernel,
        grid=(width // cb, bsz // 2),
        in_specs=[pl.BlockSpec((2, length, cb), lambda c, q: (q, 0, v_col + c)),
                  pl.BlockSpec((2, length, cb), lambda c, q: (q, 0, x_col + c)),
                  pl.BlockSpec((n, cb), lambda c, q: (0, k_col + c)),
                  pl.BlockSpec((n, cb), lambda c, q: (0, k_col + c)),
                  pl.BlockSpec((1, cb), lambda c, q: (0, c)),
                  _const_spec(cst["gf"].shape), _const_spec(cst["gi"].shape)],
        out_specs=pl.BlockSpec((2, length, cb), lambda c, q: (q, 0, c)),
        out_shape=jax.ShapeDtypeStruct((bsz, length, width), F32),
        compiler_params=_params(("parallel", "arbitrary")),
        name="hy_conv1",
    )(va, xa, kr, ki, bias.reshape(1, width), _bf16(cst["gf"]), _bf16(cst["gi"]))


def _hyena(a, conv_w, conv_b, w1, b1, w2, b2, w3, freq, bias, width):
    bsz, length, _ = a.shape
    p = _shortconv(a, conv_w, conv_b, 3 * width)
    taps = _hyena_taps(length, w1, b1, w2, b2, w3, freq, width)
    two_stage = (2 * length) % (LANES * 16) == 0
    kr, ki = (_filter_spectrum2 if two_stage else _filter_spectrum1)(taps)
    conv = _hyconv2 if two_stage else _hyconv1
    ncb = width // LANES
    z = conv(p, 0, p, ncb, kr, ki, 0, bias[0], length)
    return conv(z, 0, p, 2 * ncb, kr, ki, ncb, bias[1], length)


def _ret_kernel(lg_ref, q_ref, k_ref, v_ref, g_ref, s0_ref, o_ref, sfin_ref, sb_ref, *, length, chunk, dk):
    hp = pl.program_id(1)
    nc = length // chunk
    dv = LANES
    row = lax.broadcasted_iota(jnp.int32, (chunk, chunk), 0)
    col = lax.broadcasted_iota(jnp.int32, (chunk, chunk), 1)
    diff = (row - col).astype(F32)
    lane = lax.broadcasted_iota(jnp.int32, (chunk, 2 * dk), 1)
    pos = lax.broadcasted_iota(jnp.int32, (chunk, 2 * dk), 0).astype(F32)
    ones_s = jnp.ones((2 * dk, dv), F32)
    kscale = dk ** -0.5
    tn = (((0,), (0,)), ((), ()))
    nt = (((1,), (1,)), ((), ()))
    hd = []
    for hh in range(2):
        lgf = lg_ref[0, hp * 2 + hh]
        lgb = lg_ref[1, hp * 2 + hh]
        hd.append(dict(
            decay=jnp.where(diff >= 0.0, jnp.exp(jnp.maximum(diff, 0.0) * lgf),
                            jnp.exp(jnp.maximum(-diff, 0.0) * lgb)),
            qmask=(lane >= dk * hh) & (lane < dk * (hh + 1)),
            qf=jnp.exp((pos + 1.0) * lgf), qb=jnp.exp((chunk - pos) * lgb),
            kf=jnp.exp((chunk - 1.0 - pos) * lgf) * kscale, kb=jnp.exp(pos * lgb) * kscale,
            cdf=jnp.exp(ones_s * (chunk * lgf)), cdb=jnp.exp(ones_s * (chunk * lgb)),
            vs=slice(dv * hh, dv * (hh + 1))))

    def bstep(i, states):
        n = nc - 1 - i
        r0 = pl.multiple_of(n * chunk, chunk)
        k = k_ref[0, pl.ds(r0, chunk), :].astype(F32)
        out = []
        for hh, c in enumerate(hd):
            sb_ref[hh, n] = states[hh]
            v = v_ref[0, pl.ds(r0, chunk), c['vs']]
            inc = lax.dot_general((k * c['kb']).astype(BF16), v, tn, preferred_element_type=F32)
            out.append(c['cdb'] * states[hh] + inc)
        return tuple(out)

    fin = lax.fori_loop(0, nc, bstep, (s0_ref[0, 1, 0], s0_ref[0, 1, 1]), unroll=4 if nc % 4 == 0 else 1)
    sfin_ref[0, 1, 0] = fin[0]
    sfin_ref[0, 1, 1] = fin[1]

    def fstep(n, states):
        r0 = pl.multiple_of(n * chunk, chunk)
        qa = q_ref[0, pl.ds(r0, chunk), :].astype(F32)
        k = k_ref[0, pl.ds(r0, chunk), :].astype(F32)
        ks = (k * kscale).astype(BF16)
        out = []
        for hh, c in enumerate(hd):
            q = jnp.where(c['qmask'], qa, 0.0)
            v = v_ref[0, pl.ds(r0, chunk), c['vs']]
            s = lax.dot_general(q.astype(BF16), ks, nt, preferred_element_type=F32)
            y = jnp.dot((s * c['decay']).astype(BF16), v, preferred_element_type=F32)
            y = y + jnp.dot((q * c['qf']).astype(BF16), states[hh].astype(BF16), preferred_element_type=F32)
            y = y + jnp.dot((q * c['qb']).astype(BF16), sb_ref[hh, n].astype(BF16), preferred_element_type=F32)
            y = y * lax.rsqrt(jnp.mean(y * y, axis=-1, keepdims=True) + EPS)
            g = g_ref[0, pl.ds(r0, chunk), c['vs']].astype(F32)
            o_ref[0, pl.ds(r0, chunk), c['vs']] = (jax.nn.silu(g) * y).astype(o_ref.dtype)
            inc = lax.dot_general((k * c['kf']).astype(BF16), v, tn, preferred_element_type=F32)
            out.append(c['cdf'] * states[hh] + inc)
        return tuple(out)

    fin = lax.fori_loop(0, nc, fstep, (s0_ref[0, 0, 0], s0_ref[0, 0, 1]), unroll=4 if nc % 4 == 0 else 1)
    sfin_ref[0, 0, 0] = fin[0]
    sfin_ref[0, 0, 1] = fin[1]


def _retention(a, log_gamma, s0, q_off, dk):
    bsz, length, _ = a.shape
    heads = RET_HEADS
    dv = 2 * dk
    assert dv == LANES
    chunk = min(RET_CHUNK, length)
    qb = q_off // (2 * dk)
    kb = qb + heads // 2
    vb = (q_off + 2 * heads * dk) // (2 * dv)
    gb = vb + heads // 2
    seq = lambda blk, off: pl.BlockSpec((1, length, blk), lambda b, h, lg: (b, 0, off + h))
    st = pl.BlockSpec((1, 2, 2, 2 * dk, dv), lambda b, h, lg: (b, 0, h, 0, 0))
    grid_spec = pltpu.PrefetchScalarGridSpec(
        num_scalar_prefetch=1,
        grid=(bsz, heads // 2),
        in_specs=[seq(2 * dk, qb), seq(2 * dk, kb), seq(2 * dv, vb), seq(2 * dv, gb), st],
        out_specs=[pl.BlockSpec((1, length, 2 * dv), lambda b, h, lg: (b, 0, h)), st],
        scratch_shapes=[pltpu.VMEM((2, length // chunk, 2 * dk, dv), F32)],
    )
    return pl.pallas_call(
        functools.partial(_ret_kernel, length=length, chunk=chunk, dk=dk),
        grid_spec=grid_spec,
        out_shape=[jax.ShapeDtypeStruct((bsz, length, heads * dv), BF16),
                   jax.ShapeDtypeStruct(s0.shape, F32)],
        compiler_params=_params(("parallel", "parallel")),
        name="retention",
    )(log_gamma, a, a, a, a, s0)


@functools.lru_cache(maxsize=None)
def _s5_expanders():
    gl = LANES // S5_GROUP
    t_len = S5_CHUNK
    ex_to = np.zeros((t_len * S5_GROUP, t_len * LANES), np.float32)
    for t in range(t_len):
        for h in range(gl):
            for o in range(S5_GROUP):
                ex_to[t * S5_GROUP + o, t * LANES + h * S5_GROUP + o] = 1.0
    ex_p = np.kron(np.eye(4, dtype=np.float32), np.tile(np.eye(S5_STATE, dtype=np.float32), (1, gl)))
    g_sgi = (np.arange(t_len * LANES) // S5_GROUP) % gl
    g_tho = (np.arange(t_len * LANES) // S5_GROUP) % gl
    g_hp = np.arange(gl * S5_STATE) // S5_STATE
    eq = lambda a, b: (a[:, None] == b[None, :]).astype(np.float32)
    return ex_to, ex_p, eq(g_sgi, g_tho), eq(g_sgi, np.tile(g_hp, 4)), eq(g_hp, g_tho)


def _s5_mats(lam_re, lam_im, log_step, b_re, b_im, c_re, c_im):
    t_len = S5_CHUNK
    lr = jnp.minimum(lam_re.astype(F32), -1e-4)
    li = lam_im.astype(F32)
    step = jnp.exp(log_step.astype(F32))[..., None]
    dr, di = lr * step, li * step
    d = jnp.arange(t_len + 1, dtype=F32)[:, None, None, None]
    mag = jnp.exp(d * dr)
    pr, pi = mag * jnp.cos(d * di), mag * jnp.sin(d * di)
    nr, ni = pr[1] - 1.0, pi[1]
    den = lr * lr + li * li
    cr, ci = (nr * lr + ni * li) / den, (ni * lr - nr * li) / den
    bbr = cr[..., None] * b_re - ci[..., None] * b_im
    bbi = cr[..., None] * b_im + ci[..., None] * b_re
    ctr = jnp.swapaxes(c_re.astype(F32), 2, 3)
    cti = jnp.swapaxes(c_im.astype(F32), 2, 3)
    groups = lr.shape[1]
    gl = LANES // S5_GROUP
    nq = groups // gl
    tw = t_len * S5_GROUP
    tq = lambda e: e.reshape((nq, gl) + e.shape[1:])
    ex_to, ex_p, m_rc, m_rp, m_pc = _s5_expanders()

    def spread(compact, expand, mask):
        return jnp.matmul(compact.astype(BF16), jnp.asarray(expand, BF16),
                          preferred_element_type=BF16) * jnp.asarray(mask, BF16)

    mr = (bbr[..., :, None] * ctr[..., None, :] - bbi[..., :, None] * cti[..., None, :]).reshape(
        2, groups, S5_STATE, S5_GROUP * S5_GROUP)
    mi = (bbr[..., :, None] * cti[..., None, :] + bbi[..., :, None] * ctr[..., None, :]).reshape(
        2, groups, S5_STATE, S5_GROUP * S5_GROUP)
    kern = (jnp.einsum('dxgp,xgpn->xgdn', pr, mr, precision=HIGHEST)
            - jnp.einsum('dxgp,xgpn->xgdn', pi, mi, precision=HIGHEST))
    kern = jnp.transpose(kern.reshape(2, groups, t_len + 1, S5_GROUP, S5_GROUP), (0, 1, 3, 2, 4))
    kf = kern[0, :, :, :t_len].reshape(groups, S5_GROUP, tw)
    kb = kern[1, :, :, :t_len][:, :, ::-1].reshape(groups, S5_GROUP, tw)
    zpad = jnp.zeros((groups, S5_GROUP, tw - S5_GROUP), F32)
    kf = jnp.concatenate([zpad, kf], axis=-1)
    kb = jnp.concatenate([kb, zpad], axis=-1)
    toe = jnp.stack([kf[..., (t_len - 1 - s) * S5_GROUP:(t_len - 1 - s) * S5_GROUP + tw]
                     + kb[..., (t_len - 1 - s) * S5_GROUP:(t_len - 1 - s) * S5_GROUP + tw]
                     for s in range(t_len)], axis=1)
    rows_sgi = lambda e: jnp.transpose(tq(e), (0, 2, 1, 3, 4)).reshape(nq, t_len * LANES, e.shape[-1])
    w1t = spread(rows_sgi(toe), ex_to, m_rc)
    btr, bti = jnp.swapaxes(bbr, 2, 3), jnp.swapaxes(bbi, 2, 3)

    def e_part(x, order):
        pw_r = jnp.transpose(pr[order, x], (1, 0, 2))[:, :, None, :]
        pw_i = jnp.transpose(pi[order, x], (1, 0, 2))[:, :, None, :]
        br, bi = btr[x][:, None], bti[x][:, None]
        return [rows_sgi(e) for e in (pw_r * br - pw_i * bi, pw_r * bi + pw_i * br)]

    prt, pit = jnp.transpose(pr, (1, 2, 3, 0)), jnp.transpose(pi, (1, 2, 3, 0))

    def q_part(x, order):
        pw_r, pw_i = prt[x][..., order][..., None], pit[x][..., order][..., None]
        c_r, c_i = ctr[x][:, :, None, :], cti[x][:, :, None, :]
        return [spread(tq(e).reshape(nq, gl * S5_STATE, tw), ex_to, m_pc)
                for e in (c_r * pw_r - c_i * pw_i, -(c_r * pw_i + c_i * pw_r))]

    ecat = jnp.concatenate(e_part(0, jnp.arange(t_len - 1, -1, -1)) + e_part(1, jnp.arange(t_len)), axis=2)
    w1e = spread(ecat, ex_p, m_rp)
    w2 = jnp.concatenate(q_part(0, jnp.arange(1, t_len + 1)) + q_part(1, jnp.arange(t_len, 0, -1)), axis=1)
    pl2 = lambda e: e.reshape(nq, gl * S5_STATE)
    lam_t = jnp.stack([pl2(pr[t_len, 0]), pl2(pi[t_len, 0]), pl2(pr[t_len, 1]), pl2(pi[t_len, 1])], axis=1)
    return w1t.astype(BF16), w1e.astype(BF16), w2.astype(BF16), lam_t


def _s5_kernel(u_ref, w1t_ref, w1e_ref, w2_ref, lam_ref, s0_ref, y_ref, sfin_ref, r_ref, *, nc):
    sw = (LANES // S5_GROUP) * S5_STATE
    yw = S5_CHUNK * LANES
    nb = u_ref.shape[0]
    u = u_ref[...].reshape(nb * nc, yw)
    r_ref[:, 0:yw] = jnp.dot(u, w1t_ref[0], preferred_element_type=F32)
    r_ref[:, yw:yw + 4 * sw] = jnp.dot(u, w1e_ref[0], preferred_element_type=F32)
    lfr, lfi, lbr, lbi = (lam_ref[0, i:i + 1, :] for i in range(4))
    cols = [slice(yw + i * sw, yw + (i + 1) * sw) for i in range(4)]
    rid = lax.broadcasted_iota(jnp.int32, (SUBLANES, sw), 0)
    nblk = nc // SUBLANES

    def block(kb, carry):
        out = []
        for b in range(nb):
            fr, fi, br, bi = carry[4 * b:4 * b + 4]
            rf = pl.ds(pl.multiple_of(b * nc + kb * SUBLANES, SUBLANES), SUBLANES)
            rb = pl.ds(pl.multiple_of(b * nc + (nblk - 1 - kb) * SUBLANES, SUBLANES), SUBLANES)
            efr, efi = r_ref[rf, cols[0]], r_ref[rf, cols[1]]
            ebr, ebi = r_ref[rb, cols[2]], r_ref[rb, cols[3]]
            xfr, xfi, xbr, xbi = efr, efi, ebr, ebi
            for s in range(SUBLANES):
                xfr = jnp.where(rid == s, fr, xfr)
                xfi = jnp.where(rid == s, fi, xfi)
                fr, fi = (lfr * fr - lfi * fi + efr[s:s + 1], lfr * fi + lfi * fr + efi[s:s + 1])
                z = SUBLANES - 1 - s
                xbr = jnp.where(rid == z, br, xbr)
                xbi = jnp.where(rid == z, bi, xbi)
                br, bi = (lbr * br - lbi * bi + ebr[z:z + 1], lbr * bi + lbi * br + ebi[z:z + 1])
            r_ref[rf, cols[0]] = xfr
            r_ref[rf, cols[1]] = xfi
            r_ref[rb, cols[2]] = xbr
            r_ref[rb, cols[3]] = xbi
            out += [fr, fi, br, bi]
        return tuple(out)

    fin = lax.fori_loop(0, nblk, block, tuple(s0_ref[b, 0, i:i + 1, :] for b in range(nb) for i in range(4)))
    for b in range(nb):
        for i in range(4):
            sfin_ref[b, 0, i:i + 1, :] = fin[4 * b + i]
    xin = r_ref[:, yw:yw + 4 * sw].astype(BF16)
    y = r_ref[:, 0:yw] + jnp.dot(xin, w2_ref[0], preferred_element_type=F32)
    y_ref[...] = y.reshape(nb, nc, yw)


def _s5(uc, w1t, w1e, w2, lam_t, layer, s0):
    bsz, nc, cw = uc.shape
    nq = w1t.shape[1]
    tw = cw // nq
    sw = lam_t.shape[-1]
    nb = max(1, min(bsz, 512 // nc))
    assert bsz % nb == 0
    st = pl.BlockSpec((nb, 1, 4, sw), lambda q, b: (b, q, 0, 0))
    single = pl.Buffered(1)
    return pl.pallas_call(
        functools.partial(_s5_kernel, nc=nc),
        grid=(nq, bsz // nb),
        in_specs=[pl.BlockSpec((nb, nc, tw), lambda q, b: (b, 0, q)),
                  pl.BlockSpec((None, 1) + w1t.shape[2:], lambda q, b: (layer, q, 0, 0), pipeline_mode=single),
                  pl.BlockSpec((None, 1) + w1e.shape[2:], lambda q, b: (layer, q, 0, 0), pipeline_mode=single),
                  pl.BlockSpec((None, 1) + w2.shape[2:], lambda q, b: (layer, q, 0, 0), pipeline_mode=single),
                  pl.BlockSpec((None, 1, 4, sw), lambda q, b: (layer, q, 0, 0)),
                  st],
        out_specs=[pl.BlockSpec((nb, nc, tw), lambda q, b: (b, 0, q)), st],
        out_shape=[jax.ShapeDtypeStruct((bsz, nc, cw), F32), jax.ShapeDtypeStruct(s0.shape, F32)],
        scratch_shapes=[pltpu.VMEM((nb * nc, w1t.shape[3] + w1e.shape[3]), F32)],
        compiler_params=_params(("parallel", "arbitrary")),
        name="s5_scan",
    )(uc, w1t, w1e, w2, lam_t, s0)


def _s5out_kernel(y_ref, u_ref, d_ref, w_ref, b_ref, o_ref, tok_ref):
    nchunk = y_ref.shape[0]
    nq = tok_ref.shape[0]
    for q in range(nq):
        for t in range(S5_CHUNK):
            c0 = (q * S5_CHUNK + t) * LANES
            tok_ref[q, pl.ds(t, nchunk, stride=S5_CHUNK), :] = y_ref[:, c0:c0 + LANES]
    y = jnp.concatenate([tok_ref[q] for q in range(nq)], axis=-1)
    z = jax.nn.gelu(y + d_ref[...] * u_ref[...].astype(F32))
    gate = jnp.dot(z.astype(BF16), w_ref[...], preferred_element_type=F32) + b_ref[...]
    o_ref[...] = (z * jax.nn.sigmoid(gate)).astype(o_ref.dtype)


def _s5_out(yc, a, u_col, d, glu_w, glu_b):
    width = d.shape[-1]
    r = a.shape[0]
    bm = min(1024, r)
    return pl.pallas_call(
        _s5out_kernel,
        grid=(r // bm,),
        in_specs=[pl.BlockSpec((bm // S5_CHUNK, S5_CHUNK * width), lambda i: (i, 0)),
                  pl.BlockSpec((bm, width), lambda i: (i, u_col)),
                  _const_spec((1, width)), _const_spec((width, width)), _const_spec((1, width))],
        out_specs=pl.BlockSpec((bm, width), lambda i: (i, 0)),
        out_shape=jax.ShapeDtypeStruct((r, width), BF16),
        scratch_shapes=[pltpu.VMEM((width // LANES, bm, LANES), F32)],
        compiler_params=_params(("parallel",)),
        name="s5_out",
    )(yc, a, d.reshape(1, width).astype(F32), glu_w.astype(BF16), glu_b.reshape(1, width).astype(F32))


def _ffnact_kernel(*refs, rows, width, chunk):
    halo = rows > 1
    if halo:
        g_ref, ga_ref, gb_ref, v_ref, cw_ref, cb_ref, o_ref, pad_ref = refs
    else:
        g_ref, v_ref, cw_ref, cb_ref, o_ref = refs
    i = pl.program_id(0)
    bm, fk = g_ref.shape

    if halo:
        per_img = (rows * width) // bm
        top = (i % per_img) == 0
        bottom = (i % per_img) == per_img - 1
        pad_ref[0:width, :] = jnp.where(top, 0.0, ga_ref[...].astype(F32))
        pad_ref[width:width + bm, :] = g_ref[...].astype(F32)
        pad_ref[width + bm:2 * width + bm, :] = jnp.where(bottom, 0.0, gb_ref[...].astype(F32))
    col = lax.broadcasted_iota(jnp.int32, (chunk, LANES), 0) % width
    not_first = col != 0
    not_last = col != width - 1
    bias = cb_ref[...]
    wt = [cw_ref[t:t + 1, :] for t in range(9)]

    def body(c, carry):
        r = pl.multiple_of(c * chunk, chunk)
        for lb in range(fk // LANES):
            ls = slice(lb * LANES, (lb + 1) * LANES)
            w9 = [t[:, ls] for t in wt]
            if halo:
                up = pad_ref[pl.ds(r, chunk), ls]
                mid = pad_ref[pl.ds(width + r, chunk), ls]
                dn = pad_ref[pl.ds(2 * width + r, chunk), ls]
                v0, v1, v2 = (up * w9[dw] + mid * w9[3 + dw] + dn * w9[6 + dw] for dw in range(3))
            else:
                mid = g_ref[pl.ds(r, chunk), ls].astype(F32)
                v0, v1, v2 = (mid * w9[3 + dw] for dw in range(3))
            left = jnp.where(not_first, pltpu.roll(v0, 1, 0), 0.0)
            right = jnp.where(not_last, pltpu.roll(v2, chunk - 1, 0), 0.0)
            gate = jax.nn.gelu(v1 + left + right + bias[:, ls])
            o_ref[pl.ds(r, chunk), ls] = (gate * v_ref[pl.ds(r, chunk), ls].astype(F32)).astype(o_ref.dtype)
        return carry

    lax.fori_loop(0, bm // chunk, body, 0, unroll=2 if (bm // chunk) % 2 == 0 else 1)


def _ffn_act(gv, conv_w, conv_b, rows, width):
    r, f2 = gv.shape
    dff = f2 // 2
    bm = min(1024, r)
    fk = 512
    nk = dff // fk
    chunk = width
    assert bm % chunk == 0 and ((rows * width) % bm == 0 if rows > 1 else bm % width == 0)
    in_specs = [pl.BlockSpec((bm, fk), lambda i, k: (i, k))]
    args = [gv]
    scratch = []
    if rows > 1:
        per = bm // width
        last = r // width - 1
        in_specs += [pl.BlockSpec((width, fk), lambda i, k: (jnp.maximum(i * per - 1, 0), k)),
                     pl.BlockSpec((width, fk), lambda i, k: (jnp.minimum((i + 1) * per, last), k))]
        args += [gv, gv]
        scratch.append(pltpu.VMEM((bm + 2 * width, fk), F32))
    in_specs += [pl.BlockSpec((bm, fk), lambda i, k: (i, nk + k)),
                 pl.BlockSpec((9, fk), lambda i, k: (0, k)),
                 pl.BlockSpec((1, fk), lambda i, k: (0, k))]
    args += [gv, conv_w.reshape(9, dff).astype(F32), conv_b.reshape(1, dff).astype(F32)]
    return pl.pallas_call(
        functools.partial(_ffnact_kernel, rows=rows, width=width, chunk=chunk),
        grid=(r // bm, nk),
        in_specs=in_specs,
        out_specs=pl.BlockSpec((bm, fk), lambda i, k: (i, k)),
        out_shape=jax.ShapeDtypeStruct((r, dff), BF16),
        scratch_shapes=scratch,
        compiler_params=_params(("parallel", "parallel")),
        name="ffn_act",
    )(*args)


def _rms_kernel(x_ref, g_ref, o_ref):
    x = x_ref[...]
    o_ref[...] = x * lax.rsqrt(jnp.mean(x * x, axis=-1, keepdims=True) + EPS) * g_ref[...]


def _rmsnorm(x, g):
    r, d = x.shape
    bm = min(1024, r)
    return pl.pallas_call(
        _rms_kernel,
        grid=(r // bm,),
        in_specs=[pl.BlockSpec((bm, d), lambda i: (i, 0)), _const_spec((1, d))],
        out_specs=pl.BlockSpec((bm, d), lambda i: (i, 0)),
        out_shape=jax.ShapeDtypeStruct((r, d), F32),
        compiler_params=_params(("parallel",)),
        name="final_norm",
    )(x, g.reshape(1, d).astype(F32))


def _mixer(h, mods, i, p, states, img_rows, img_width, full):
    bsz, length, d = h.shape
    r = bsz * length
    rpm = r // mods.shape[0]
    width = d // 4
    dk = d // 2 // RET_HEADS // 2
    h2 = h.reshape(r, d)
    q_off = 3 * width
    u_off = q_off + 2 * RET_HEADS * dk + 2 * (d // 2)
    a, uc = _modmm(h2, p['norm1_g'][i], mods[:, 0], mods[:, 1], p['w_in'], i, rpm, 1024, 1024,
                   chunk_cols=(u_off, width))
    a3 = a.reshape(bsz, length, -1)
    ret, ret_fin = _retention(a3, p['log_gamma'][i], states[0], q_off, dk)
    y5, s5_fin = _s5(uc.reshape(bsz, length // S5_CHUNK, -1), *p['s5_mats'], i, states[1])
    if not full:
        return None, (ret_fin, s5_fin)
    hy = _hyena(a3, p['hy_conv_w'][i], p['hy_conv_b'][i], p['hy_w1'][i], p['hy_b1'][i], p['hy_w2'][i],
                p['hy_b2'][i], p['hy_w3'][i], p['hy_freq'][i], p['hy_bias'][i], width)
    s5o = _s5_out(y5.reshape(r // S5_CHUNK, -1), a, u_off // width, p['s5_d'][i], p['s5_glu_w'][i],
                  p['s5_glu_b'][i])
    h2 = _resmm([hy.reshape(r, width), ret.reshape(r, d // 2), s5o], p['w_out'], i, h2, mods[:, 2], rpm, 512, d)
    gv = _modmm(h2, p['norm2_g'][i], mods[:, 3], mods[:, 4], p['ffn_w_up'], i, rpm, 1024, 1024)
    act = _ffn_act(gv, p['ffn_conv_w'][i], p['ffn_conv_b'][i], img_rows, img_width)
    h2 = _resmm([act], p['ffn_w_down'], i, h2, mods[:, 5], rpm, 1024, 512)
    return h2.reshape(bsz, length, d), (ret_fin, s5_fin)


def kernel(x, c, ctx, c_ctx, ada_w, ada_b, norm1_g, w_in, hy_conv_w, hy_conv_b, hy_w1, hy_b1, hy_w2, hy_b2,
           hy_w3, hy_freq, hy_bias, ret_decay, s5_lam_re, s5_lam_im, s5_log_step, s5_b_re, s5_b_im, s5_c_re,
           s5_c_im, s5_d, s5_glu_w, s5_glu_b, w_out, norm2_g, ffn_w_up, ffn_conv_w, ffn_conv_b, ffn_w_down,
           norm_f):
    bsz, length, d = x.shape
    depth = ada_w.shape[0]
    ctx_len = ctx.shape[1]
    dk = d // 2 // RET_HEADS // 2
    pairs = d // 4 // (2 * S5_GROUP)

    cc = jnp.concatenate([c_ctx[None], c, jnp.zeros((8 - 1 - bsz, d), F32)], axis=0)
    mods = _modulation(cc, ada_w, ada_b).reshape(depth, 8, N_MOD, 1, d)
    p = dict(norm1_g=norm1_g, norm2_g=norm2_g, hy_conv_w=hy_conv_w, hy_conv_b=hy_conv_b, hy_w1=hy_w1,
             hy_b1=hy_b1, hy_w2=hy_w2, hy_b2=hy_b2, hy_w3=hy_w3, hy_freq=hy_freq, hy_bias=hy_bias,
             s5_d=s5_d, s5_glu_w=s5_glu_w, s5_glu_b=s5_glu_b, ffn_conv_w=ffn_conv_w, ffn_conv_b=ffn_conv_b,
             w_in=w_in, w_out=w_out, ffn_w_up=ffn_w_up,
             ffn_w_down=ffn_w_down.astype(BF16),
             log_gamma=-jnp.exp(ret_decay.astype(F32)),
             s5_mats=jax.vmap(_s5_mats)(s5_lam_re, s5_lam_im, s5_log_step, s5_b_re, s5_b_im, s5_c_re, s5_c_im))
    zero_states = (jnp.zeros((bsz, 2, RET_HEADS, 2 * dk, 2 * dk), F32),
                   jnp.zeros((bsz, d // 4 // LANES, 4, (LANES // S5_GROUP) * S5_STATE), F32))
    h_lat, h_ctx = x, ctx
    for i in range(depth):
        last = i == depth - 1
        h_ctx, ctx_states = _mixer(h_ctx, mods[i, 0:1], i, p, zero_states, 1, ctx_len, not last)
        h_lat, _ = _mixer(h_lat, mods[i, 1:1 + bsz], i, p, ctx_states, length // GRID_W, GRID_W, True)
    return _rmsnorm(h_lat.reshape(bsz * length, d), norm_f).reshape(bsz, length, d)
```

```python
import functools
import math

import numpy as np
import jax
import jax.numpy as jnp
from jax import lax
from jax.experimental import pallas as pl
from jax.experimental.pallas import tpu as pltpu

F32 = jnp.float32
BF16 = jnp.bfloat16
HIGHEST = lax.Precision.HIGHEST

EPS = 1e-6
N_MOD = 6
GRID_W = 64
LANES = 128
SUBLANES = 8
VMEM_LIMIT_MB = 56

HY_ORDER = 2
HY_BANDS = 16
HY_FAST_DECAY = 0.3
HY_SLOW_DECAY = 1.5
HY_TARGET = 1e-2
RET_HEADS = 8
RET_CHUNK = 256
S5_GROUP = 16
S5_STATE = 64
S5_CHUNK = 16


def _params(sem, vmem_mb=VMEM_LIMIT_MB):
    return pltpu.CompilerParams(dimension_semantics=sem, vmem_limit_bytes=vmem_mb << 20)


def _const_spec(shape):
    nd = len(shape)
    return pl.BlockSpec(shape, lambda *_: (0,) * nd)


def _mod_kernel(c_ref, w_ref, b_ref, o_ref):
    s = jax.nn.silu(c_ref[...])
    w = w_ref[0]
    s_hi, w_hi = s.astype(BF16), w.astype(BF16)
    s_lo, w_lo = (s - s_hi.astype(F32)).astype(BF16), (w - w_hi.astype(F32)).astype(BF16)
    o_ref[0] = (jnp.dot(s_hi, w_hi, preferred_element_type=F32) + jnp.dot(s_lo, w_hi, preferred_element_type=F32)
                + jnp.dot(s_hi, w_lo, preferred_element_type=F32)) + b_ref[0]


def _modulation(cc, ada_w, ada_b):
    depth, d, n = ada_w.shape
    bn = 1536
    return pl.pallas_call(
        _mod_kernel,
        grid=(depth, n // bn),
        in_specs=[pl.BlockSpec((8, d), lambda i, j: (0, 0)),
                  pl.BlockSpec((1, d, bn), lambda i, j: (i, 0, j)),
                  pl.BlockSpec((1, 1, bn), lambda i, j: (i, 0, j))],
        out_specs=pl.BlockSpec((1, 8, bn), lambda i, j: (i, 0, j)),
        out_shape=jax.ShapeDtypeStruct((depth, 8, n), F32),
        compiler_params=_params(("parallel", "parallel")),
        name="adaln_mod",
    )(cc, ada_w, ada_b.reshape(depth, 1, n))


def _modmm_kernel(h_ref, g_ref, sh_ref, sc_ref, w_ref, o_ref, *rest, chunk_cols):
    xm_ref = rest[-1]

    @pl.when(pl.program_id(1) == 0)
    def _():
        rows = 2 * SUBLANES
        gain, scale1, shift = g_ref[...], 1.0 + sc_ref[0], sh_ref[0]

        def norm_rows(c, carry):
            r = pl.ds(pl.multiple_of(c * rows, rows), rows)
            x = h_ref[r, :]
            y = x * lax.rsqrt(jnp.mean(x * x, axis=-1, keepdims=True) + EPS)
            xm_ref[r, :] = (y * gain * scale1 + shift).astype(BF16)
            return carry

        lax.fori_loop(0, h_ref.shape[0] // rows, norm_rows, 0, unroll=4)

    res = jnp.dot(xm_ref[...], w_ref[...].astype(BF16), preferred_element_type=F32)
    o_ref[...] = res.astype(o_ref.dtype)
    if chunk_cols is not None:
        oc_ref, tok_ref = rest[0], rest[1]
        jblk, lo, width = chunk_cols

        @pl.when(pl.program_id(1) == jblk)
        def _():
            nchunk = oc_ref.shape[0]
            for q in range(width // LANES):
                tok_ref[q] = res[:, lo + q * LANES:lo + (q + 1) * LANES]
                for t in range(S5_CHUNK):
                    c0 = (q * S5_CHUNK + t) * LANES
                    oc_ref[:, c0:c0 + LANES] = tok_ref[q, pl.ds(t, nchunk, stride=S5_CHUNK), :].astype(oc_ref.dtype)


def _modmm(h, g, shift, scale, w, layer, rows_per_mod, bm, bn, chunk_cols=None):
    r, d = h.shape
    n = w.shape[2]
    bm = min(bm, r)
    mod_idx = lambda i, j: ((i * bm) // rows_per_mod, 0, 0)
    out_specs = pl.BlockSpec((bm, bn), lambda i, j: (i, j))
    out_shape = jax.ShapeDtypeStruct((r, n), BF16)
    scratch = [pltpu.VMEM((bm, d), BF16)]
    cc = None
    if chunk_cols is not None:
        start, width = chunk_cols
        assert start // bn == (start + width - 1) // bn
        cc = (start // bn, start % bn, width)
        out_specs = [out_specs, pl.BlockSpec((bm // S5_CHUNK, S5_CHUNK * width), lambda i, j: (i, 0))]
        out_shape = [out_shape, jax.ShapeDtypeStruct((r // S5_CHUNK, S5_CHUNK * width), BF16)]
        scratch = [pltpu.VMEM((width // LANES, bm, LANES), F32)] + scratch
    return pl.pallas_call(
        functools.partial(_modmm_kernel, chunk_cols=cc),
        grid=(r // bm, n // bn),
        in_specs=[pl.BlockSpec((bm, d), lambda i, j: (i, 0)),
                  pl.BlockSpec((1, d), lambda i, j: (0, 0)),
                  pl.BlockSpec((1, 1, d), mod_idx),
                  pl.BlockSpec((1, 1, d), mod_idx),
                  pl.BlockSpec((None, d, bn), lambda i, j: (layer, 0, j))],
        out_specs=out_specs,
        out_shape=out_shape,
        scratch_shapes=scratch,
        compiler_params=_params(("parallel", "arbitrary")),
        name="modmm",
    )(h, g.reshape(1, d), shift, scale, w)


def _resmm_kernel(*refs, n_in):
    x_refs = refs[:n_in]
    w_ref, h_ref, gate_ref, o_ref = refs[n_in:]
    off = 0
    acc = None
    for x_ref in x_refs:
        k = x_ref.shape[1]
        part = jnp.dot(x_ref[...].astype(BF16), w_ref[off:off + k, :].astype(BF16), preferred_element_type=F32)
        acc = part if acc is None else acc + part
        off += k
    o_ref[...] = h_ref[...] + gate_ref[0] * acc


def _resmm(xs, w, layer, h, gate, rows_per_mod, bm, bn):
    r, n = h.shape
    bm = min(bm, r)
    k = w.shape[1]
    w_mode = pl.Buffered(1) if bn == n else None
    in_specs = [pl.BlockSpec((bm, x.shape[1]), lambda i, j: (i, 0)) for x in xs]
    in_specs += [pl.BlockSpec((None, k, bn), lambda i, j: (layer, 0, j), pipeline_mode=w_mode),
                 pl.BlockSpec((bm, bn), lambda i, j: (i, j)),
                 pl.BlockSpec((1, 1, bn), lambda i, j: ((i * bm) // rows_per_mod, 0, j))]
    return pl.pallas_call(
        functools.partial(_resmm_kernel, n_in=len(xs)),
        grid=(r // bm, n // bn),
        in_specs=in_specs,
        out_specs=pl.BlockSpec((bm, bn), lambda i, j: (i, j)),
        out_shape=jax.ShapeDtypeStruct((r, n), F32),
        compiler_params=_params(("parallel", "arbitrary")),
        name="resmm",
    )(*xs, w, h, gate)


def _shortconv_kernel(x_ref, w_ref, b_ref, o_ref, pad_ref, *, length, chunk):
    cb = x_ref.shape[-1]
    zeros = jnp.zeros((8, cb), F32)
    pad_ref[0:8, :] = zeros
    pad_ref[length + 8:length + 16, :] = zeros
    pad_ref[8:length + 8, :] = x_ref[0].astype(F32)
    w0, w1, w2 = w_ref[0:1, :], w_ref[1:2, :], w_ref[2:3, :]
    b = b_ref[...]
    for c in range(length // chunk):
        r = c * chunk
        o_ref[0, r:r + chunk, :] = (pad_ref[r + 7:r + 7 + chunk, :] * w0 + pad_ref[r + 8:r + 8 + chunk, :] * w1
                                    + pad_ref[r + 9:r + 9 + chunk, :] * w2 + b)


def _shortconv(a, w, b, width):
    bsz, length, _ = a.shape
    cb = 256
    chunk = min(512, length)
    return pl.pallas_call(
        functools.partial(_shortconv_kernel, length=length, chunk=chunk),
        grid=(bsz, width // cb),
        in_specs=[pl.BlockSpec((1, length, cb), lambda i, j: (i, 0, j)),
                  pl.BlockSpec((3, cb), lambda i, j: (0, j)),
                  pl.BlockSpec((1, cb), lambda i, j: (0, j))],
        out_specs=pl.BlockSpec((1, length, cb), lambda i, j: (i, 0, j)),
        out_shape=jax.ShapeDtypeStruct((bsz, length, width), F32),
        scratch_shapes=[pltpu.VMEM((length + 16, cb), F32)],
        compiler_params=_params(("parallel", "parallel")),
        name="hy_shortconv",
    )(a, w, b.reshape(1, width))


def _taps_kernel(z_ref, w1_ref, b1_ref, w2_ref, b2_ref, w3_ref, f_ref, dl_ref, o_ref, *, length):
    z = z_ref[...]
    h = jnp.sin(f_ref[0:1, :] * (jnp.dot(z, w1_ref[...], preferred_element_type=F32, precision=HIGHEST)
                                  + b1_ref[...]))
    h = jnp.sin(f_ref[1:2, :] * (jnp.dot(h, w2_ref[...], preferred_element_type=F32, precision=HIGHEST)
                                  + b2_ref[...]))
    w3 = w3_ref[...]
    h_hi, w_hi = h.astype(BF16), w3.astype(BF16)
    h_lo, w_lo = (h - h_hi.astype(F32)).astype(BF16), (w3 - w_hi.astype(F32)).astype(BF16)
    h = (jnp.dot(h_hi, w_hi, preferred_element_type=F32) + jnp.dot(h_lo, w_hi, preferred_element_type=F32)
         + jnp.dot(h_hi, w_lo, preferred_element_type=F32))
    t = z[:, 0:1]
    h = h * jnp.exp(-t * jnp.abs(dl_ref[...]))
    rb = z.shape[0]
    row = lax.broadcasted_iota(jnp.int32, h.shape, 0) + pl.program_id(0) * rb
    o_ref[...] = jnp.where(row == length, 0.0, h)


def _hyena_taps(length, w1, b1, w2, b2, w3, freq, width):
    n = 2 * length
    hid = w1.shape[1]
    pos = jnp.arange(n)
    idx = jnp.where(pos < length, pos, n - pos).astype(F32)
    t = (idx / max(length - 1, 1))[:, None]
    bands = jnp.linspace(1e-4, HY_BANDS - 1, HY_BANDS, dtype=F32)
    ang = (2.0 * math.pi * idx / length)[:, None] * bands[None]
    emb = 1 + 2 * HY_BANDS
    z = jnp.concatenate([t, jnp.cos(ang), -jnp.sin(ang), jnp.zeros((n, 64 - emb), F32)], axis=-1)
    w1p = jnp.concatenate([w1.astype(F32), jnp.zeros((64 - emb, hid), F32)], axis=0)
    max_decay = math.log(HY_TARGET) / HY_FAST_DECAY
    min_decay = math.log(HY_TARGET) / HY_SLOW_DECAY
    deltas = jnp.tile(jnp.linspace(min_decay, max_decay, width, dtype=F32), HY_ORDER)[None]
    oc = HY_ORDER * width
    rb = min(1024, length)
    half = length // rb
    return pl.pallas_call(
        functools.partial(_taps_kernel, length=length),
        grid=(n // rb,),
        in_specs=[pl.BlockSpec((rb, 64), lambda i: (i, 0)),
                  _const_spec((64, hid)), _const_spec((1, hid)),
                  _const_spec((hid, hid)), _const_spec((1, hid)),
                  pl.BlockSpec((hid, oc), lambda i: (0, i // half)),
                  _const_spec((2, hid)), _const_spec((1, oc))],
        out_specs=pl.BlockSpec((rb, oc), lambda i: (i, 0)),
        out_shape=jax.ShapeDtypeStruct((n, oc), F32),
        compiler_params=_params(("parallel",)),
        name="hy_taps",
    )(z, w1p, b1.reshape(1, hid).astype(F32), w2.astype(F32), b2.reshape(1, hid).astype(F32),
      w3.astype(F32), freq.astype(F32), deltas)


def _bf16(x):
    return jnp.asarray(x).astype(BF16)


@functools.lru_cache(maxsize=None)
def _dft2_consts(length):
    n = 2 * length
    n2 = LANES
    n1 = n // n2
    hf = n1 // 2
    j = np.arange(n2)[:, None, None]
    k1 = np.arange(n1)[None, :, None]
    m1 = np.arange(n1)[None, None, :]
    ph = -2.0 * np.pi * (j * k1 / n + (m1 * k1 % n1) / n1)
    mr, mi = np.cos(ph), np.sin(ph)
    g1 = np.concatenate([np.concatenate([mr[:, :, :hf], -mi[:, :, :hf]], 2),
                         np.concatenate([mi[:, :, :hf], mr[:, :, :hf]], 2)], 1)
    g1f = np.concatenate([mr, mi], 1)
    mrt = np.transpose(mr, (0, 2, 1))[:, :hf] / n
    mit = -np.transpose(mi, (0, 2, 1))[:, :hf] / n
    g1i = np.concatenate([np.concatenate([mrt, -mit], 2),
                          np.concatenate([mit, mrt], 2)], 1)
    a = np.arange(n2)
    ph2 = -2.0 * np.pi * ((a[:, None] * a[None, :]) % n2) / n2
    fr, fi = np.cos(ph2), np.sin(ph2)
    g2 = np.block([[fr, -fi], [fi, fr]])
    g2i = np.block([[fr, fi], [-fi, fr]])
    f32 = lambda m: np.asarray(m, np.float32)
    return dict(g1=f32(g1), g1i=f32(g1i), g2=f32(g2), g2i=f32(g2i), g1f=f32(g1f))


@functools.lru_cache(maxsize=None)
def _dft1_consts(length):
    n = 2 * length
    a = np.arange(n)
    ph = -2.0 * np.pi * ((a[:, None] * a[None, :]) % n) / n
    fr, fi = np.cos(ph), np.sin(ph)
    gf = np.block([[fr[:, :length], -fi[:, :length]], [fi[:, :length], fr[:, :length]]])
    gi = np.block([[fr[:length], fi[:length]], [-fi[:length], fr[:length]]]) / n
    gff = np.concatenate([fr, fi], 0)
    f32 = lambda m: np.asarray(m, np.float32)
    return dict(gf=f32(gf), gi=f32(gi), gff=f32(gff))


def _fspec2_kernel(k_ref, g1_ref, g2_ref, kr_ref, ki_ref, *, n1):
    inv = 1.0 / jnp.sum(jnp.abs(k_ref[...]), axis=0, keepdims=True)

    def s1(j, c):
        x = k_ref[pl.ds(j, n1, stride=LANES), :].astype(BF16)
        a = jnp.dot(g1_ref[j], x, preferred_element_type=F32)
        kr_ref[pl.ds(j, n1, stride=LANES), :] = a[:n1]
        ki_ref[pl.ds(j, n1, stride=LANES), :] = a[n1:]
        return c

    lax.fori_loop(0, LANES, s1, 0, unroll=8)

    def s2(k1, c):
        r0 = pl.multiple_of(k1 * LANES, LANES)
        a = jnp.concatenate([kr_ref[pl.ds(r0, LANES), :], ki_ref[pl.ds(r0, LANES), :]], axis=0).astype(BF16)
        x = jnp.dot(g2_ref[...], a, preferred_element_type=F32) * inv
        kr_ref[pl.ds(r0, LANES), :] = x[:LANES]
        ki_ref[pl.ds(r0, LANES), :] = x[LANES:]
        return c

    lax.fori_loop(0, n1, s2, 0, unroll=8)


def _filter_spectrum2(taps):
    n, oc = taps.shape
    cst = _dft2_consts(n // 2)
    n1 = n // LANES
    cb = LANES
    g1, g2 = _bf16(cst["g1f"]), _bf16(cst["g2"])
    out = jax.ShapeDtypeStruct((n, oc), F32)
    return pl.pallas_call(
        functools.partial(_fspec2_kernel, n1=n1),
        grid=(oc // cb,),
        in_specs=[pl.BlockSpec((n, cb), lambda i: (0, i)), _const_spec(g1.shape), _const_spec(g2.shape)],
        out_specs=[pl.BlockSpec((n, cb), lambda i: (0, i))] * 2,
        out_shape=[out, out],
        compiler_params=_params(("parallel",)),
        name="hy_fspec2",
    )(taps, g1, g2)


def _fspec1_kernel(k_ref, g_ref, kr_ref, ki_ref):
    k = k_ref[...]
    n = k.shape[0]
    inv = 1.0 / jnp.sum(jnp.abs(k), axis=0, keepdims=True)
    x = jnp.dot(g_ref[...], k.astype(BF16), preferred_element_type=F32) * inv
    kr_ref[...] = x[:n]
    ki_ref[...] = x[n:]


def _filter_spectrum1(taps):
    n, oc = taps.shape
    g = _bf16(_dft1_consts(n // 2)["gff"])
    cb = 256
    out = jax.ShapeDtypeStruct((n, oc), F32)
    return pl.pallas_call(
        _fspec1_kernel,
        grid=(oc // cb,),
        in_specs=[pl.BlockSpec((n, cb), lambda i: (0, i)), _const_spec(g.shape)],
        out_specs=[pl.BlockSpec((n, cb), lambda i: (0, i))] * 2,
        out_shape=[out, out],
        compiler_params=_params(("parallel",)),
        name="hy_fspec1",
    )(taps, g)


def _hyconv2_kernel(v_ref, x_ref, kr_ref, ki_ref, bias_ref, g1_ref, g1i_ref, g2_ref, g2i_ref, o_ref,
                    ar_ref, ai_ref, *, n1):
    hf = n1 // 2

    def s1(j, c):
        xa = v_ref[0, pl.ds(j, hf, stride=LANES), :]
        xb = v_ref[1, pl.ds(j, hf, stride=LANES), :]
        x = jnp.concatenate([xa, xb], axis=0).astype(BF16)
        a = jnp.dot(g1_ref[j], x, preferred_element_type=F32)
        ar_ref[pl.ds(j, n1, stride=LANES), :] = a[:n1]
        ai_ref[pl.ds(j, n1, stride=LANES), :] = a[n1:]
        return c

    lax.fori_loop(0, LANES, s1, 0, unroll=8)

    def s2(k1, c):
        r0 = pl.multiple_of(k1 * LANES, LANES)
        a = jnp.concatenate([ar_ref[pl.ds(r0, LANES), :], ai_ref[pl.ds(r0, LANES), :]], axis=0).astype(BF16)
        x = jnp.dot(g2_ref[...], a, preferred_element_type=F32)
        xr, xi = x[:LANES], x[LANES:]
        kr = kr_ref[pl.ds(r0, LANES), :]
        ki = ki_ref[pl.ds(r0, LANES), :]
        y = jnp.concatenate([xr * kr - xi * ki, xr * ki + xi * kr], axis=0).astype(BF16)
        b = jnp.dot(g2i_ref[...], y, preferred_element_type=F32)
        ar_ref[pl.ds(r0, LANES), :] = b[:LANES]
        ai_ref[pl.ds(r0, LANES), :] = b[LANES:]
        return c

    lax.fori_loop(0, n1, s2, 0, unroll=8)

    def s3(j, c):
        b = jnp.concatenate([ar_ref[pl.ds(j, n1, stride=LANES), :], ai_ref[pl.ds(j, n1, stride=LANES), :]],
                            axis=0).astype(BF16)
        y = jnp.dot(g1i_ref[j], b, preferred_element_type=F32)
        o_ref[0, pl.ds(j, hf, stride=LANES), :] = y[:hf]
        o_ref[1, pl.ds(j, hf, stride=LANES), :] = y[hf:]
        return c

    lax.fori_loop(0, LANES, s3, 0, unroll=8)
    bias = bias_ref[...]
    for b in range(2):
        o_ref[b] = x_ref[b] * (o_ref[b] + bias * v_ref[b])


def _hyconv2(va, v_col, xa, x_col, kr, ki, k_col, bias, length):
    bsz = va.shape[0]
    width = bias.shape[-1]
    cst = _dft2_consts(length)
    n = 2 * length
    n1 = n // LANES
    cb = LANES
    ncb = width // cb
    return pl.pallas_call(
        functools.partial(_hyconv2_kernel, n1=n1),
        grid=(ncb, bsz // 2),
        in_specs=[pl.BlockSpec((2, length, cb), lambda c, q: (q, 0, v_col + c)),
                  pl.BlockSpec((2, length, cb), lambda c, q: (q, 0, x_col + c)),
                  pl.BlockSpec((n, cb), lambda c, q: (0, k_col + c), pipeline_mode=pl.Buffered(1)),
                  pl.BlockSpec((n, cb), lambda c, q: (0, k_col + c), pipeline_mode=pl.Buffered(1)),
                  pl.BlockSpec((1, cb), lambda c, q: (0, c)),
                  _const_spec(cst["g1"].shape), _const_spec(cst["g1i"].shape),
                  _const_spec(cst["g2"].shape), _const_spec(cst["g2i"].shape)],
        out_specs=pl.BlockSpec((2, length, cb), lambda c, q: (q, 0, c)),
        out_shape=jax.ShapeDtypeStruct((bsz, length, width), F32),
        scratch_shapes=[pltpu.VMEM((n, cb), F32), pltpu.VMEM((n, cb), F32)],
        compiler_params=_params(("parallel", "arbitrary")),
        name="hy_conv2",
    )(va, xa, kr, ki, bias.reshape(1, width), _bf16(cst["g1"]), _bf16(cst["g1i"]), _bf16(cst["g2"]),
      _bf16(cst["g2i"]))


def _hyconv1_kernel(v_ref, x_ref, kr_ref, ki_ref, bias_ref, gf_ref, gi_ref, o_ref):
    length = v_ref.shape[1]
    n = 2 * length
    x = jnp.concatenate([v_ref[0], v_ref[1]], axis=0).astype(BF16)
    s = jnp.dot(gf_ref[...], x, preferred_element_type=F32)
    sr, si = s[:n], s[n:]
    kr, ki = kr_ref[...], ki_ref[...]
    y = jnp.concatenate([sr * kr - si * ki, sr * ki + si * kr], axis=0).astype(BF16)
    out = jnp.dot(gi_ref[...], y, preferred_element_type=F32)
    bias = bias_ref[...]
    for b in range(2):
        o_ref[b] = x_ref[b] * (out[b * length:(b + 1) * length] + bias * v_ref[b])


def _hyconv1(va, v_col, xa, x_col, kr, ki, k_col, bias, length):
    bsz = va.shape[0]
    width = bias.shape[-1]
    cst = _dft1_consts(length)
    n = 2 * length
    cb = LANES
    return pl.pallas_call(
        _hyconv1_kernel,
        grid=(width // cb, bsz // 2),
        in_specs=[pl.BlockSpec((2, length, cb), lambda c, q: (q, 0, v_col + c)),
                  pl.BlockSpec((2, length, cb), lambda c, q: (q, 0, x_col + c)),
                  pl.BlockSpec((n, cb), lambda c, q: (0, k_col + c)),
                  pl.BlockSpec((n, cb), lambda c, q: (0, k_col + c)),
                  pl.BlockSpec((1, cb), lambda c, q: (0, c)),
                  _const_spec(cst["gf"].shape), _const_spec(cst["gi"].shape)],
        out_specs=pl.BlockSpec((2, length, cb), lambda c, q: (q, 0, c)),
        out_shape=jax.ShapeDtypeStruct((bsz, length, width), F32),
        compiler_params=_params(("parallel", "arbitrary")),
        name="hy_conv1",
    )(va, xa, kr, ki, bias.reshape(1, width), _bf16(cst["gf"]), _bf16(cst["gi"]))


def _hyena(a, conv_w, conv_b, w1, b1, w2, b2, w3, freq, bias, width):
    bsz, length, _ = a.shape
    p = _shortconv(a, conv_w, conv_b, 3 * width)
    taps = _hyena_taps(length, w1, b1, w2, b2, w3, freq, width)
    two_stage = (2 * length) % (LANES * 16) == 0
    kr, ki = (_filter_spectrum2 if two_stage else _filter_spectrum1)(taps)
    conv = _hyconv2 if two_stage else _hyconv1
    ncb = width // LANES
    z = conv(p, 0, p, ncb, kr, ki, 0, bias[0], length)
    return conv(z, 0, p, 2 * ncb, kr, ki, ncb, bias[1], length)


def _ret_kernel(lg_ref, q_ref, k_ref, v_ref, g_ref, s0_ref, o_ref, sfin_ref, sb_ref, *, length, chunk, dk):
    hp = pl.program_id(1)
    nc = length // chunk
    dv = LANES
    row = lax.broadcasted_iota(jnp.int32, (chunk, chunk), 0)
    col = lax.broadcasted_iota(jnp.int32, (chunk, chunk), 1)
    diff = (row - col).astype(F32)
    lane = lax.broadcasted_iota(jnp.int32, (chunk, 2 * dk), 1)
    pos = lax.broadcasted_iota(jnp.int32, (chunk, 2 * dk), 0).astype(F32)
    ones_s = jnp.ones((2 * dk, dv), F32)
    kscale = dk ** -0.5
    tn = (((0,), (0,)), ((), ()))
    nt = (((1,), (1,)), ((), ()))
    hd = []
    for hh in range(2):
        lgf = lg_ref[0, hp * 2 + hh]
        lgb = lg_ref[1, hp * 2 + hh]
        hd.append(dict(
            decay=jnp.where(diff >= 0.0, jnp.exp(jnp.maximum(diff, 0.0) * lgf),
                            jnp.exp(jnp.maximum(-diff, 0.0) * lgb)),
            qmask=(lane >= dk * hh) & (lane < dk * (hh + 1)),
            qf=jnp.exp((pos + 1.0) * lgf), qb=jnp.exp((chunk - pos) * lgb),
            kf=jnp.exp((chunk - 1.0 - pos) * lgf) * kscale, kb=jnp.exp(pos * lgb) * kscale,
            cdf=jnp.exp(ones_s * (chunk * lgf)), cdb=jnp.exp(ones_s * (chunk * lgb)),
            vs=slice(dv * hh, dv * (hh + 1))))

    def bstep(i, states):
        n = nc - 1 - i
        r0 = pl.multiple_of(n * chunk, chunk)
        k = k_ref[0, pl.ds(r0, chunk), :].astype(F32)
        out = []
        for hh, c in enumerate(hd):
            sb_ref[hh, n] = states[hh]
            v = v_ref[0, pl.ds(r0, chunk), c['vs']]
            inc = lax.dot_general((k * c['kb']).astype(BF16), v, tn, preferred_element_type=F32)
            out.append(c['cdb'] * states[hh] + inc)
        return tuple(out)

    fin = lax.fori_loop(0, nc, bstep, (s0_ref[0, 1, 0], s0_ref[0, 1, 1]), unroll=4 if nc % 4 == 0 else 1)
    sfin_ref[0, 1, 0] = fin[0]
    sfin_ref[0, 1, 1] = fin[1]

    def fstep(n, states):
        r0 = pl.multiple_of(n * chunk, chunk)
        qa = q_ref[0, pl.ds(r0, chunk), :].astype(F32)
        k = k_ref[0, pl.ds(r0, chunk), :].astype(F32)
        ks = (k * kscale).astype(BF16)
        out = []
        for hh, c in enumerate(hd):
            q = jnp.where(c['qmask'], qa, 0.0)
            v = v_ref[0, pl.ds(r0, chunk), c['vs']]
            s = lax.dot_general(q.astype(BF16), ks, nt, preferred_element_type=F32)
            y = jnp.dot((s * c['decay']).astype(BF16), v, preferred_element_type=F32)
            y = y + jnp.dot((q * c['qf']).astype(BF16), states[hh].astype(BF16), preferred_element_type=F32)
            y = y + jnp.dot((q * c['qb']).astype(BF16), sb_ref[hh, n].astype(BF16), preferred_element_type=F32)
            y = y * lax.rsqrt(jnp.mean(y * y, axis=-1, keepdims=True) + EPS)
            g = g_ref[0, pl.ds(r0, chunk), c['vs']].astype(F32)
            o_ref[0, pl.ds(r0, chunk), c['vs']] = (jax.nn.silu(g) * y).astype(o_ref.dtype)
            inc = lax.dot_general((k * c['kf']).astype(BF16), v, tn, preferred_element_type=F32)
            out.append(c['cdf'] * states[hh] + inc)
        return tuple(out)

    fin = lax.fori_loop(0, nc, fstep, (s0_ref[0, 0, 0], s0_ref[0, 0, 1]), unroll=4 if nc % 4 == 0 else 1)
    sfin_ref[0, 0, 0] = fin[0]
    sfin_ref[0, 0, 1] = fin[1]


def _retention(a, log_gamma, s0, q_off, dk):
    bsz, length, _ = a.shape
    heads = RET_HEADS
    dv = 2 * dk
    assert dv == LANES
    chunk = min(RET_CHUNK, length)
    qb = q_off // (2 * dk)
    kb = qb + heads // 2
    vb = (q_off + 2 * heads * dk) // (2 * dv)
    gb = vb + heads // 2
    seq = lambda blk, off: pl.BlockSpec((1, length, blk), lambda b, h, lg: (b, 0, off + h))
    st = pl.BlockSpec((1, 2, 2, 2 * dk, dv), lambda b, h, lg: (b, 0, h, 0, 0))
    grid_spec = pltpu.PrefetchScalarGridSpec(
        num_scalar_prefetch=1,
        grid=(bsz, heads // 2),
        in_specs=[seq(2 * dk, qb), seq(2 * dk, kb), seq(2 * dv, vb), seq(2 * dv, gb), st],
        out_specs=[pl.BlockSpec((1, length, 2 * dv), lambda b, h, lg: (b, 0, h)), st],
        scratch_shapes=[pltpu.VMEM((2, length // chunk, 2 * dk, dv), F32)],
    )
    return pl.pallas_call(
        functools.partial(_ret_kernel, length=length, chunk=chunk, dk=dk),
        grid_spec=grid_spec,
        out_shape=[jax.ShapeDtypeStruct((bsz, length, heads * dv), BF16),
                   jax.ShapeDtypeStruct(s0.shape, F32)],
        compiler_params=_params(("parallel", "parallel")),
        name="retention",
    )(log_gamma, a, a, a, a, s0)


@functools.lru_cache(maxsize=None)
def _s5_expanders():
    gl = LANES // S5_GROUP
    t_len = S5_CHUNK
    ex_to = np.zeros((t_len * S5_GROUP, t_len * LANES), np.float32)
    for t in range(t_len):
        for h in range(gl):
            for o in range(S5_GROUP):
                ex_to[t * S5_GROUP + o, t * LANES + h * S5_GROUP + o] = 1.0
    ex_p = np.kron(np.eye(4, dtype=np.float32), np.tile(np.eye(S5_STATE, dtype=np.float32), (1, gl)))
    g_sgi = (np.arange(t_len * LANES) // S5_GROUP) % gl
    g_tho = (np.arange(t_len * LANES) // S5_GROUP) % gl
    g_hp = np.arange(gl * S5_STATE) // S5_STATE
    eq = lambda a, b: (a[:, None] == b[None, :]).astype(np.float32)
    return ex_to, ex_p, eq(g_sgi, g_tho), eq(g_sgi, np.tile(g_hp, 4)), eq(g_hp, g_tho)


def _s5_mats(lam_re, lam_im, log_step, b_re, b_im, c_re, c_im):
    t_len = S5_CHUNK
    lr = jnp.minimum(lam_re.astype(F32), -1e-4)
    li = lam_im.astype(F32)
    step = jnp.exp(log_step.astype(F32))[..., None]
    dr, di = lr * step, li * step
    d = jnp.arange(t_len + 1, dtype=F32)[:, None, None, None]
    mag = jnp.exp(d * dr)
    pr, pi = mag * jnp.cos(d * di), mag * jnp.sin(d * di)
    nr, ni = pr[1] - 1.0, pi[1]
    den = lr * lr + li * li
    cr, ci = (nr * lr + ni * li) / den, (ni * lr - nr * li) / den
    bbr = cr[..., None] * b_re - ci[..., None] * b_im
    bbi = cr[..., None] * b_im + ci[..., None] * b_re
    ctr = jnp.swapaxes(c_re.astype(F32), 2, 3)
    cti = jnp.swapaxes(c_im.astype(F32), 2, 3)
    groups = lr.shape[1]
    gl = LANES // S5_GROUP
    nq = groups // gl
    tw = t_len * S5_GROUP
    tq = lambda e: e.reshape((nq, gl) + e.shape[1:])
    ex_to, ex_p, m_rc, m_rp, m_pc = _s5_expanders()

    def spread(compact, expand, mask):
        return jnp.matmul(compact.astype(BF16), jnp.asarray(expand, BF16),
                          preferred_element_type=BF16) * jnp.asarray(mask, BF16)

    mr = (bbr[..., :, None] * ctr[..., None, :] - bbi[..., :, None] * cti[..., None, :]).reshape(
        2, groups, S5_STATE, S5_GROUP * S5_GROUP)
    mi = (bbr[..., :, None] * cti[..., None, :] + bbi[..., :, None] * ctr[..., None, :]).reshape(
        2, groups, S5_STATE, S5_GROUP * S5_GROUP)
    kern = (jnp.einsum('dxgp,xgpn->xgdn', pr, mr, precision=HIGHEST)
            - jnp.einsum('dxgp,xgpn->xgdn', pi, mi, precision=HIGHEST))
    kern = jnp.transpose(kern.reshape(2, groups, t_len + 1, S5_GROUP, S5_GROUP), (0, 1, 3, 2, 4))
    kf = kern[0, :, :, :t_len].reshape(groups, S5_GROUP, tw)
    kb = kern[1, :, :, :t_len][:, :, ::-1].reshape(groups, S5_GROUP, tw)
    zpad = jnp.zeros((groups, S5_GROUP, tw - S5_GROUP), F32)
    kf = jnp.concatenate([zpad, kf], axis=-1)
    kb = jnp.concatenate([kb, zpad], axis=-1)
    toe = jnp.stack([kf[..., (t_len - 1 - s) * S5_GROUP:(t_len - 1 - s) * S5_GROUP + tw]
                     + kb[..., (t_len - 1 - s) * S5_GROUP:(t_len - 1 - s) * S5_GROUP + tw]
                     for s in range(t_len)], axis=1)
    rows_sgi = lambda e: jnp.transpose(tq(e), (0, 2, 1, 3, 4)).reshape(nq, t_len * LANES, e.shape[-1])
    w1t = spread(rows_sgi(toe), ex_to, m_rc)
    btr, bti = jnp.swapaxes(bbr, 2, 3), jnp.swapaxes(bbi, 2, 3)

    def e_part(x, order):
        pw_r = jnp.transpose(pr[order, x], (1, 0, 2))[:, :, None, :]
        pw_i = jnp.transpose(pi[order, x], (1, 0, 2))[:, :, None, :]
        br, bi = btr[x][:, None], bti[x][:, None]
        return [rows_sgi(e) for e in (pw_r * br - pw_i * bi, pw_r * bi + pw_i * br)]

    prt, pit = jnp.transpose(pr, (1, 2, 3, 0)), jnp.transpose(pi, (1, 2, 3, 0))

    def q_part(x, order):
        pw_r, pw_i = prt[x][..., order][..., None], pit[x][..., order][..., None]
        c_r, c_i = ctr[x][:, :, None, :], cti[x][:, :, None, :]
        return [spread(tq(e).reshape(nq, gl * S5_STATE, tw), ex_to, m_pc)
                for e in (c_r * pw_r - c_i * pw_i, -(c_r * pw_i + c_i * pw_r))]

    ecat = jnp.concatenate(e_part(0, jnp.arange(t_len - 1, -1, -1)) + e_part(1, jnp.arange(t_len)), axis=2)
    w1e = spread(ecat, ex_p, m_rp)
    w2 = jnp.concatenate(q_part(0, jnp.arange(1, t_len + 1)) + q_part(1, jnp.arange(t_len, 0, -1)), axis=1)
    pl2 = lambda e: e.reshape(nq, gl * S5_STATE)
    lam_t = jnp.stack([pl2(pr[t_len, 0]), pl2(pi[t_len, 0]), pl2(pr[t_len, 1]), pl2(pi[t_len, 1])], axis=1)
    return w1t.astype(BF16), w1e.astype(BF16), w2.astype(BF16), lam_t


def _s5_kernel(u_ref, w1t_ref, w1e_ref, w2_ref, lam_ref, s0_ref, y_ref, sfin_ref, r_ref, *, nc):
    sw = (LANES // S5_GROUP) * S5_STATE
    yw = S5_CHUNK * LANES
    nb = u_ref.shape[0]
    u = u_ref[...].reshape(nb * nc, yw)
    r_ref[:, 0:yw] = jnp.dot(u, w1t_ref[0], preferred_element_type=F32)
    r_ref[:, yw:yw + 4 * sw] = jnp.dot(u, w1e_ref[0], preferred_element_type=F32)
    lfr, lfi, lbr, lbi = (lam_ref[0, i:i + 1, :] for i in range(4))
    cols = [slice(yw + i * sw, yw + (i + 1) * sw) for i in range(4)]
    rid = lax.broadcasted_iota(jnp.int32, (SUBLANES, sw), 0)
    nblk = nc // SUBLANES

    def block(kb, carry):
        out = []
        for b in range(nb):
            fr, fi, br, bi = carry[4 * b:4 * b + 4]
            rf = pl.ds(pl.multiple_of(b * nc + kb * SUBLANES, SUBLANES), SUBLANES)
            rb = pl.ds(pl.multiple_of(b * nc + (nblk - 1 - kb) * SUBLANES, SUBLANES), SUBLANES)
            efr, efi = r_ref[rf, cols[0]], r_ref[rf, cols[1]]
            ebr, ebi = r_ref[rb, cols[2]], r_ref[rb, cols[3]]
            xfr, xfi, xbr, xbi = efr, efi, ebr, ebi
            for s in range(SUBLANES):
                xfr = jnp.where(rid == s, fr, xfr)
                xfi = jnp.where(rid == s, fi, xfi)
                fr, fi = (lfr * fr - lfi * fi + efr[s:s + 1], lfr * fi + lfi * fr + efi[s:s + 1])
                z = SUBLANES - 1 - s
                xbr = jnp.where(rid == z, br, xbr)
                xbi = jnp.where(rid == z, bi, xbi)
                br, bi = (lbr * br - lbi * bi + ebr[z:z + 1], lbr * bi + lbi * br + ebi[z:z + 1])
            r_ref[rf, cols[0]] = xfr
            r_ref[rf, cols[1]] = xfi
            r_ref[rb, cols[2]] = xbr
            r_ref[rb, cols[3]] = xbi
            out += [fr, fi, br, bi]
        return tuple(out)

    fin = lax.fori_loop(0, nblk, block, tuple(s0_ref[b, 0, i:i + 1, :] for b in range(nb) for i in range(4)))
    for b in range(nb):
        for i in range(4):
            sfin_ref[b, 0, i:i + 1, :] = fin[4 * b + i]
    xin = r_ref[:, yw:yw + 4 * sw].astype(BF16)
    y = r_ref[:, 0:yw] + jnp.dot(xin, w2_ref[0], preferred_element_type=F32)
    y_ref[...] = y.reshape(nb, nc, yw)


def _s5(uc, w1t, w1e, w2, lam_t, layer, s0):
    bsz, nc, cw = uc.shape
    nq = w1t.shape[1]
    tw = cw // nq
    sw = lam_t.shape[-1]
    nb = max(1, min(bsz, 512 // nc))
    assert bsz % nb == 0
    st = pl.BlockSpec((nb, 1, 4, sw), lambda q, b: (b, q, 0, 0))
    single = pl.Buffered(1)
    return pl.pallas_call(
        functools.partial(_s5_kernel, nc=nc),
        grid=(nq, bsz // nb),
        in_specs=[pl.BlockSpec((nb, nc, tw), lambda q, b: (b, 0, q)),
                  pl.BlockSpec((None, 1) + w1t.shape[2:], lambda q, b: (layer, q, 0, 0), pipeline_mode=single),
                  pl.BlockSpec((None, 1) + w1e.shape[2:], lambda q, b: (layer, q, 0, 0), pipeline_mode=single),
                  pl.BlockSpec((None, 1) + w2.shape[2:], lambda q, b: (layer, q, 0, 0), pipeline_mode=single),
                  pl.BlockSpec((None, 1, 4, sw), lambda q, b: (layer, q, 0, 0)),
                  st],
        out_specs=[pl.BlockSpec((nb, nc, tw), lambda q, b: (b, 0, q)), st],
        out_shape=[jax.ShapeDtypeStruct((bsz, nc, cw), F32), jax.ShapeDtypeStruct(s0.shape, F32)],
        scratch_shapes=[pltpu.VMEM((nb * nc, w1t.shape[3] + w1e.shape[3]), F32)],
        compiler_params=_params(("parallel", "arbitrary")),
        name="s5_scan",
    )(uc, w1t, w1e, w2, lam_t, s0)


def _s5out_kernel(y_ref, u_ref, d_ref, w_ref, b_ref, o_ref, tok_ref):
    nchunk = y_ref.shape[0]
    nq = tok_ref.shape[0]
    for q in range(nq):
        for t in range(S5_CHUNK):
            c0 = (q * S5_CHUNK + t) * LANES
            tok_ref[q, pl.ds(t, nchunk, stride=S5_CHUNK), :] = y_ref[:, c0:c0 + LANES]
    y = jnp.concatenate([tok_ref[q] for q in range(nq)], axis=-1)
    z = jax.nn.gelu(y + d_ref[...] * u_ref[...].astype(F32))
    gate = jnp.dot(z.astype(BF16), w_ref[...], preferred_element_type=F32) + b_ref[...]
    o_ref[...] = (z * jax.nn.sigmoid(gate)).astype(o_ref.dtype)


def _s5_out(yc, a, u_col, d, glu_w, glu_b):
    width = d.shape[-1]
    r = a.shape[0]
    bm = min(1024, r)
    return pl.pallas_call(
        _s5out_kernel,
        grid=(r // bm,),
        in_specs=[pl.BlockSpec((bm // S5_CHUNK, S5_CHUNK * width), lambda i: (i, 0)),
                  pl.BlockSpec((bm, width), lambda i: (i, u_col)),
                  _const_spec((1, width)), _const_spec((width, width)), _const_spec((1, width))],
        out_specs=pl.BlockSpec((bm, width), lambda i: (i, 0)),
        out_shape=jax.ShapeDtypeStruct((r, width), BF16),
        scratch_shapes=[pltpu.VMEM((width // LANES, bm, LANES), F32)],
        compiler_params=_params(("parallel",)),
        name="s5_out",
    )(yc, a, d.reshape(1, width).astype(F32), glu_w.astype(BF16), glu_b.reshape(1, width).astype(F32))


def _ffnact_kernel(*refs, rows, width, chunk):
    halo = rows > 1
    if halo:
        g_ref, ga_ref, gb_ref, v_ref, cw_ref, cb_ref, o_ref, pad_ref = refs
    else:
        g_ref, v_ref, cw_ref, cb_ref, o_ref = refs
    i = pl.program_id(0)
    bm, fk = g_ref.shape

    if halo:
        per_img = (rows * width) // bm
        top = (i % per_img) == 0
        bottom = (i % per_img) == per_img - 1
        pad_ref[0:width, :] = jnp.where(top, 0.0, ga_ref[...].astype(F32))
        pad_ref[width:width + bm, :] = g_ref[...].astype(F32)
        pad_ref[width + bm:2 * width + bm, :] = jnp.where(bottom, 0.0, gb_ref[...].astype(F32))
    col = lax.broadcasted_iota(jnp.int32, (chunk, LANES), 0) % width
    not_first = col != 0
    not_last = col != width - 1
    bias = cb_ref[...]
    wt = [cw_ref[t:t + 1, :] for t in range(9)]

    def body(c, carry):
        r = pl.multiple_of(c * chunk, chunk)
        for lb in range(fk // LANES):
            ls = slice(lb * LANES, (lb + 1) * LANES)
            w9 = [t[:, ls] for t in wt]
            if halo:
                up = pad_ref[pl.ds(r, chunk), ls]
                mid = pad_ref[pl.ds(width + r, chunk), ls]
                dn = pad_ref[pl.ds(2 * width + r, chunk), ls]
                v0, v1, v2 = (up * w9[dw] + mid * w9[3 + dw] + dn * w9[6 + dw] for dw in range(3))
            else:
                mid = g_ref[pl.ds(r, chunk), ls].astype(F32)
                v0, v1, v2 = (mid * w9[3 + dw] for dw in range(3))
            left = jnp.where(not_first, pltpu.roll(v0, 1, 0), 0.0)
            right = jnp.where(not_last, pltpu.roll(v2, chunk - 1, 0), 0.0)
            gate = jax.nn.gelu(v1 + left + right + bias[:, ls])
            o_ref[pl.ds(r, chunk), ls] = (gate * v_ref[pl.ds(r, chunk), ls].astype(F32)).astype(o_ref.dtype)
        return carry

    lax.fori_loop(0, bm // chunk, body, 0, unroll=2 if (bm // chunk) % 2 == 0 else 1)


def _ffn_act(gv, conv_w, conv_b, rows, width):
    r, f2 = gv.shape
    dff = f2 // 2
    bm = min(1024, r)
    fk = 512
    nk = dff // fk
    chunk = width
    assert bm % chunk == 0 and ((rows * width) % bm == 0 if rows > 1 else bm % width == 0)
    in_specs = [pl.BlockSpec((bm, fk), lambda i, k: (i, k))]
    args = [gv]
    scratch = []
    if rows > 1:
        per = bm // width
        last = r // width - 1
        in_specs += [pl.BlockSpec((width, fk), lambda i, k: (jnp.maximum(i * per - 1, 0), k)),
                     pl.BlockSpec((width, fk), lambda i, k: (jnp.minimum((i + 1) * per, last), k))]
        args += [gv, gv]
        scratch.append(pltpu.VMEM((bm + 2 * width, fk), F32))
    in_specs += [pl.BlockSpec((bm, fk), lambda i, k: (i, nk + k)),
                 pl.BlockSpec((9, fk), lambda i, k: (0, k)),
                 pl.BlockSpec((1, fk), lambda i, k: (0, k))]
    args += [gv, conv_w.reshape(9, dff).astype(F32), conv_b.reshape(1, dff).astype(F32)]
    return pl.pallas_call(
        functools.partial(_ffnact_kernel, rows=rows, width=width, chunk=chunk),
        grid=(r // bm, nk),
        in_specs=in_specs,
        out_specs=pl.BlockSpec((bm, fk), lambda i, k: (i, k)),
        out_shape=jax.ShapeDtypeStruct((r, dff), BF16),
        scratch_shapes=scratch,
        compiler_params=_params(("parallel", "parallel")),
        name="ffn_act",
    )(*args)


def _rms_kernel(x_ref, g_ref, o_ref):
    x = x_ref[...]
    o_ref[...] = x * lax.rsqrt(jnp.mean(x * x, axis=-1, keepdims=True) + EPS) * g_ref[...]


def _rmsnorm(x, g):
    r, d = x.shape
    bm = min(1024, r)
    return pl.pallas_call(
        _rms_kernel,
        grid=(r // bm,),
        in_specs=[pl.BlockSpec((bm, d), lambda i: (i, 0)), _const_spec((1, d))],
        out_specs=pl.BlockSpec((bm, d), lambda i: (i, 0)),
        out_shape=jax.ShapeDtypeStruct((r, d), F32),
        compiler_params=_params(("parallel",)),
        name="final_norm",
    )(x, g.reshape(1, d).astype(F32))


def _mixer(h, mods, i, p, states, img_rows, img_width, full):
    bsz, length, d = h.shape
    r = bsz * length
    rpm = r // mods.shape[0]
    width = d // 4
    dk = d // 2 // RET_HEADS // 2
    h2 = h.reshape(r, d)
    q_off = 3 * width
    u_off = q_off + 2 * RET_HEADS * dk + 2 * (d // 2)
    a, uc = _modmm(h2, p['norm1_g'][i], mods[:, 0], mods[:, 1], p['w_in'], i, rpm, 1024, 1024,
                   chunk_cols=(u_off, width))
    a3 = a.reshape(bsz, length, -1)
    ret, ret_fin = _retention(a3, p['log_gamma'][i], states[0], q_off, dk)
    y5, s5_fin = _s5(uc.reshape(bsz, length // S5_CHUNK, -1), *p['s5_mats'], i, states[1])
    if not full:
        return None, (ret_fin, s5_fin)
    hy = _hyena(a3, p['hy_conv_w'][i], p['hy_conv_b'][i], p['hy_w1'][i], p['hy_b1'][i], p['hy_w2'][i],
                p['hy_b2'][i], p['hy_w3'][i], p['hy_freq'][i], p['hy_bias'][i], width)
    s5o = _s5_out(y5.reshape(r // S5_CHUNK, -1), a, u_off // width, p['s5_d'][i], p['s5_glu_w'][i],
                  p['s5_glu_b'][i])
    h2 = _resmm([hy.reshape(r, width), ret.reshape(r, d // 2), s5o], p['w_out'], i, h2, mods[:, 2], rpm, 512, d)
    gv = _modmm(h2, p['norm2_g'][i], mods[:, 3], mods[:, 4], p['ffn_w_up'], i, rpm, 1024, 1024)
    act = _ffn_act(gv, p['ffn_conv_w'][i], p['ffn_conv_b'][i], img_rows, img_width)
    h2 = _resmm([act], p['ffn_w_down'], i, h2, mods[:, 5], rpm, 1024, 512)
    return h2.reshape(bsz, length, d), (ret_fin, s5_fin)


def kernel(x, c, ctx, c_ctx, ada_w, ada_b, norm1_g, w_in, hy_conv_w, hy_conv_b, hy_w1, hy_b1, hy_w2, hy_b2,
           hy_w3, hy_freq, hy_bias, ret_decay, s5_lam_re, s5_lam_im, s5_log_step, s5_b_re, s5_b_im, s5_c_re,
           s5_c_im, s5_d, s5_glu_w, s5_glu_b, w_out, norm2_g, ffn_w_up, ffn_conv_w, ffn_conv_b, ffn_w_down,
           norm_f):
    bsz, length, d = x.shape
    depth = ada_w.shape[0]
    ctx_len = ctx.shape[1]
    dk = d // 2 // RET_HEADS // 2
    pairs = d // 4 // (2 * S5_GROUP)

    cc = jnp.concatenate([c_ctx[None], c, jnp.zeros((8 - 1 - bsz, d), F32)], axis=0)
    mods = _modulation(cc, ada_w, ada_b).reshape(depth, 8, N_MOD, 1, d)
    p = dict(norm1_g=norm1_g, norm2_g=norm2_g, hy_conv_w=hy_conv_w, hy_conv_b=hy_conv_b, hy_w1=hy_w1,
             hy_b1=hy_b1, hy_w2=hy_w2, hy_b2=hy_b2, hy_w3=hy_w3, hy_freq=hy_freq, hy_bias=hy_bias,
             s5_d=s5_d, s5_glu_w=s5_glu_w, s5_glu_b=s5_glu_b, ffn_conv_w=ffn_conv_w, ffn_conv_b=ffn_conv_b,
             w_in=w_in, w_out=w_out, ffn_w_up=ffn_w_up,
             ffn_w_down=ffn_w_down.astype(BF16),
             log_gamma=-jnp.exp(ret_decay.astype(F32)),
             s5_mats=jax.vmap(_s5_mats)(s5_lam_re, s5_lam_im, s5_log_step, s5_b_re, s5_b_im, s5_c_re, s5_c_im))
    zero_states = (jnp.zeros((bsz, 2, RET_HEADS, 2 * dk, 2 * dk), F32),
                   jnp.zeros((bsz, d // 4 // LANES, 4, (LANES // S5_GROUP) * S5_STATE), F32))
    h_lat, h_ctx = x, ctx
    for i in range(depth):
        last = i == depth - 1
        h_ctx, ctx_states = _mixer(h_ctx, mods[i, 0:1], i, p, zero_states, 1, ctx_len, not last)
        h_lat, _ = _mixer(h_lat, mods[i, 1:1 + bsz], i, p, ctx_states, length // GRID_W, GRID_W, True)
    return _rmsnorm(h_lat.reshape(bsz * length, d), norm_f).reshape(bsz, length, d)
```

```python
import functools
import math

import numpy as np
import jax
import jax.numpy as jnp
from jax import lax
from jax.experimental import pallas as pl
from jax.experimental.pallas import tpu as pltpu

F32 = jnp.float32
BF16 = jnp.bfloat16
HIGHEST = lax.Precision.HIGHEST

EPS = 1e-6
N_MOD = 6
GRID_W = 64
LANES = 128
SUBLANES = 8
VMEM_LIMIT_MB = 56

HY_ORDER = 2
HY_BANDS = 16
HY_FAST_DECAY = 0.3
HY_SLOW_DECAY = 1.5
HY_TARGET = 1e-2
RET_HEADS = 8
RET_CHUNK = 256
S5_GROUP = 16
S5_STATE = 64
S5_CHUNK = 16


def _params(sem, vmem_mb=VMEM_LIMIT_MB):
    return pltpu.CompilerParams(dimension_semantics=sem, vmem_limit_bytes=vmem_mb << 20)


def _const_spec(shape):
    nd = len(shape)
    return pl.BlockSpec(shape, lambda *_: (0,) * nd)


def _mod_kernel(c_ref, w_ref, b_ref, o_ref):
    s = jax.nn.silu(c_ref[...])
    w = w_ref[0]
    s_hi, w_hi = s.astype(BF16), w.astype(BF16)
    s_lo, w_lo = (s - s_hi.astype(F32)).astype(BF16), (w - w_hi.astype(F32)).astype(BF16)
    o_ref[0] = (jnp.dot(s_hi, w_hi, preferred_element_type=F32) + jnp.dot(s_lo, w_hi, preferred_element_type=F32)
                + jnp.dot(s_hi, w_lo, preferred_element_type=F32)) + b_ref[0]


def _modulation(cc, ada_w, ada_b):
    depth, d, n = ada_w.shape
    bn = 1536
    return pl.pallas_call(
        _mod_kernel,
        grid=(depth, n // bn),
        in_specs=[pl.BlockSpec((8, d), lambda i, j: (0, 0)),
                  pl.BlockSpec((1, d, bn), lambda i, j: (i, 0, j)),
                  pl.BlockSpec((1, 1, bn), lambda i, j: (i, 0, j))],
        out_specs=pl.BlockSpec((1, 8, bn), lambda i, j: (i, 0, j)),
        out_shape=jax.ShapeDtypeStruct((depth, 8, n), F32),
        compiler_params=_params(("parallel", "parallel")),
        name="adaln_mod",
    )(cc, ada_w, ada_b.reshape(depth, 1, n))


def _modmm_kernel(h_ref, g_ref, sh_ref, sc_ref, w_ref, o_ref, *rest, chunk_cols):
    xm_ref = rest[-1]

    @pl.when(pl.program_id(1) == 0)
    def _():
        rows = 2 * SUBLANES
        gain, scale1, shift = g_ref[...], 1.0 + sc_ref[0], sh_ref[0]

        def norm_rows(c, carry):
            r = pl.ds(pl.multiple_of(c * rows, rows), rows)
            x = h_ref[r, :]
            y = x * lax.rsqrt(jnp.mean(x * x, axis=-1, keepdims=True) + EPS)
            xm_ref[r, :] = (y * gain * scale1 + shift).astype(BF16)
            return carry

        lax.fori_loop(0, h_ref.shape[0] // rows, norm_rows, 0, unroll=4)

    res = jnp.dot(xm_ref[...], w_ref[...].astype(BF16), preferred_element_type=F32)
    o_ref[...] = res.astype(o_ref.dtype)
    if chunk_cols is not None:
        oc_ref, tok_ref = rest[0], rest[1]
        jblk, lo, width = chunk_cols

        @pl.when(pl.program_id(1) == jblk)
        def _():
            nchunk = oc_ref.shape[0]
            for q in range(width // LANES):
                tok_ref[q] = res[:, lo + q * LANES:lo + (q + 1) * LANES]
                for t in range(S5_CHUNK):
                    c0 = (q * S5_CHUNK + t) * LANES
                    oc_ref[:, c0:c0 + LANES] = tok_ref[q, pl.ds(t, nchunk, stride=S5_CHUNK), :].astype(oc_ref.dtype)


def _modmm(h, g, shift, scale, w, layer, rows_per_mod, bm, bn, chunk_cols=None):
    r, d = h.shape
    n = w.shape[2]
    bm = min(bm, r)
    mod_idx = lambda i, j: ((i * bm) // rows_per_mod, 0, 0)
    out_specs = pl.BlockSpec((bm, bn), lambda i, j: (i, j))
    out_shape = jax.ShapeDtypeStruct((r, n), BF16)
    scratch = [pltpu.VMEM((bm, d), BF16)]
    cc = None
    if chunk_cols is not None:
        start, width = chunk_cols
        assert start // bn == (start + width - 1) // bn
        cc = (start // bn, start % bn, width)
        out_specs = [out_specs, pl.BlockSpec((bm // S5_CHUNK, S5_CHUNK * width), lambda i, j: (i, 0))]
        out_shape = [out_shape, jax.ShapeDtypeStruct((r // S5_CHUNK, S5_CHUNK * width), BF16)]
        scratch = [pltpu.VMEM((width // LANES, bm, LANES), F32)] + scratch
    return pl.pallas_call(
        functools.partial(_modmm_kernel, chunk_cols=cc),
        grid=(r // bm, n // bn),
        in_specs=[pl.BlockSpec((bm, d), lambda i, j: (i, 0)),
                  pl.BlockSpec((1, d), lambda i, j: (0, 0)),
                  pl.BlockSpec((1, 1, d), mod_idx),
                  pl.BlockSpec((1, 1, d), mod_idx),
                  pl.BlockSpec((None, d, bn), lambda i, j: (layer, 0, j))],
        out_specs=out_specs,
        out_shape=out_shape,
        scratch_shapes=scratch,
        compiler_params=_params(("parallel", "arbitrary")),
        name="modmm",
    )(h, g.reshape(1, d), shift, scale, w)


def _resmm_kernel(*refs, n_in):
    x_refs = refs[:n_in]
    w_ref, h_ref, gate_ref, o_ref = refs[n_in:]
    off = 0
    acc = None
    for x_ref in x_refs:
        k = x_ref.shape[1]
        part = jnp.dot(x_ref[...].astype(BF16), w_ref[off:off + k, :].astype(BF16), preferred_element_type=F32)
        acc = part if acc is None else acc + part
        off += k
    o_ref[...] = h_ref[...] + gate_ref[0] * acc


def _resmm(xs, w, layer, h, gate, rows_per_mod, bm, bn):
    r, n = h.shape
    bm = min(bm, r)
    k = w.shape[1]
    w_mode = pl.Buffered(1) if bn == n else None
    in_specs = [pl.BlockSpec((bm, x.shape[1]), lambda i, j: (i, 0)) for x in xs]
    in_specs += [pl.BlockSpec((None, k, bn), lambda i, j: (layer, 0, j), pipeline_mode=w_mode),
                 pl.BlockSpec((bm, bn), lambda i, j: (i, j)),
                 pl.BlockSpec((1, 1, bn), lambda i, j: ((i * bm) // rows_per_mod, 0, j))]
    return pl.pallas_call(
        functools.partial(_resmm_kernel, n_in=len(xs)),
        grid=(r // bm, n // bn),
        in_specs=in_specs,
        out_specs=pl.BlockSpec((bm, bn), lambda i, j: (i, j)),
        out_shape=jax.ShapeDtypeStruct((r, n), F32),
        compiler_params=_params(("parallel", "arbitrary")),
        name="resmm",
    )(*xs, w, h, gate)


def _shortconv_kernel(x_ref, w_ref, b_ref, o_ref, pad_ref, *, length, chunk):
    cb = x_ref.shape[-1]
    zeros = jnp.zeros((8, cb), F32)
    pad_ref[0:8, :] = zeros
    pad_ref[length + 8:length + 16, :] = zeros
    pad_ref[8:length + 8, :] = x_ref[0].astype(F32)
    w0, w1, w2 = w_ref[0:1, :], w_ref[1:2, :], w_ref[2:3, :]
    b = b_ref[...]
    for c in range(length // chunk):
        r = c * chunk
        o_ref[0, r:r + chunk, :] = (pad_ref[r + 7:r + 7 + chunk, :] * w0 + pad_ref[r + 8:r + 8 + chunk, :] * w1
                                    + pad_ref[r + 9:r + 9 + chunk, :] * w2 + b)


def _shortconv(a, w, b, width):
    bsz, length, _ = a.shape
    cb = 256
    chunk = min(512, length)
    return pl.pallas_call(
        functools.partial(_shortconv_kernel, length=length, chunk=chunk),
        grid=(bsz, width // cb),
        in_specs=[pl.BlockSpec((1, length, cb), lambda i, j: (i, 0, j)),
                  pl.BlockSpec((3, cb), lambda i, j: (0, j)),
                  pl.BlockSpec((1, cb), lambda i, j: (0, j))],
        out_specs=pl.BlockSpec((1, length, cb), lambda i, j: (i, 0, j)),
        out_shape=jax.ShapeDtypeStruct((bsz, length, width), F32),
        scratch_shapes=[pltpu.VMEM((length + 16, cb), F32)],
        compiler_params=_params(("parallel", "parallel")),
        name="hy_shortconv",
    )(a, w, b.reshape(1, width))


def _taps_kernel(z_ref, w1_ref, b1_ref, w2_ref, b2_ref, w3_ref, f_ref, dl_ref, o_ref, *, length):
    z = z_ref[...]
    h = jnp.sin(f_ref[0:1, :] * (jnp.dot(z, w1_ref[...], preferred_element_type=F32, precision=HIGHEST)
                                  + b1_ref[...]))
    h = jnp.sin(f_ref[1:2, :] * (jnp.dot(h, w2_ref[...], preferred_element_type=F32, precision=HIGHEST)
                                  + b2_ref[...]))
    w3 = w3_ref[...]
    h_hi, w_hi = h.astype(BF16), w3.astype(BF16)
    h_lo, w_lo = (h - h_hi.astype(F32)).astype(BF16), (w3 - w_hi.astype(F32)).astype(BF16)
    h = (jnp.dot(h_hi, w_hi, preferred_element_type=F32) + jnp.dot(h_lo, w_hi, preferred_element_type=F32)
         + jnp.dot(h_hi, w_lo, preferred_element_type=F32))
    t = z[:, 0:1]
    h = h * jnp.exp(-t * jnp.abs(dl_ref[...]))
    rb = z.shape[0]
    row = lax.broadcasted_iota(jnp.int32, h.shape, 0) + pl.program_id(0) * rb
    o_ref[...] = jnp.where(row == length, 0.0, h)


def _hyena_taps(length, w1, b1, w2, b2, w3, freq, width):
    n = 2 * length
    hid = w1.shape[1]
    pos = jnp.arange(n)
    idx = jnp.where(pos < length, pos, n - pos).astype(F32)
    t = (idx / max(length - 1, 1))[:, None]
    bands = jnp.linspace(1e-4, HY_BANDS - 1, HY_BANDS, dtype=F32)
    ang = (2.0 * math.pi * idx / length)[:, None] * bands[None]
    emb = 1 + 2 * HY_BANDS
    z = jnp.concatenate([t, jnp.cos(ang), -jnp.sin(ang), jnp.zeros((n, 64 - emb), F32)], axis=-1)
    w1p = jnp.concatenate([w1.astype(F32), jnp.zeros((64 - emb, hid), F32)], axis=0)
    max_decay = math.log(HY_TARGET) / HY_FAST_DECAY
    min_decay = math.log(HY_TARGET) / HY_SLOW_DECAY
    deltas = jnp.tile(jnp.linspace(min_decay, max_decay, width, dtype=F32), HY_ORDER)[None]
    oc = HY_ORDER * width
    rb = min(1024, length)
    half = length // rb
    return pl.pallas_call(
        functools.partial(_taps_kernel, length=length),
        grid=(n // rb,),
        in_specs=[pl.BlockSpec((rb, 64), lambda i: (i, 0)),
                  _const_spec((64, hid)), _const_spec((1, hid)),
                  _const_spec((hid, hid)), _const_spec((1, hid)),
                  pl.BlockSpec((hid, oc), lambda i: (0, i // half)),
                  _const_spec((2, hid)), _const_spec((1, oc))],
        out_specs=pl.BlockSpec((rb, oc), lambda i: (i, 0)),
        out_shape=jax.ShapeDtypeStruct((n, oc), F32),
        compiler_params=_params(("parallel",)),
        name="hy_taps",
    )(z, w1p, b1.reshape(1, hid).astype(F32), w2.astype(F32), b2.reshape(1, hid).astype(F32),
      w3.astype(F32), freq.astype(F32), deltas)


def _bf16(x):
    return jnp.asarray(x).astype(BF16)


@functools.lru_cache(maxsize=None)
def _dft2_consts(length):
    n = 2 * length
    n2 = LANES
    n1 = n // n2
    hf = n1 // 2
    j = np.arange(n2)[:, None, None]
    k1 = np.arange(n1)[None, :, None]
    m1 = np.arange(n1)[None, None, :]
    ph = -2.0 * np.pi * (j * k1 / n + (m1 * k1 % n1) / n1)
    mr, mi = np.cos(ph), np.sin(ph)
    g1 = np.concatenate([np.concatenate([mr[:, :, :hf], -mi[:, :, :hf]], 2),
                         np.concatenate([mi[:, :, :hf], mr[:, :, :hf]], 2)], 1)
    g1f = np.concatenate([mr, mi], 1)
    mrt = np.transpose(mr, (0, 2, 1))[:, :hf] / n
    mit = -np.transpose(mi, (0, 2, 1))[:, :hf] / n
    g1i = np.concatenate([np.concatenate([mrt, -mit], 2),
                          np.concatenate([mit, mrt], 2)], 1)
    a = np.arange(n2)
    ph2 = -2.0 * np.pi * ((a[:, None] * a[None, :]) % n2) / n2
    fr, fi = np.cos(ph2), np.sin(ph2)
    g2 = np.block([[fr, -fi], [fi, fr]])
    g2i = np.block([[fr, fi], [-fi, fr]])
    f32 = lambda m: np.asarray(m, np.float32)
    return dict(g1=f32(g1), g1i=f32(g1i), g2=f32(g2), g2i=f32(g2i), g1f=f32(g1f))


@functools.lru_cache(maxsize=None)
def _dft1_consts(length):
    n = 2 * length
    a = np.arange(n)
    ph = -2.0 * np.pi * ((a[:, None] * a[None, :]) % n) / n
    fr, fi = np.cos(ph), np.sin(ph)
    gf = np.block([[fr[:, :length], -fi[:, :length]], [fi[:, :length], fr[:, :length]]])
    gi = np.block([[fr[:length], fi[:length]], [-fi[:length], fr[:length]]]) / n
    gff = np.concatenate([fr, fi], 0)
    f32 = lambda m: np.asarray(m, np.float32)
    return dict(gf=f32(gf), gi=f32(gi), gff=f32(gff))


def _fspec2_kernel(k_ref, g1_ref, g2_ref, kr_ref, ki_ref, *, n1):
    inv = 1.0 / jnp.sum(jnp.abs(k_ref[...]), axis=0, keepdims=True)

    def s1(j, c):
        x = k_ref[pl.ds(j, n1, stride=LANES), :].astype(BF16)
        a = jnp.dot(g1_ref[j], x, preferred_element_type=F32)
        kr_ref[pl.ds(j, n1, stride=LANES), :] = a[:n1]
        ki_ref[pl.ds(j, n1, stride=LANES), :] = a[n1:]
        return c

    lax.fori_loop(0, LANES, s1, 0, unroll=8)

    def s2(k1, c):
        r0 = pl.multiple_of(k1 * LANES, LANES)
        a = jnp.concatenate([kr_ref[pl.ds(r0, LANES), :], ki_ref[pl.ds(r0, LANES), :]], axis=0).astype(BF16)
        x = jnp.dot(g2_ref[...], a, preferred_element_type=F32) * inv
        kr_ref[pl.ds(r0, LANES), :] = x[:LANES]
        ki_ref[pl.ds(r0, LANES), :] = x[LANES:]
        return c

    lax.fori_loop(0, n1, s2, 0, unroll=8)


def _filter_spectrum2(taps):
    n, oc = taps.shape
    cst = _dft2_consts(n // 2)
    n1 = n // LANES
    cb = LANES
    g1, g2 = _bf16(cst["g1f"]), _bf16(cst["g2"])
    out = jax.ShapeDtypeStruct((n, oc), F32)
    return pl.pallas_call(
        functools.partial(_fspec2_kernel, n1=n1),
        grid=(oc // cb,),
        in_specs=[pl.BlockSpec((n, cb), lambda i: (0, i)), _const_spec(g1.shape), _const_spec(g2.shape)],
        out_specs=[pl.BlockSpec((n, cb), lambda i: (0, i))] * 2,
        out_shape=[out, out],
        compiler_params=_params(("parallel",)),
        name="hy_fspec2",
    )(taps, g1, g2)


def _fspec1_kernel(k_ref, g_ref, kr_ref, ki_ref):
    k = k_ref[...]
    n = k.shape[0]
    inv = 1.0 / jnp.sum(jnp.abs(k), axis=0, keepdims=True)
    x = jnp.dot(g_ref[...], k.astype(BF16), preferred_element_type=F32) * inv
    kr_ref[...] = x[:n]
    ki_ref[...] = x[n:]


def _filter_spectrum1(taps):
    n, oc = taps.shape
    g = _bf16(_dft1_consts(n // 2)["gff"])
    cb = 256
    out = jax.ShapeDtypeStruct((n, oc), F32)
    return pl.pallas_call(
        _fspec1_kernel,
        grid=(oc // cb,),
        in_specs=[pl.BlockSpec((n, cb), lambda i: (0, i)), _const_spec(g.shape)],
        out_specs=[pl.BlockSpec((n, cb), lambda i: (0, i))] * 2,
        out_shape=[out, out],
        compiler_params=_params(("parallel",)),
        name="hy_fspec1",
    )(taps, g)


def _hyconv2_kernel(v_ref, x_ref, kr_ref, ki_ref, bias_ref, g1_ref, g1i_ref, g2_ref, g2i_ref, o_ref,
                    ar_ref, ai_ref, *, n1):
    hf = n1 // 2

    def s1(j, c):
        xa = v_ref[0, pl.ds(j, hf, stride=LANES), :]
        xb = v_ref[1, pl.ds(j, hf, stride=LANES), :]
        x = jnp.concatenate([xa, xb], axis=0).astype(BF16)
        a = jnp.dot(g1_ref[j], x, preferred_element_type=F32)
        ar_ref[pl.ds(j, n1, stride=LANES), :] = a[:n1]
        ai_ref[pl.ds(j, n1, stride=LANES), :] = a[n1:]
        return c

    lax.fori_loop(0, LANES, s1, 0, unroll=8)

    def s2(k1, c):
        r0 = pl.multiple_of(k1 * LANES, LANES)
        a = jnp.concatenate([ar_ref[pl.ds(r0, LANES), :], ai_ref[pl.ds(r0, LANES), :]], axis=0).astype(BF16)
        x = jnp.dot(g2_ref[...], a, preferred_element_type=F32)
        xr, xi = x[:LANES], x[LANES:]
        kr = kr_ref[pl.ds(r0, LANES), :]
        ki = ki_ref[pl.ds(r0, LANES), :]
        y = jnp.concatenate([xr * kr - xi * ki, xr * ki + xi * kr], axis=0).astype(BF16)
        b = jnp.dot(g2i_ref[...], y, preferred_element_type=F32)
        ar_ref[pl.ds(r0, LANES), :] = b[:LANES]
        ai_ref[pl.ds(r0, LANES), :] = b[LANES:]
        return c

    lax.fori_loop(0, n1, s2, 0, unroll=8)

    def s3(j, c):
        b = jnp.concatenate([ar_ref[pl.ds(j, n1, stride=LANES), :], ai_ref[pl.ds(j, n1, stride=LANES), :]],
                            axis=0).astype(BF16)
        y = jnp.dot(g1i_ref[j], b, preferred_element_type=F32)
        o_ref[0, pl.ds(j, hf, stride=LANES), :] = y[:hf]
        o_ref[1, pl.ds(j, hf, stride=LANES), :] = y[hf:]
        return c

    lax.fori_loop(0, LANES, s3, 0, unroll=8)
    bias = bias_ref[...]
    for b in range(2):
        o_ref[b] = x_ref[b] * (o_ref[b] + bias * v_ref[b])


def _hyconv2(va, v_col, xa, x_col, kr, ki, k_col, bias, length):
    bsz = va.shape[0]
    width = bias.shape[-1]
    cst = _dft2_consts(length)
    n = 2 * length
    n1 = n // LANES
    cb = LANES
    ncb = width // cb
    return pl.pallas_call(
        functools.partial(_hyconv2_kernel, n1=n1),
        grid=(ncb, bsz // 2),
        in_specs=[pl.BlockSpec((2, length, cb), lambda c, q: (q, 0, v_col + c)),
                  pl.BlockSpec((2, length, cb), lambda c, q: (q, 0, x_col + c)),
                  pl.BlockSpec((n, cb), lambda c, q: (0, k_col + c), pipeline_mode=pl.Buffered(1)),
                  pl.BlockSpec((n, cb), lambda c, q: (0, k_col + c), pipeline_mode=pl.Buffered(1)),
                  pl.BlockSpec((1, cb), lambda c, q: (0, c)),
                  _const_spec(cst["g1"].shape), _const_spec(cst["g1i"].shape),
                  _const_spec(cst["g2"].shape), _const_spec(cst["g2i"].shape)],
        out_specs=pl.BlockSpec((2, length, cb), lambda c, q: (q, 0, c)),
        out_shape=jax.ShapeDtypeStruct((bsz, length, width), F32),
        scratch_shapes=[pltpu.VMEM((n, cb), F32), pltpu.VMEM((n, cb), F32)],
        compiler_params=_params(("parallel", "arbitrary")),
        name="hy_conv2",
    )(va, xa, kr, ki, bias.reshape(1, width), _bf16(cst["g1"]), _bf16(cst["g1i"]), _bf16(cst["g2"]),
      _bf16(cst["g2i"]))


def _hyconv1_kernel(v_ref, x_ref, kr_ref, ki_ref, bias_ref, gf_ref, gi_ref, o_ref):
    length = v_ref.shape[1]
    n = 2 * length
    x = jnp.concatenate([v_ref[0], v_ref[1]], axis=0).astype(BF16)
    s = jnp.dot(gf_ref[...], x, preferred_element_type=F32)
    sr, si = s[:n], s[n:]
    kr, ki = kr_ref[...], ki_ref[...]
    y = jnp.concatenate([sr * kr - si * ki, sr * ki + si * kr], axis=0).astype(BF16)
    out = jnp.dot(gi_ref[...], y, preferred_element_type=F32)
    bias = bias_ref[...]
    for b in range(2):
        o_ref[b] = x_ref[b] * (out[b * length:(b + 1) * length] + bias * v_ref[b])


def _hyconv1(va, v_col, xa, x_col, kr, ki, k_col, bias, length):
    bsz = va.shape[0]
    width = bias.shape[-1]
    cst = _dft1_consts(length)
    n = 2 * length
    cb = LANES
    return pl.pallas_call(
        _hyconv1_kernel,
        grid=(width // cb, bsz // 2),
        in_specs=[pl.BlockSpec((2, length, cb), lambda c, q: (q, 0, v_col + c)),
                  pl.BlockSpec((2, length, cb), lambda c, q: (q, 0, x_col + c)),
                  pl.BlockSpec((n, cb), lambda c, q: (0, k_col + c)),
                  pl.BlockSpec((n, cb), lambda c, q: (0, k_col + c)),
                  pl.BlockSpec((1, cb), lambda c, q: (0, c)),
                  _const_spec(cst["gf"].shape), _const_spec(cst["gi"].shape)],
        out_specs=pl.BlockSpec((2, length, cb), lambda c, q: (q, 0, c)),
        out_shape=jax.ShapeDtypeStruct((bsz, length, width), F32),
        compiler_params=_params(("parallel", "arbitrary")),
        name="hy_conv1",
    )(va, xa, kr, ki, bias.reshape(1, width), _bf16(cst["gf"]), _bf16(cst["gi"]))


def _hyena(a, conv_w, conv_b, w1, b1, w2, b2, w3, freq, bias, width):
    bsz, length, _ = a.shape
    p = _shortconv(a, conv_w, conv_b, 3 * width)
    taps = _hyena_taps(length, w1, b1, w2, b2, w3, freq, width)
    two_stage = (2 * length) % (LANES * 16) == 0
    kr, ki = (_filter_spectrum2 if two_stage else _filter_spectrum1)(taps)
    conv = _hyconv2 if two_stage else _hyconv1
    ncb = width // LANES
    z = conv(p, 0, p, ncb, kr, ki, 0, bias[0], length)
    return conv(z, 0, p, 2 * ncb, kr, ki, ncb, bias[1], length)


def _ret_kernel(lg_ref, q_ref, k_ref, v_ref, g_ref, s0_ref, o_ref, sfin_ref, sb_ref, *, length, chunk, dk):
    hp = pl.program_id(1)
    nc = length // chunk
    dv = LANES
    row = lax.broadcasted_iota(jnp.int32, (chunk, chunk), 0)
    col = lax.broadcasted_iota(jnp.int32, (chunk, chunk), 1)
    diff = (row - col).astype(F32)
    lane = lax.broadcasted_iota(jnp.int32, (chunk, 2 * dk), 1)
    pos = lax.broadcasted_iota(jnp.int32, (chunk, 2 * dk), 0).astype(F32)
    ones_s = jnp.ones((2 * dk, dv), F32)
    kscale = dk ** -0.5
    tn = (((0,), (0,)), ((), ()))
    nt = (((1,), (1,)), ((), ()))
    hd = []
    for hh in range(2):
        lgf = lg_ref[0, hp * 2 + hh]
        lgb = lg_ref[1, hp * 2 + hh]
        hd.append(dict(
            decay=jnp.where(diff >= 0.0, jnp.exp(jnp.maximum(diff, 0.0) * lgf),
                            jnp.exp(jnp.maximum(-diff, 0.0) * lgb)),
            qmask=(lane >= dk * hh) & (lane < dk * (hh + 1)),
            qf=jnp.exp((pos + 1.0) * lgf), qb=jnp.exp((chunk - pos) * lgb),
            kf=jnp.exp((chunk - 1.0 - pos) * lgf) * kscale, kb=jnp.exp(pos * lgb) * kscale,
            cdf=jnp.exp(ones_s * (chunk * lgf)), cdb=jnp.exp(ones_s * (chunk * lgb)),
            vs=slice(dv * hh, dv * (hh + 1))))

    def bstep(i, states):
        n = nc - 1 - i
        r0 = pl.multiple_of(n * chunk, chunk)
        k = k_ref[0, pl.ds(r0, chunk), :].astype(F32)
        out = []
        for hh, c in enumerate(hd):
            sb_ref[hh, n] = states[hh]
            v = v_ref[0, pl.ds(r0, chunk), c['vs']]
            inc = lax.dot_general((k * c['kb']).astype(BF16), v, tn, preferred_element_type=F32)
            out.append(c['cdb'] * states[hh] + inc)
        return tuple(out)

    fin = lax.fori_loop(0, nc, bstep, (s0_ref[0, 1, 0], s0_ref[0, 1, 1]), unroll=4 if nc % 4 == 0 else 1)
    sfin_ref[0, 1, 0] = fin[0]
    sfin_ref[0, 1, 1] = fin[1]

    def fstep(n, states):
        r0 = pl.multiple_of(n * chunk, chunk)
        qa = q_ref[0, pl.ds(r0, chunk), :].astype(F32)
        k = k_ref[0, pl.ds(r0, chunk), :].astype(F32)
        ks = (k * kscale).astype(BF16)
        out = []
        for hh, c in enumerate(hd):
            q = jnp.where(c['qmask'], qa, 0.0)
            v = v_ref[0, pl.ds(r0, chunk), c['vs']]
            s = lax.dot_general(q.astype(BF16), ks, nt, preferred_element_type=F32)
            y = jnp.dot((s * c['decay']).astype(BF16), v, preferred_element_type=F32)
            y = y + jnp.dot((q * c['qf']).astype(BF16), states[hh].astype(BF16), preferred_element_type=F32)
            y = y + jnp.dot((q * c['qb']).astype(BF16), sb_ref[hh, n].astype(BF16), preferred_element_type=F32)
            y = y * lax.rsqrt(jnp.mean(y * y, axis=-1, keepdims=True) + EPS)
            g = g_ref[0, pl.ds(r0, chunk), c['vs']].astype(F32)
            o_ref[0, pl.ds(r0, chunk), c['vs']] = (jax.nn.silu(g) * y).astype(o_ref.dtype)
            inc = lax.dot_general((k * c['kf']).astype(BF16), v, tn, preferred_element_type=F32)
            out.append(c['cdf'] * states[hh] + inc)
        return tuple(out)

    fin = lax.fori_loop(0, nc, fstep, (s0_ref[0, 0, 0], s0_ref[0, 0, 1]), unroll=4 if nc % 4 == 0 else 1)
    sfin_ref[0, 0, 0] = fin[0]
    sfin_ref[0, 0, 1] = fin[1]


def _retention(a, log_gamma, s0, q_off, dk):
    bsz, length, _ = a.shape
    heads = RET_HEADS
    dv = 2 * dk
    assert dv == LANES
    chunk = min(RET_CHUNK, length)
    qb = q_off // (2 * dk)
    kb = qb + heads // 2
    vb = (q_off + 2 * heads * dk) // (2 * dv)
    gb = vb + heads // 2
    seq = lambda blk, off: pl.BlockSpec((1, length, blk), lambda b, h, lg: (b, 0, off + h))
    st = pl.BlockSpec((1, 2, 2, 2 * dk, dv), lambda b, h, lg: (b, 0, h, 0, 0))
    grid_spec = pltpu.PrefetchScalarGridSpec(
        num_scalar_prefetch=1,
        grid=(bsz, heads // 2),
        in_specs=[seq(2 * dk, qb), seq(2 * dk, kb), seq(2 * dv, vb), seq(2 * dv, gb), st],
        out_specs=[pl.BlockSpec((1, length, 2 * dv), lambda b, h, lg: (b, 0, h)), st],
        scratch_shapes=[pltpu.VMEM((2, length // chunk, 2 * dk, dv), F32)],
    )
    return pl.pallas_call(
        functools.partial(_ret_kernel, length=length, chunk=chunk, dk=dk),
        grid_spec=grid_spec,
        out_shape=[jax.ShapeDtypeStruct((bsz, length, heads * dv), BF16),
                   jax.ShapeDtypeStruct(s0.shape, F32)],
        compiler_params=_params(("parallel", "parallel")),
        name="retention",
    )(log_gamma, a, a, a, a, s0)


@functools.lru_cache(maxsize=None)
def _s5_expanders():
    gl = LANES // S5_GROUP
    t_len = S5_CHUNK
    ex_to = np.zeros((t_len * S5_GROUP, t_len * LANES), np.float32)
    for t in range(t_len):
        for h in range(gl):
            for o in range(S5_GROUP):
                ex_to[t * S5_GROUP + o, t * LANES + h * S5_GROUP + o] = 1.0
    ex_p = np.kron(np.eye(4, dtype=np.float32), np.tile(np.eye(S5_STATE, dtype=np.float32), (1, gl)))
    g_sgi = (np.arange(t_len * LANES) // S5_GROUP) % gl
    g_tho = (np.arange(t_len * LANES) // S5_GROUP) % gl
    g_hp = np.arange(gl * S5_STATE) // S5_STATE
    eq = lambda a, b: (a[:, None] == b[None, :]).astype(np.float32)
    return ex_to, ex_p, eq(g_sgi, g_tho), eq(g_sgi, np.tile(g_hp, 4)), eq(g_hp, g_tho)


def _s5_mats(lam_re, lam_im, log_step, b_re, b_im, c_re, c_im):
    t_len = S5_CHUNK
    lr = jnp.minimum(lam_re.astype(F32), -1e-4)
    li = lam_im.astype(F32)
    step = jnp.exp(log_step.astype(F32))[..., None]
    dr, di = lr * step, li * step
    d = jnp.arange(t_len + 1, dtype=F32)[:, None, None, None]
    mag = jnp.exp(d * dr)
    pr, pi = mag * jnp.cos(d * di), mag * jnp.sin(d * di)
    nr, ni = pr[1] - 1.0, pi[1]
    den = lr * lr + li * li
    cr, ci = (nr * lr + ni * li) / den, (ni * lr - nr * li) / den
    bbr = cr[..., None] * b_re - ci[..., None] * b_im
    bbi = cr[..., None] * b_im + ci[..., None] * b_re
    ctr = jnp.swapaxes(c_re.astype(F32), 2, 3)
    cti = jnp.swapaxes(c_im.astype(F32), 2, 3)
    groups = lr.shape[1]
    gl = LANES // S5_GROUP
    nq = groups // gl
    tw = t_len * S5_GROUP
    tq = lambda e: e.reshape((nq, gl) + e.shape[1:])
    ex_to, ex_p, m_rc, m_rp, m_pc = _s5_expanders()

    def spread(compact, expand, mask):
        return jnp.matmul(compact.astype(BF16), jnp.asarray(expand, BF16),
                          preferred_element_type=BF16) * jnp.asarray(mask, BF16)

    mr = (bbr[..., :, None] * ctr[..., None, :] - bbi[..., :, None] * cti[..., None, :]).reshape(
        2, groups, S5_STATE, S5_GROUP * S5_GROUP)
    mi = (bbr[..., :, None] * cti[..., None, :] + bbi[..., :, None] * ctr[..., None, :]).reshape(
        2, groups, S5_STATE, S5_GROUP * S5_GROUP)
    kern = (jnp.einsum('dxgp,xgpn->xgdn', pr, mr, precision=HIGHEST)
            - jnp.einsum('dxgp,xgpn->xgdn', pi, mi, precision=HIGHEST))
    kern = jnp.transpose(kern.reshape(2, groups, t_len + 1, S5_GROUP, S5_GROUP), (0, 1, 3, 2, 4))
    kf = kern[0, :, :, :t_len].reshape(groups, S5_GROUP, tw)
    kb = kern[1, :, :, :t_len][:, :, ::-1].reshape(groups, S5_GROUP, tw)
    zpad = jnp.zeros((groups, S5_GROUP, tw - S5_GROUP), F32)
    kf = jnp.concatenate([zpad, kf], axis=-1)
    kb = jnp.concatenate([kb, zpad], axis=-1)
    toe = jnp.stack([kf[..., (t_len - 1 - s) * S5_GROUP:(t_len - 1 - s) * S5_GROUP + tw]
                     + kb[..., (t_len - 1 - s) * S5_GROUP:(t_len - 1 - s) * S5_GROUP + tw]
                     for s in range(t_len)], axis=1)
    rows_sgi = lambda e: jnp.transpose(tq(e), (0, 2, 1, 3, 4)).reshape(nq, t_len * LANES, e.shape[-1])
    w1t = spread(rows_sgi(toe), ex_to, m_rc)
    btr, bti = jnp.swapaxes(bbr, 2, 3), jnp.swapaxes(bbi, 2, 3)

    def e_part(x, order):
        pw_r = jnp.transpose(pr[order, x], (1, 0, 2))[:, :, None, :]
        pw_i = jnp.transpose(pi[order, x], (1, 0, 2))[:, :, None, :]
        br, bi = btr[x][:, None], bti[x][:, None]
        return [rows_sgi(e) for e in (pw_r * br - pw_i * bi, pw_r * bi + pw_i * br)]

    prt, pit = jnp.transpose(pr, (1, 2, 3, 0)), jnp.transpose(pi, (1, 2, 3, 0))

    def q_part(x, order):
        pw_r, pw_i = prt[x][..., order][..., None], pit[x][..., order][..., None]
        c_r, c_i = ctr[x][:, :, None, :], cti[x][:, :, None, :]
        return [spread(tq(e).reshape(nq, gl * S5_STATE, tw), ex_to, m_pc)
                for e in (c_r * pw_r - c_i * pw_i, -(c_r * pw_i + c_i * pw_r))]

    ecat = jnp.concatenate(e_part(0, jnp.arange(t_len - 1, -1, -1)) + e_part(1, jnp.arange(t_len)), axis=2)
    w1e = spread(ecat, ex_p, m_rp)
    w2 = jnp.concatenate(q_part(0, jnp.arange(1, t_len + 1)) + q_part(1, jnp.arange(t_len, 0, -1)), axis=1)
    pl2 = lambda e: e.reshape(nq, gl * S5_STATE)
    lam_t = jnp.stack([pl2(pr[t_len, 0]), pl2(pi[t_len, 0]), pl2(pr[t_len, 1]), pl2(pi[t_len, 1])], axis=1)
    return w1t.astype(BF16), w1e.astype(BF16), w2.astype(BF16), lam_t


def _s5_kernel(u_ref, w1t_ref, w1e_ref, w2_ref, lam_ref, s0_ref, y_ref, sfin_ref, r_ref, *, nc):
    sw = (LANES // S5_GROUP) * S5_STATE
    yw = S5_CHUNK * LANES
    nb = u_ref.shape[0]
    u = u_ref[...].reshape(nb * nc, yw)
    r_ref[:, 0:yw] = jnp.dot(u, w1t_ref[0], preferred_element_type=F32)
    r_ref[:, yw:yw + 4 * sw] = jnp.dot(u, w1e_ref[0], preferred_element_type=F32)
    lfr, lfi, lbr, lbi = (lam_ref[0, i:i + 1, :] for i in range(4))
    cols = [slice(yw + i * sw, yw + (i + 1) * sw) for i in range(4)]
    rid = lax.broadcasted_iota(jnp.int32, (SUBLANES, sw), 0)
    nblk = nc // SUBLANES

    def block(kb, carry):
        out = []
        for b in range(nb):
            fr, fi, br, bi = carry[4 * b:4 * b + 4]
            rf = pl.ds(pl.multiple_of(b * nc + kb * SUBLANES, SUBLANES), SUBLANES)
            rb = pl.ds(pl.multiple_of(b * nc + (nblk - 1 - kb) * SUBLANES, SUBLANES), SUBLANES)
            efr, efi = r_ref[rf, cols[0]], r_ref[rf, cols[1]]
            ebr, ebi = r_ref[rb, cols[2]], r_ref[rb, cols[3]]
            xfr, xfi, xbr, xbi = efr, efi, ebr, ebi
            for s in range(SUBLANES):
                xfr = jnp.where(rid == s, fr, xfr)
                xfi = jnp.where(rid == s, fi, xfi)
                fr, fi = (lfr * fr - lfi * fi + efr[s:s + 1], lfr * fi + lfi * fr + efi[s:s + 1])
                z = SUBLANES - 1 - s
                xbr = jnp.where(rid == z, br, xbr)
                xbi = jnp.where(rid == z, bi, xbi)
                br, bi = (lbr * br - lbi * bi + ebr[z:z + 1], lbr * bi + lbi * br + ebi[z:z + 1])
            r_ref[rf, cols[0]] = xfr
            r_ref[rf, cols[1]] = xfi
            r_ref[rb, cols[2]] = xbr
            r_ref[rb, cols[3]] = xbi
            out += [fr, fi, br, bi]
        return tuple(out)

    fin = lax.fori_loop(0, nblk, block, tuple(s0_ref[b, 0, i:i + 1, :] for b in range(nb) for i in range(4)))
    for b in range(nb):
        for i in range(4):
            sfin_ref[b, 0, i:i + 1, :] = fin[4 * b + i]
    xin = r_ref[:, yw:yw + 4 * sw].astype(BF16)
    y = r_ref[:, 0:yw] + jnp.dot(xin, w2_ref[0], preferred_element_type=F32)
    y_ref[...] = y.reshape(nb, nc, yw)


def _s5(uc, w1t, w1e, w2, lam_t, layer, s0):
    bsz, nc, cw = uc.shape
    nq = w1t.shape[1]
    tw = cw // nq
    sw = lam_t.shape[-1]
    nb = max(1, min(bsz, 512 // nc))
    assert bsz % nb == 0
    st = pl.BlockSpec((nb, 1, 4, sw), lambda q, b: (b, q, 0, 0))
    single = pl.Buffered(1)
    return pl.pallas_call(
        functools.partial(_s5_kernel, nc=nc),
        grid=(nq, bsz // nb),
        in_specs=[pl.BlockSpec((nb, nc, tw), lambda q, b: (b, 0, q)),
                  pl.BlockSpec((None, 1) + w1t.shape[2:], lambda q, b: (layer, q, 0, 0), pipeline_mode=single),
                  pl.BlockSpec((None, 1) + w1e.shape[2:], lambda q, b: (layer, q, 0, 0), pipeline_mode=single),
                  pl.BlockSpec((None, 1) + w2.shape[2:], lambda q, b: (layer, q, 0, 0), pipeline_mode=single),
                  pl.BlockSpec((None, 1, 4, sw), lambda q, b: (layer, q, 0, 0)),
                  st],
        out_specs=[pl.BlockSpec((nb, nc, tw), lambda q, b: (b, 0, q)), st],
        out_shape=[jax.ShapeDtypeStruct((bsz, nc, cw), F32), jax.ShapeDtypeStruct(s0.shape, F32)],
        scratch_shapes=[pltpu.VMEM((nb * nc, w1t.shape[3] + w1e.shape[3]), F32)],
        compiler_params=_params(("parallel", "arbitrary")),
        name="s5_scan",
    )(uc, w1t, w1e, w2, lam_t, s0)


def _s5out_kernel(y_ref, u_ref, d_ref, w_ref, b_ref, o_ref, tok_ref):
    nchunk = y_ref.shape[0]
    nq = tok_ref.shape[0]
    for q in range(nq):
        for t in range(S5_CHUNK):
            c0 = (q * S5_CHUNK + t) * LANES
            tok_ref[q, pl.ds(t, nchunk, stride=S5_CHUNK), :] = y_ref[:, c0:c0 + LANES]
    y = jnp.concatenate([tok_ref[q] for q in range(nq)], axis=-1)
    z = jax.nn.gelu(y + d_ref[...] * u_ref[...].astype(F32))
    gate = jnp.dot(z.astype(BF16), w_ref[...], preferred_element_type=F32) + b_ref[...]
    o_ref[...] = (z * jax.nn.sigmoid(gate)).astype(o_ref.dtype)


def _s5_out(yc, a, u_col, d, glu_w, glu_b):
    width = d.shape[-1]
    r = a.shape[0]
    bm = min(1024, r)
    return pl.pallas_call(
        _s5out_kernel,
        grid=(r // bm,),
        in_specs=[pl.BlockSpec((bm // S5_CHUNK, S5_CHUNK * width), lambda i: (i, 0)),
                  pl.BlockSpec((bm, width), lambda i: (i, u_col)),
                  _const_spec((1, width)), _const_spec((width, width)), _const_spec((1, width))],
        out_specs=pl.BlockSpec((bm, width), lambda i: (i, 0)),
        out_shape=jax.ShapeDtypeStruct((r, width), BF16),
        scratch_shapes=[pltpu.VMEM((width // LANES, bm, LANES), F32)],
        compiler_params=_params(("parallel",)),
        name="s5_out",
    )(yc, a, d.reshape(1, width).astype(F32), glu_w.astype(BF16), glu_b.reshape(1, width).astype(F32))


def _ffnact_kernel(*refs, rows, width, chunk):
    halo = rows > 1
    if halo:
        g_ref, ga_ref, gb_ref, v_ref, cw_ref, cb_ref, o_ref, pad_ref = refs
    else:
        g_ref, v_ref, cw_ref, cb_ref, o_ref = refs
    i = pl.program_id(0)
    bm, fk = g_ref.shape

    if halo:
        per_img = (rows * width) // bm
        top = (i % per_img) == 0
        bottom = (i % per_img) == per_img - 1
        pad_ref[0:width, :] = jnp.where(top, 0.0, ga_ref[...].astype(F32))
        pad_ref[width:width + bm, :] = g_ref[...].astype(F32)
        pad_ref[width + bm:2 * width + bm, :] = jnp.where(bottom, 0.0, gb_ref[...].astype(F32))
    col = lax.broadcasted_iota(jnp.int32, (chunk, LANES), 0) % width
    not_first = col != 0
    not_last = col != width - 1
    bias = cb_ref[...]
    wt = [cw_ref[t:t + 1, :] for t in range(9)]

    def body(c, carry):
        r = pl.multiple_of(c * chunk, chunk)
        for lb in range(fk // LANES):
            ls = slice(lb * LANES, (lb + 1) * LANES)
            w9 = [t[:, ls] for t in wt]
            if halo:
                up = pad_ref[pl.ds(r, chunk), ls]
                mid = pad_ref[pl.ds(width + r, chunk), ls]
                dn = pad_ref[pl.ds(2 * width + r, chunk), ls]
                v0, v1, v2 = (up * w9[dw] + mid * w9[3 + dw] + dn * w9[6 + dw] for dw in range(3))
            else:
                mid = g_ref[pl.ds(r, chunk), ls].astype(F32)
                v0, v1, v2 = (mid * w9[3 + dw] for dw in range(3))
            left = jnp.where(not_first, pltpu.roll(v0, 1, 0), 0.0)
            right = jnp.where(not_last, pltpu.roll(v2, chunk - 1, 0), 0.0)
            gate = jax.nn.gelu(v1 + left + right + bias[:, ls])
            o_ref[pl.ds(r, chunk), ls] = (gate * v_ref[pl.ds(r, chunk), ls].astype(F32)).astype(o_ref.dtype)
        return carry

    lax.fori_loop(0, bm // chunk, body, 0, unroll=4 if (bm // chunk) % 4 == 0 else 1)


def _ffn_act(gv, conv_w, conv_b, rows, width):
    r, f2 = gv.shape
    dff = f2 // 2
    bm = min(1024, r)
    fk = 512
    nk = dff // fk
    chunk = width
    assert bm % chunk == 0 and ((rows * width) % bm == 0 if rows > 1 else bm % width == 0)
    in_specs = [pl.BlockSpec((bm, fk), lambda i, k: (i, k))]
    args = [gv]
    scratch = []
    if rows > 1:
        per = bm // width
        last = r // width - 1
        in_specs += [pl.BlockSpec((width, fk), lambda i, k: (jnp.maximum(i * per - 1, 0), k)),
                     pl.BlockSpec((width, fk), lambda i, k: (jnp.minimum((i + 1) * per, last), k))]
        args += [gv, gv]
        scratch.append(pltpu.VMEM((bm + 2 * width, fk), F32))
    in_specs += [pl.BlockSpec((bm, fk), lambda i, k: (i, nk + k)),
                 pl.BlockSpec((9, fk), lambda i, k: (0, k)),
                 pl.BlockSpec((1, fk), lambda i, k: (0, k))]
    args += [gv, conv_w.reshape(9, dff).astype(F32), conv_b.reshape(1, dff).astype(F32)]
    return pl.pallas_call(
        functools.partial(_ffnact_kernel, rows=rows, width=width, chunk=chunk),
        grid=(r // bm, nk),
        in_specs=in_specs,
        out_specs=pl.BlockSpec((bm, fk), lambda i, k: (i, k)),
        out_shape=jax.ShapeDtypeStruct((r, dff), BF16),
        scratch_shapes=scratch,
        compiler_params=_params(("parallel", "parallel")),
        name="ffn_act",
    )(*args)


def _rms_kernel(x_ref, g_ref, o_ref):
    x = x_ref[...]
    o_ref[...] = x * lax.rsqrt(jnp.mean(x * x, axis=-1, keepdims=True) + EPS) * g_ref[...]


def _rmsnorm(x, g):
    r, d = x.shape
    bm = min(1024, r)
    return pl.pallas_call(
        _rms_kernel,
        grid=(r // bm,),
        in_specs=[pl.BlockSpec((bm, d), lambda i: (i, 0)), _const_spec((1, d))],
        out_specs=pl.BlockSpec((bm, d), lambda i: (i, 0)),
        out_shape=jax.ShapeDtypeStruct((r, d), F32),
        compiler_params=_params(("parallel",)),
        name="final_norm",
    )(x, g.reshape(1, d).astype(F32))


def _mixer(h, mods, i, p, states, img_rows, img_width, full):
    bsz, length, d = h.shape
    r = bsz * length
    rpm = r // mods.shape[0]
    width = d // 4
    dk = d // 2 // RET_HEADS // 2
    h2 = h.reshape(r, d)
    q_off = 3 * width
    u_off = q_off + 2 * RET_HEADS * dk + 2 * (d // 2)
    a, uc = _modmm(h2, p['norm1_g'][i], mods[:, 0], mods[:, 1], p['w_in'], i, rpm, 1024, 1024,
                   chunk_cols=(u_off, width))
    a3 = a.reshape(bsz, length, -1)
    ret, ret_fin = _retention(a3, p['log_gamma'][i], states[0], q_off, dk)
    y5, s5_fin = _s5(uc.reshape(bsz, length // S5_CHUNK, -1), *p['s5_mats'], i, states[1])
    if not full:
        return None, (ret_fin, s5_fin)
    hy = _hyena(a3, p['hy_conv_w'][i], p['hy_conv_b'][i], p['hy_w1'][i], p['hy_b1'][i], p['hy_w2'][i],
                p['hy_b2'][i], p['hy_w3'][i], p['hy_freq'][i], p['hy_bias'][i], width)
    s5o = _s5_out(y5.reshape(r // S5_CHUNK, -1), a, u_off // width, p['s5_d'][i], p['s5_glu_w'][i],
                  p['s5_glu_b'][i])
    h2 = _resmm([hy.reshape(r, width), ret.reshape(r, d // 2), s5o], p['w_out'], i, h2, mods[:, 2], rpm, 512, d)
    gv = _modmm(h2, p['norm2_g'][i], mods[:, 3], mods[:, 4], p['ffn_w_up'], i, rpm, 1024, 1024)
    act = _ffn_act(gv, p['ffn_conv_w'][i], p['ffn_conv_b'][i], img_rows, img_width)
    h2 = _resmm([act], p['ffn_w_down'], i, h2, mods[:, 5], rpm, 1024, 512)
    return h2.reshape(bsz, length, d), (ret_fin, s5_fin)


def kernel(x, c, ctx, c_ctx, ada_w, ada_b, norm1_g, w_in, hy_conv_w, hy_conv_b, hy_w1, hy_b1, hy_w2, hy_b2,
           hy_w3, hy_freq, hy_bias, ret_decay, s5_lam_re, s5_lam_im, s5_log_step, s5_b_re, s5_b_im, s5_c_re,
           s5_c_im, s5_d, s5_glu_w, s5_glu_b, w_out, norm2_g, ffn_w_up, ffn_conv_w, ffn_conv_b, ffn_w_down,
           norm_f):
    bsz, length, d = x.shape
    depth = ada_w.shape[0]
    ctx_len = ctx.shape[1]
    dk = d // 2 // RET_HEADS // 2
    pairs = d // 4 // (2 * S5_GROUP)

    cc = jnp.concatenate([c_ctx[None], c, jnp.zeros((8 - 1 - bsz, d), F32)], axis=0)
    mods = _modulation(cc, ada_w, ada_b).reshape(depth, 8, N_MOD, 1, d)
    p = dict(norm1_g=norm1_g, norm2_g=norm2_g, hy_conv_w=hy_conv_w, hy_conv_b=hy_conv_b, hy_w1=hy_w1,
             hy_b1=hy_b1, hy_w2=hy_w2, hy_b2=hy_b2, hy_w3=hy_w3, hy_freq=hy_freq, hy_bias=hy_bias,
             s5_d=s5_d, s5_glu_w=s5_glu_w, s5_glu_b=s5_glu_b, ffn_conv_w=ffn_conv_w, ffn_conv_b=ffn_conv_b,
             w_in=w_in, w_out=w_out, ffn_w_up=ffn_w_up,
             ffn_w_down=ffn_w_down.astype(BF16),
             log_gamma=-jnp.exp(ret_decay.astype(F32)),
             s5_mats=jax.vmap(_s5_mats)(s5_lam_re, s5_lam_im, s5_log_step, s5_b_re, s5_b_im, s5_c_re, s5_c_im))
    zero_states = (jnp.zeros((bsz, 2, RET_HEADS, 2 * dk, 2 * dk), F32),
                   jnp.zeros((bsz, d // 4 // LANES, 4, (LANES // S5_GROUP) * S5_STATE), F32))
    h_lat, h_ctx = x, ctx
    for i in range(depth):
        last = i == depth - 1
        h_ctx, ctx_states = _mixer(h_ctx, mods[i, 0:1], i, p, zero_states, 1, ctx_len, not last)
        h_lat, _ = _mixer(h_lat, mods[i, 1:1 + bsz], i, p, ctx_states, length // GRID_W, GRID_W, True)
    return _rmsnorm(h_lat.reshape(bsz * length, d), norm_f).reshape(bsz, length, d)
```
